```python
import math
import jax, jax.numpy as jnp
from jax import lax
import numpy as np

D_MODEL = 1024
BATCH = 4
SEQ = 8192
DEPTH = 1

N_HEADS = 16
HEAD_DIM = 64
N_KV_GROUPS = 4
HEADS_PER_GROUP = N_HEADS // N_KV_GROUPS
CMP_BLOCK = 32
CMP_STRIDE = 16
CMP_HIDDEN = 256
SEL_BLOCK = 64
N_SELECT = 16
WINDOW = 512
NSA_Q_BLOCK = 64
FORCE = 1e4
CONV_DIM = D_MODEL
CONV_WIDTH = 3
N_MEM = 256
MEM_HEADS = 4
MEM_HEAD_DIM = D_MODEL // MEM_HEADS
REL_BUCKETS = 32
REL_MAX_DIST = 128
D_FF = 2816
EPS = 1e-6
NEG = -1e30

NSA_WIDTH = N_HEADS * HEAD_DIM
KV_WIDTH = N_KV_GROUPS * HEAD_DIM
MEM_WIDTH = MEM_HEADS * MEM_HEAD_DIM
IN_SPLITS = (NSA_WIDTH, 6 * KV_WIDTH, 3 * N_HEADS, 3 * CONV_DIM, MEM_WIDTH, 3 * D_MODEL)
IN_WIDTH = NSA_WIDTH + 6 * KV_WIDTH + 3 * N_HEADS + 3 * CONV_DIM + MEM_WIDTH + 3 * D_MODEL

kernel_name = "hybrid_nsa_shortconv_memory_macaron"


def rmsnorm(x, g):
    xf = x.astype(jnp.float32)
    y = xf * lax.rsqrt(jnp.mean(xf * xf, axis=-1, keepdims=True) + EPS)
    return (y * g.astype(jnp.float32)).astype(x.dtype)


def swiglu_ffn(h, w_in, w_out):
    a, b = jnp.split(h @ w_in, 2, axis=-1)
    return (jax.nn.silu(a) * b) @ w_out


def masked_softmax(s, valid):
    return jax.nn.softmax(jnp.where(valid, s.astype(jnp.float32), NEG), axis=-1)


def rel_bucket(dist):
    n = jnp.maximum(dist, 0)
    max_exact = REL_BUCKETS // 2
    nf = jnp.maximum(n, 1).astype(jnp.float32)
    large = max_exact + (jnp.log(nf / max_exact) / math.log(REL_MAX_DIST / max_exact)
                         * (REL_BUCKETS - max_exact)).astype(jnp.int32)
    large = jnp.minimum(large, REL_BUCKETS - 1)
    return jnp.where(n < max_exact, n, large)


def compress(kv, pe, w1, w2):
    b, s, g, d = kv.shape
    ratio = CMP_BLOCK // CMP_STRIDE
    n_chunks = s // CMP_STRIDE
    chunks = kv.reshape(b, n_chunks, CMP_STRIDE, g, d)
    blocks = jnp.concatenate([chunks[:, r:n_chunks - ratio + 1 + r] for r in range(ratio)], axis=2)
    blocks = blocks + pe[None, None, :, None, :]
    n_cmp = blocks.shape[1]
    flat = blocks.transpose(0, 1, 3, 2, 4).reshape(b, n_cmp, g, CMP_BLOCK * d)
    return jax.nn.silu(flat @ w1) @ w2


def cmp_to_sel_matrix(n_cmp, n_sel):
    cs = jnp.arange(n_cmp)[:, None] * CMP_STRIDE
    ss = jnp.arange(n_sel)[None, :] * SEL_BLOCK
    ov = jnp.maximum(jnp.minimum(cs + CMP_BLOCK, ss + SEL_BLOCK) - jnp.maximum(cs, ss), 0)
    return ov.astype(jnp.float32) / CMP_BLOCK


def nsa_attention(q, k_cmp, v_cmp, k_slc, v_slc, k_win, v_win, gates, rel_bias):
    b, s, h, d = q.shape
    g, r = N_KV_GROUPS, HEADS_PER_GROUP
    n_cmp = k_cmp.shape[1]
    n_sel = s // SEL_BLOCK
    top = min(N_SELECT, n_sel)
    scale = d ** -0.5
    qg = q.reshape(b, s, g, r, d)
    gg = gates.reshape(b, s, g, r, 3)
    bias_grk = rel_bias.reshape(REL_BUCKETS, g, r).transpose(1, 0, 2)
    sel_map = cmp_to_sel_matrix(n_cmp, n_sel)
    cmp_end = jnp.arange(n_cmp) * CMP_STRIDE + CMP_BLOCK - 1
    ks_blocks = k_slc.reshape(b, n_sel, SEL_BLOCK, g, d).transpose(0, 3, 1, 2, 4)
    vs_blocks = v_slc.reshape(b, n_sel, SEL_BLOCK, g, d).transpose(0, 3, 1, 2, 4)
    pad = ((0, 0), (WINDOW, 0), (0, 0), (0, 0))
    kw_pad = jnp.pad(k_win, pad)
    vw_pad = jnp.pad(v_win, pad)
    bi = jnp.arange(b)[:, None, None, None]
    gi = jnp.arange(g)[None, :, None, None]

    def dense_bias(dist):
        qn, kn = dist.shape
        return rel_bias[rel_bucket(dist)].reshape(qn, kn, g, r).transpose(2, 3, 0, 1)

    def block(i):
        s0 = i * NSA_Q_BLOCK
        t = s0 + jnp.arange(NSA_Q_BLOCK)
        qb = lax.dynamic_slice_in_dim(qg, s0, NSA_Q_BLOCK, axis=1)
        gb = lax.dynamic_slice_in_dim(gg, s0, NSA_Q_BLOCK, axis=1)
        valid_c = cmp_end[None, :] <= t[:, None]
        s_c = jnp.einsum('bqgrd,bcgd->bgrqc', qb, k_cmp) * scale + dense_bias(t[:, None] - cmp_end[None, :])
        p_c = masked_softmax(s_c, valid_c) * jnp.any(valid_c, axis=-1)[:, None].astype(jnp.float32)
        o_c = jnp.einsum('bgrqc,bcgd->bqgrd', p_c.astype(v_cmp.dtype), v_cmp)
        imp = jnp.einsum('bgrqc,cn->bgqn', p_c, sel_map)
        blk = jnp.arange(n_sel)[None, :]
        cur = (t // SEL_BLOCK)[:, None]
        valid_b = blk <= cur
        forced = (blk == 0) | (blk == cur) | (blk == cur - 1)
        score = jnp.where(valid_b, imp + jnp.where(forced, FORCE, 0.0), -FORCE)
        _, idx = lax.top_k(score, top)
        ks = ks_blocks[bi, gi, idx].reshape(b, g, NSA_Q_BLOCK, top * SEL_BLOCK, d)
        vs = vs_blocks[bi, gi, idx].reshape(b, g, NSA_Q_BLOCK, top * SEL_BLOCK, d)
        pos = (idx[..., None] * SEL_BLOCK + jnp.arange(SEL_BLOCK)).reshape(b, g, NSA_Q_BLOCK, top * SEL_BLOCK)
        dist_s = t[None, None, :, None] - pos
        bias_s = bias_grk[gi, rel_bucket(dist_s)].transpose(0, 1, 4, 2, 3)
        s_s = jnp.einsum('bqgrd,bgqtd->bgrqt', qb, ks) * scale + bias_s
        p_s = masked_softmax(s_s, (dist_s >= 0)[:, :, None])
        o_s = jnp.einsum('bgrqt,bgqtd->bqgrd', p_s.astype(vs.dtype), vs)
        kw = lax.dynamic_slice_in_dim(kw_pad, s0, NSA_Q_BLOCK + WINDOW, axis=1)
        vw = lax.dynamic_slice_in_dim(vw_pad, s0, NSA_Q_BLOCK + WINDOW, axis=1)
        kp = s0 - WINDOW + jnp.arange(NSA_Q_BLOCK + WINDOW)
        dist_w = t[:, None] - kp[None, :]
        valid_w = (kp[None, :] >= 0) & (dist_w >= 0) & (dist_w < WINDOW)
        s_w = jnp.einsum('bqgrd,bkgd->bgrqk', qb, kw) * scale + dense_bias(dist_w)
        p_w = masked_softmax(s_w, valid_w)
        o_w = jnp.einsum('bgrqk,bkgd->bqgrd', p_w.astype(vw.dtype), vw)
        return gb[..., 0:1] * o_c + gb[..., 1:2] * o_s + gb[..., 2:3] * o_w

    out = lax.map(block, jnp.arange(s // NSA_Q_BLOCK))
    return out.transpose(1, 0, 2, 3, 4, 5).reshape(b, s, h * d)


def short_gated_conv(conv_in, conv_w, conv_b):
    gate_b, gate_c, x_in = jnp.split(conv_in, 3, axis=-1)
    u = gate_c * x_in
    y = lax.conv_general_dilated(u, conv_w[:, None, :], window_strides=(1,),
                                 padding=((CONV_WIDTH - 1, 0),),
                                 dimension_numbers=('NWC', 'WIO', 'NWC'),
                                 feature_group_count=u.shape[-1])
    return gate_b * (y + conv_b)


def memory_attention(q_mem, mem, mem_norm_g, w_mem_kv, q_g, k_g):
    b, s, _ = q_mem.shape
    m = mem.shape[1]
    km, vm = jnp.split(rmsnorm(mem, mem_norm_g) @ w_mem_kv, 2, axis=-1)
    km = rmsnorm(km.reshape(b, m, MEM_HEADS, MEM_HEAD_DIM), k_g)
    vm = vm.reshape(b, m, MEM_HEADS, MEM_HEAD_DIM)
    qm = rmsnorm(q_mem.reshape(b, s, MEM_HEADS, MEM_HEAD_DIM), q_g)
    sm = jnp.einsum('bshd,bmhd->bhsm', qm, km) * MEM_HEAD_DIM ** -0.5
    pm = jax.nn.softmax(sm.astype(jnp.float32), axis=-1).astype(vm.dtype)
    return jnp.einsum('bhsm,bmhd->bshd', pm, vm).reshape(b, s, MEM_WIDTH)


def hybrid_layer(x, mem, ffn1_norm_g, ffn1_w_in, ffn1_w_out, mix_norm_g, w_in, q_norm_g, k_norm_g,
                 cmp_pe_k, cmp_w1_k, cmp_w2_k, cmp_pe_v, cmp_w1_v, cmp_w2_v, conv_w, conv_b,
                 mem_norm_g, w_mem_kv, mem_q_norm_g, mem_k_norm_g, w_out,
                 ffn2_norm_g, ffn2_w_in, ffn2_w_out, rel_bias):
    b, s, _ = x.shape
    x = x + 0.5 * swiglu_ffn(rmsnorm(x, ffn1_norm_g), ffn1_w_in, ffn1_w_out)
    h = rmsnorm(x, mix_norm_g)
    split_points = np.cumsum(IN_SPLITS)[:-1].tolist()
    q, kv, nsa_g, conv_in, q_mem, merge_g = jnp.split(h @ w_in, split_points, axis=-1)
    q = rmsnorm(q.reshape(b, s, N_HEADS, HEAD_DIM), q_norm_g)
    kc, vc, ks, vs, kw, vw = [t.reshape(b, s, N_KV_GROUPS, HEAD_DIM) for t in jnp.split(kv, 6, axis=-1)]
    k_cmp = rmsnorm(compress(kc, cmp_pe_k, cmp_w1_k, cmp_w2_k), k_norm_g)
    v_cmp = compress(vc, cmp_pe_v, cmp_w1_v, cmp_w2_v)
    gates = jax.nn.sigmoid(nsa_g.astype(jnp.float32)).astype(x.dtype).reshape(b, s, N_HEADS, 3)
    o_nsa = nsa_attention(q, k_cmp, v_cmp, rmsnorm(ks, k_norm_g), vs, rmsnorm(kw, k_norm_g), vw,
                          gates, rel_bias)
    o_conv = short_gated_conv(conv_in, conv_w, conv_b)
    o_mem = memory_attention(q_mem, mem, mem_norm_g, w_mem_kv, mem_q_norm_g, mem_k_norm_g)
    g_nsa, g_conv, g_mem = jnp.split(jax.nn.sigmoid(merge_g.astype(jnp.float32)).astype(x.dtype), 3, axis=-1)
    x = x + (g_nsa * o_nsa + g_conv * o_conv + g_mem * o_mem) @ w_out
    x = x + 0.5 * swiglu_ffn(rmsnorm(x, ffn2_norm_g), ffn2_w_in, ffn2_w_out)
    return x


def setup_inputs(seed: int = 0) -> dict:
    key = jax.random.key(seed)
    keys = iter(jax.random.split(key, 32))
    L = DEPTH

    def nrm(shape, scale):
        return jax.random.normal(next(keys), shape, jnp.float32) * scale

    def w(shape, fan_in):
        return nrm(shape, fan_in ** -0.5)

    def gain(shape):
        return 1.0 + nrm(shape, 0.05)

    return {
        "x": nrm((BATCH, SEQ, D_MODEL), 1.0),
        "mem": nrm((BATCH, N_MEM, D_MODEL), 1.0),
        "ffn1_norm_g": gain((L, D_MODEL)),
        "ffn1_w_in": w((L, D_MODEL, 2 * D_FF), D_MODEL),
        "ffn1_w_out": w((L, D_FF, D_MODEL), D_FF),
        "mix_norm_g": gain((L, D_MODEL)),
        "w_in": w((L, D_MODEL, IN_WIDTH), D_MODEL),
        "q_norm_g": gain((L, HEAD_DIM)),
        "k_norm_g": gain((L, HEAD_DIM)),
        "cmp_pe_k": nrm((L, CMP_BLOCK, HEAD_DIM), 0.1),
        "cmp_w1_k": w((L, CMP_BLOCK * HEAD_DIM, CMP_HIDDEN), CMP_BLOCK * HEAD_DIM),
        "cmp_w2_k": w((L, CMP_HIDDEN, HEAD_DIM), CMP_HIDDEN),
        "cmp_pe_v": nrm((L, CMP_BLOCK, HEAD_DIM), 0.1),
        "cmp_w1_v": w((L, CMP_BLOCK * HEAD_DIM, CMP_HIDDEN), CMP_BLOCK * HEAD_DIM),
        "cmp_w2_v": w((L, CMP_HIDDEN, HEAD_DIM), CMP_HIDDEN),
        "conv_w": w((L, CONV_WIDTH, CONV_DIM), CONV_WIDTH),
        "conv_b": nrm((L, CONV_DIM), 0.02),
        "mem_norm_g": gain((L, D_MODEL)),
        "w_mem_kv": w((L, D_MODEL, 2 * MEM_WIDTH), D_MODEL),
        "mem_q_norm_g": gain((L, MEM_HEAD_DIM)),
        "mem_k_norm_g": gain((L, MEM_HEAD_DIM)),
        "w_out": w((L, D_MODEL, D_MODEL), D_MODEL),
        "ffn2_norm_g": gain((L, D_MODEL)),
        "ffn2_w_in": w((L, D_MODEL, 2 * D_FF), D_MODEL),
        "ffn2_w_out": w((L, D_FF, D_MODEL), D_FF),
        "rel_bias": nrm((REL_BUCKETS, N_HEADS), 0.5),
    }


def reference(x, mem, ffn1_norm_g, ffn1_w_in, ffn1_w_out, mix_norm_g, w_in, q_norm_g, k_norm_g,
              cmp_pe_k, cmp_w1_k, cmp_w2_k, cmp_pe_v, cmp_w1_v, cmp_w2_v, conv_w, conv_b,
              mem_norm_g, w_mem_kv, mem_q_norm_g, mem_k_norm_g, w_out,
              ffn2_norm_g, ffn2_w_in, ffn2_w_out, rel_bias):
    for l in range(DEPTH):
        x = hybrid_layer(x, mem, ffn1_norm_g[l], ffn1_w_in[l], ffn1_w_out[l], mix_norm_g[l], w_in[l],
                         q_norm_g[l], k_norm_g[l], cmp_pe_k[l], cmp_w1_k[l], cmp_w2_k[l],
                         cmp_pe_v[l], cmp_w1_v[l], cmp_w2_v[l], conv_w[l], conv_b[l],
                         mem_norm_g[l], w_mem_kv[l], mem_q_norm_g[l], mem_k_norm_g[l], w_out[l],
                         ffn2_norm_g[l], ffn2_w_in[l], ffn2_w_out[l], rel_bias)
    return x
```

```python
import functools
import math

import jax
import jax.numpy as jnp
from jax import lax
from jax.experimental import pallas as pl
from jax.experimental.pallas import tpu as pltpu

N_HEADS = 16
HEAD_DIM = 64
N_KV_GROUPS = 4
HEADS_PER_GROUP = N_HEADS // N_KV_GROUPS
GROUP_WIDTH = HEADS_PER_GROUP * HEAD_DIM
CMP_BLOCK = 32
CMP_STRIDE = 16
SEL_BLOCK = 64
N_SELECT = 16
WINDOW = 512
FORCE = 1e4
CONV_WIDTH = 3
MEM_HEADS = 4
REL_BUCKETS = 32
REL_MAX_DIST = 128
EPS = 1e-6
NEG = -1e30

T = 256
BLK_PER_T = T // SEL_BLOCK
CMP_PER_T = T // CMP_STRIDE
CMP_NEAR = 2 * CMP_PER_T
FFN_TM = 512
ROW_TM = 256
MEM_TM = 512
MERGE_TM = 512
HALO = 8
VMEM_LIMIT = 52 * 1024 * 1024

F32 = jnp.float32
BF16 = jnp.bfloat16
HI = lax.Precision.HIGHEST


def _dot(a, b):
    return jnp.dot(a, b, preferred_element_type=F32)


def _dot_nt(a, b):
    return lax.dot_general(a, b, (((1,), (1,)), ((), ())), preferred_element_type=F32)


def _rms_rows(xf, g):
    return xf * lax.rsqrt(jnp.mean(xf * xf, axis=-1, keepdims=True) + EPS) * g


def _sigmoid(x):
    return 1.0 / (1.0 + jnp.exp(-x))


def _resident(shape):
    zeros = (0,) * len(shape)
    return pl.BlockSpec(shape, lambda *_: zeros, pipeline_mode=pl.Buffered(1))


def _params(n_axes):
    return pltpu.CompilerParams(dimension_semantics=("arbitrary",) * n_axes,
                                vmem_limit_bytes=VMEM_LIMIT)


def _ffn_kernel(x_ref, g_ref, wa_ref, wb_ref, wo_ref, o_ref, *, ff_chunk):
    x = x_ref[...]
    h = _rms_rows(x, g_ref[...]).astype(BF16)
    d_ff = wa_ref.shape[1]
    acc = jnp.zeros(x.shape, F32)
    for lo in range(0, d_ff, ff_chunk):
        a = _dot(h, wa_ref[:, lo:lo + ff_chunk])
        b = _dot(h, wb_ref[:, lo:lo + ff_chunk])
        z = (a * _sigmoid(a) * b).astype(BF16)
        acc = acc + _dot(z, wo_ref[lo:lo + ff_chunk, :])
    o_ref[...] = x + 0.5 * acc


def _ffn(x2d, g, w_in, w_out):
    n, d = x2d.shape
    d_ff = w_out.shape[0]
    wa = w_in[:, :d_ff].astype(BF16)
    wb = w_in[:, d_ff:].astype(BF16)
    wo = w_out.astype(BF16)
    ff_chunk = d_ff // 2 if (d_ff // 2) % 128 == 0 else d_ff
    return pl.pallas_call(
        functools.partial(_ffn_kernel, ff_chunk=ff_chunk),
        grid=(n // FFN_TM,),
        in_specs=[pl.BlockSpec((FFN_TM, d), lambda i: (i, 0)),
                  _resident((1, d)), _resident((d, d_ff)), _resident((d, d_ff)), _resident((d_ff, d))],
        out_specs=pl.BlockSpec((FFN_TM, d), lambda i: (i, 0)),
        out_shape=jax.ShapeDtypeStruct((n, d), F32),
        compiler_params=_params(1),
        name="ffn",
    )(x2d, g.reshape(1, d), wa, wb, wo)


def _proj_rows_kernel(x_ref, g_ref, w_ref, bd_ref, kg_ref,
                      ks_o, kw_o, kc_o, vc_o, conv_o, qm_o, mg_o, *, widths):
    h = _rms_rows(x_ref[0], g_ref[...]).astype(BF16)

    def knorm(k):
        ms = jnp.dot(k * k, bd_ref[...], precision=HI, preferred_element_type=F32)
        return (k * lax.rsqrt(ms + EPS) * kg_ref[...]).astype(BF16)

    lo = 0
    outs = (ks_o, kw_o, kc_o, vc_o, conv_o, qm_o, mg_o)
    for idx, (o_ref, wd) in enumerate(zip(outs, widths)):
        y = _dot(h, w_ref[:, lo:lo + wd])
        o_ref[0] = knorm(y) if idx < 2 else y
        lo += wd


def _proj_t_kernel(x_ref, g_ref, wt_ref, qg_ref, q_o, vs_o, vw_o, gt_o, *, d_q, d_kv):
    h = _rms_rows(x_ref[0], g_ref[...]).astype(BF16)
    qg = qg_ref[...] * (HEAD_DIM ** -0.5)
    for hd in range(d_q // HEAD_DIM):
        q = _dot_nt(wt_ref[hd * HEAD_DIM:(hd + 1) * HEAD_DIM, :], h)
        qn = q * lax.rsqrt(jnp.mean(q * q, axis=0, keepdims=True) + EPS) * qg
        q_o[0, 0, hd * HEAD_DIM:(hd + 1) * HEAD_DIM, :] = qn.astype(BF16)
    lo = d_q
    vs_o[0, 0] = _dot_nt(wt_ref[lo:lo + d_kv, :], h).astype(BF16)
    lo += d_kv
    vw_o[0, 0] = _dot_nt(wt_ref[lo:lo + d_kv, :], h).astype(BF16)
    lo += d_kv
    gt_o[0, 0] = _sigmoid(_dot_nt(wt_ref[lo:, :], h))


def _compress_kernel(c_ref, pe_ref, w1_ref, w2_ref, kg_ref, o_ref, *, is_key):
    c = c_ref[0, 0]
    n_chunks, half = c.shape
    a = _dot((c + pe_ref[:, :half]).astype(BF16), w1_ref[:half, :])
    b = _dot((c + pe_ref[:, half:]).astype(BF16), w1_ref[half:, :])
    hid = a + pltpu.roll(b, n_chunks - 1, 0)
    hid = (hid * _sigmoid(hid)).astype(BF16)
    if is_key:
        y = _dot(hid, w2_ref[...])
        y = _rms_rows(y, kg_ref[...])
        row = lax.broadcasted_iota(jnp.int32, y.shape, 0)
        o_ref[0, 0] = jnp.where(row < n_chunks - 1, y, 0.0).astype(BF16)
    else:
        y = _dot_nt(w2_ref[...], hid)
        col = lax.broadcasted_iota(jnp.int32, y.shape, 1)
        o_ref[0, 0] = jnp.where(col < n_chunks - 1, y, 0.0).astype(BF16)


def _compress(c4, pe, w1, w2, k_gain, is_key):
    b, g, n_chunks, width = c4.shape
    hidden = w1.shape[1]
    w2_arg = w2.astype(BF16) if is_key else w2.T.astype(BF16)
    out_block = (1, 1, n_chunks, HEAD_DIM) if is_key else (1, 1, HEAD_DIM, n_chunks)
    return pl.pallas_call(
        functools.partial(_compress_kernel, is_key=is_key),
        grid=(b, g),
        in_specs=[pl.BlockSpec((1, 1, n_chunks, width), lambda i, j: (i, j, 0, 0)),
                  _resident((1, 2 * width)), _resident((2 * width, hidden)),
                  _resident(w2_arg.shape), _resident((1, HEAD_DIM))],
        out_specs=pl.BlockSpec(out_block, lambda i, j: (i, j, 0, 0)),
        out_shape=jax.ShapeDtypeStruct((b, g) + out_block[2:], BF16),
        compiler_params=_params(2),
        name="compress_k" if is_key else "compress_v",
    )(c4, pe.reshape(1, 2 * width), w1.astype(BF16), w2_arg, k_gain.reshape(1, HEAD_DIM))


def _bias_kernel(rb_ref, bkt_near_ref, bkt_cmp_ref, near_o, cmp_o):
    h = pl.program_id(0)
    far = rb_ref[REL_BUCKETS - 1, h]

    def lookup(bkt):
        out = jnp.zeros(bkt.shape, F32)
        for k in range(REL_BUCKETS - 1):
            out = jnp.where(bkt == k, rb_ref[k, h] - far, out)
        return out

    key = lax.broadcasted_iota(jnp.int32, (T, T), 0)
    qry = lax.broadcasted_iota(jnp.int32, (T, T), 1)
    near_o[0, 0] = jnp.where(qry >= key, lookup(bkt_near_ref[0]), NEG)
    near_o[0, 1] = lookup(bkt_near_ref[1])
    cmp_o[0] = lookup(bkt_cmp_ref[...])


def _rel_bucket(dist):
    n = jnp.maximum(dist, 0)
    max_exact = REL_BUCKETS // 2
    nf = jnp.maximum(n, 1).astype(F32)
    large = max_exact + (jnp.log(nf / max_exact) / math.log(REL_MAX_DIST / max_exact)
                         * (REL_BUCKETS - max_exact)).astype(jnp.int32)
    large = jnp.minimum(large, REL_BUCKETS - 1)
    return jnp.where(n < max_exact, n, large)


def _bias_tiles(rel_bias):
    key = jnp.arange(T)[:, None]
    qry = jnp.arange(T)[None, :]
    bkt_near = jnp.stack([_rel_bucket(qry - key), _rel_bucket(qry - key + T)]).astype(jnp.int32)
    j = jnp.arange(CMP_NEAR)[:, None]
    bkt_cmp = _rel_bucket(qry - CMP_STRIDE * (j - CMP_PER_T) - (CMP_BLOCK - 1)).astype(jnp.int32)
    return pl.pallas_call(
        _bias_kernel,
        grid=(N_HEADS,),
        in_specs=[pl.BlockSpec(memory_space=pltpu.SMEM),
                  pl.BlockSpec((2, T, T), lambda h: (0, 0, 0)),
                  pl.BlockSpec((CMP_NEAR, T), lambda h: (0, 0))],
        out_specs=[pl.BlockSpec((1, 2, T, T), lambda h: (h, 0, 0, 0)),
                   pl.BlockSpec((1, CMP_NEAR, T), lambda h: (h, 0, 0))],
        out_shape=[jax.ShapeDtypeStruct((N_HEADS, 2, T, T), F32),
                   jax.ShapeDtypeStruct((N_HEADS, CMP_NEAR, T), F32)],
        compiler_params=_params(1),
        name="bias_tiles",
    )(rel_bias, bkt_near, bkt_cmp)


def _nsa_kernel(q_ref, ks_ref, vs_ref, kw_ref, vw_ref, kc_ref, vc_ref, smap_ref, gt_ref, nb_ref, cb_ref,
                o_ref,
                qpad_ref, sc_ref, selneg_ref, oc_ref,
                ms_ref, ls_ref, accs_ref, mw_ref, lw_ref, accw_ref, *, n_blk, n_cmp):
    g = pl.program_id(1)
    qi = pl.program_id(2)
    t = qi * T + lax.broadcasted_iota(jnp.int32, (1, T), 1)
    R = HEADS_PER_GROUP

    def q_head(r):
        return q_ref[0, 0, r * HEAD_DIM:(r + 1) * HEAD_DIM, :]

    rowgrp = lax.shift_right_logical(lax.broadcasted_iota(jnp.int32, (GROUP_WIDTH, T), 0),
                                     int(math.log2(HEAD_DIM)))
    for r in range(R):
        q4 = jnp.concatenate([q_head(r).astype(F32)] * N_KV_GROUPS, axis=0)
        qpad_ref[r] = jnp.where(rowgrp == g, q4, 0.0).astype(BF16)

    c_idx = lax.broadcasted_iota(jnp.int32, (n_cmp, T), 0)
    valid_c = (c_idx * CMP_STRIDE + (CMP_BLOCK - 1) <= t) & (c_idx < n_cmp - 1)
    near0 = pl.multiple_of(qi * CMP_PER_T, CMP_PER_T)
    psum = jnp.zeros((n_cmp, T), F32)
    for r in range(R):
        sc_ref[r, 0:CMP_PER_T, :] = jnp.zeros((CMP_PER_T, T), F32)
        sc_ref[r, CMP_PER_T:, :] = _dot(kc_ref[0, 0], q_head(r))
        sc_ref[r, pl.ds(near0, CMP_NEAR), :] = sc_ref[r, pl.ds(near0, CMP_NEAR), :] + cb_ref[r]
        s = jnp.where(valid_c, sc_ref[r, CMP_PER_T:, :], NEG)
        m = jnp.max(s, axis=0, keepdims=True)
        p = jnp.where(valid_c, jnp.exp(s - m), 0.0)
        l = jnp.sum(p, axis=0, keepdims=True)
        p = p * jnp.where(l > 0.0, 1.0 / jnp.where(l > 0.0, l, 1.0), 0.0)
        psum = psum + p
        oc_ref[r] = _dot(vc_ref[0, 0], p.astype(BF16))

    imp = jnp.dot(smap_ref[...], psum, precision=HI, preferred_element_type=F32)
    blk = lax.broadcasted_iota(jnp.int32, (n_blk, T), 0)
    cur = lax.shift_right_logical(t, int(math.log2(SEL_BLOCK)))
    forced = (blk == 0) | (blk == cur) | (blk == cur - 1)
    score = jnp.where(blk <= cur, imp + jnp.where(forced, FORCE, 0.0), -FORCE)
    selneg = jnp.full((n_blk, T), NEG, F32)
    for _ in range(min(N_SELECT, n_blk)):
        best = jnp.max(score, axis=0, keepdims=True)
        first = jnp.min(jnp.where(score == best, blk, n_blk), axis=0, keepdims=True)
        pick = blk == first
        selneg = jnp.where(pick, 0.0, selneg)
        score = jnp.where(pick, -jnp.inf, score)
    selneg_ref[...] = selneg

    for m_ref, l_ref, acc_ref in ((ms_ref, ls_ref, accs_ref), (mw_ref, lw_ref, accw_ref)):
        m_ref[...] = jnp.full(m_ref.shape, NEG, F32)
        l_ref[...] = jnp.zeros(l_ref.shape, F32)
        acc_ref[...] = jnp.zeros(acc_ref.shape, F32)

    def attend(r, k_tile, vt_tile, add, m_ref, l_ref, acc_ref):
        s = _dot(k_tile, qpad_ref[r])
        if add is not None:
            s = s + add
        m_old = m_ref[r]
        m_new = jnp.maximum(m_old, jnp.max(s, axis=0, keepdims=True))
        alpha = jnp.exp(m_old - m_new)
        p = jnp.exp(s - m_new)
        l_ref[r] = alpha * l_ref[r] + jnp.sum(p, axis=0, keepdims=True)
        acc_ref[r] = alpha * acc_ref[r] + _dot(vt_tile, p.astype(BF16))
        m_ref[r] = m_new

    def sel_add(kj):
        rows = [jnp.broadcast_to(selneg_ref[pl.ds(kj * BLK_PER_T + j, 1), :], (SEL_BLOCK, T))
                for j in range(BLK_PER_T)]
        return jnp.concatenate(rows, axis=0)

    sa = sel_add(qi)
    for r in range(R):
        attend(r, ks_ref[0, qi], vs_ref[0, qi, 0], nb_ref[r, 0] + sa, ms_ref, ls_ref, accs_ref)
        attend(r, kw_ref[0, qi], vw_ref[0, qi, 0], nb_ref[r, 0], mw_ref, lw_ref, accw_ref)

    @pl.when(qi >= 1)
    def _():
        sa1 = sel_add(qi - 1)
        for r in range(R):
            attend(r, ks_ref[0, qi - 1], vs_ref[0, qi - 1, 0], nb_ref[r, 1] + sa1, ms_ref, ls_ref, accs_ref)
            attend(r, kw_ref[0, qi - 1], vw_ref[0, qi - 1, 0], nb_ref[r, 1], mw_ref, lw_ref, accw_ref)

    @pl.when(qi >= 2)
    def _():
        key = lax.broadcasted_iota(jnp.int32, (T, T), 0)
        qry = lax.broadcasted_iota(jnp.int32, (T, T), 1)
        tail = jnp.where(key > qry, 0.0, NEG)
        for r in range(R):
            attend(r, kw_ref[0, qi - 2], vw_ref[0, qi - 2, 0], tail, mw_ref, lw_ref, accw_ref)

    def far_tile(kj, carry):
        sa_far = sel_add(kj)
        for r in range(R):
            attend(r, ks_ref[0, kj], vs_ref[0, kj, 0], sa_far, ms_ref, ls_ref, accs_ref)
        return carry

    lax.fori_loop(0, qi - 1, far_tile, 0)

    outs = []
    for r in range(R):
        head = g * R + r
        g_c = gt_ref[0, 0, pl.ds(head, 1), :]
        g_s = gt_ref[0, 0, pl.ds(N_HEADS + head, 1), :]
        g_w = gt_ref[0, 0, pl.ds(2 * N_HEADS + head, 1), :]
        outs.append(g_c * oc_ref[r] + g_s * (accs_ref[r] / ls_ref[r]) + g_w * (accw_ref[r] / lw_ref[r]))
    o_ref[0] = jnp.concatenate(outs, axis=0).T


def _nsa(q_t, ks, vs_t, kw, vw_t, kc, vc_t, smap_t, gates_t, near_bias, cmp_bias, seq):
    b, nq = q_t.shape[0], q_t.shape[1]
    n_blk = seq // SEL_BLOCK
    n_cmp = seq // CMP_STRIDE
    R = HEADS_PER_GROUP
    kv_spec = pl.BlockSpec((1, nq, T, GROUP_WIDTH), lambda i, j, k: (i, 0, 0, 0))
    vt_spec = pl.BlockSpec((1, nq, 1, HEAD_DIM, T), lambda i, j, k: (i, 0, j, 0, 0))
    stat = pltpu.VMEM((R, 1, T), F32)
    acc = pltpu.VMEM((R, HEAD_DIM, T), F32)
    return pl.pallas_call(
        functools.partial(_nsa_kernel, n_blk=n_blk, n_cmp=n_cmp),
        grid=(b, N_KV_GROUPS, nq),
        in_specs=[pl.BlockSpec((1, 1, GROUP_WIDTH, T), lambda i, j, k: (i, k, j, 0)),
                  kv_spec, vt_spec, kv_spec, vt_spec,
                  pl.BlockSpec((1, 1, n_cmp, HEAD_DIM), lambda i, j, k: (i, j, 0, 0)),
                  pl.BlockSpec((1, 1, HEAD_DIM, n_cmp), lambda i, j, k: (i, j, 0, 0)),
                  pl.BlockSpec((n_blk, n_cmp), lambda i, j, k: (0, 0)),
                  pl.BlockSpec((1, 1, gates_t.shape[2], T), lambda i, j, k: (i, k, 0, 0)),
                  pl.BlockSpec((R, 2, T, T), lambda i, j, k: (j, 0, 0, 0)),
                  pl.BlockSpec((R, CMP_NEAR, T), lambda i, j, k: (j, 0, 0))],
        out_specs=pl.BlockSpec((1, T, GROUP_WIDTH), lambda i, j, k: (i, k, j)),
        out_shape=jax.ShapeDtypeStruct((b, seq, N_HEADS * HEAD_DIM), F32),
        scratch_shapes=[pltpu.VMEM((R, GROUP_WIDTH, T), BF16),
                        pltpu.VMEM((R, n_cmp + CMP_PER_T, T), F32),
                        pltpu.VMEM((n_blk, T), F32),
                        acc, stat, stat, acc, stat, stat, acc],
        compiler_params=_params(3),
        name="nsa",
    )(q_t, ks, vs_t, kw, vw_t, kc, vc_t, smap_t, gates_t, near_bias, cmp_bias)


def _mem_kv_kernel(mem_ref, g_ref, w_ref, kg_ref, k_o, v_o):
    h = _rms_rows(mem_ref[0], g_ref[...]).astype(BF16)
    width = k_o.shape[2]
    hd = width // MEM_HEADS
    k = _dot(h, w_ref[:, :width])
    for i in range(MEM_HEADS):
        k_o[0, :, i * hd:(i + 1) * hd] = _rms_rows(k[:, i * hd:(i + 1) * hd], kg_ref[...]).astype(BF16)
    v_o[0] = _dot(h, w_ref[:, width:]).astype(BF16)


def _mem_attn_kernel(q_ref, k_ref, v_ref, qg_ref, o_ref):
    width = q_ref.shape[2]
    hd = width // MEM_HEADS
    for i in range(MEM_HEADS):
        sl = slice(i * hd, (i + 1) * hd)
        q = (_rms_rows(q_ref[0, :, sl], qg_ref[...]) * (hd ** -0.5)).astype(BF16)
        s = _dot_nt(q, k_ref[0, :, sl])
        p = jnp.exp(s - jnp.max(s, axis=-1, keepdims=True))
        p = p / jnp.sum(p, axis=-1, keepdims=True)
        o_ref[0, :, sl] = _dot(p.astype(BF16), v_ref[0, :, sl])


def _mem_attention(q_mem, mem, mem_norm_g, w_mem_kv, q_g, k_g):
    b, s, width = q_mem.shape
    n_mem, d = mem.shape[1], mem.shape[2]
    hd = width // MEM_HEADS
    kv_shape = jax.ShapeDtypeStruct((b, n_mem, width), BF16)
    kv_block = pl.BlockSpec((1, n_mem, width), lambda i: (i, 0, 0))
    km, vm = pl.pallas_call(
        _mem_kv_kernel,
        grid=(b,),
        in_specs=[pl.BlockSpec((1, n_mem, d), lambda i: (i, 0, 0)),
                  _resident((1, d)), _resident((d, 2 * width)), _resident((1, hd))],
        out_specs=[kv_block, kv_block],
        out_shape=[kv_shape, kv_shape],
        compiler_params=_params(1),
        name="mem_kv",
    )(mem, mem_norm_g.reshape(1, d), w_mem_kv.astype(BF16), k_g.reshape(1, hd))
    kv_block2 = pl.BlockSpec((1, n_mem, width), lambda i, j: (i, 0, 0))
    return pl.pallas_call(
        _mem_attn_kernel,
        grid=(b, s // MEM_TM),
        in_specs=[pl.BlockSpec((1, MEM_TM, width), lambda i, j: (i, j, 0)),
                  kv_block2, kv_block2, _resident((1, hd))],
        out_specs=pl.BlockSpec((1, MEM_TM, width), lambda i, j: (i, j, 0)),
        out_shape=jax.ShapeDtypeStruct((b, s, width), F32),
        compiler_params=_params(2),
        name="mem_attn",
    )(q_mem, km, vm, q_g.reshape(1, hd))


def _merge_kernel(x_ref, nsa_ref, mem_ref, cb_ref, cc_ref, cx_ref, hc_ref, hx_ref,
                  g1_ref, g2_ref, g3_ref, cw_ref, bias_ref, wo_ref, o_ref):
    j = pl.program_id(1)
    u = cc_ref[0] * cx_ref[0]
    halo = jnp.where(j > 0, hc_ref[0] * hx_ref[0], 0.0)
    prev1 = halo[HALO - 1:HALO, :]
    prev2 = halo[HALO - 2:HALO - 1, :]
    row = lax.broadcasted_iota(jnp.int32, u.shape, 0)
    u1 = jnp.where(row == 0, prev1, pltpu.roll(u, 1, 0))
    u2 = jnp.where(row == 0, prev2, jnp.where(row == 1, prev1, pltpu.roll(u, 2, 0)))
    y = cw_ref[0:1, :] * u2 + cw_ref[1:2, :] * u1 + cw_ref[2:3, :] * u
    o_conv = cb_ref[0] * (y + bias_ref[...])
    merged = (_sigmoid(g1_ref[0]) * nsa_ref[0] + _sigmoid(g2_ref[0]) * o_conv
              + _sigmoid(g3_ref[0]) * mem_ref[0])
    o_ref[0] = x_ref[0] + _dot(merged.astype(BF16), wo_ref[...])


def _merge(x, o_nsa, o_mem, conv_in, merge_g, conv_w, conv_b, w_out):
    b, s, d = x.shape
    tm = MERGE_TM

    def col(c):
        return pl.BlockSpec((1, tm, d), lambda i, j: (i, j, c))

    def halo(c):
        return pl.BlockSpec((1, HALO, d), lambda i, j: (i, jnp.maximum(j * (tm // HALO) - 1, 0), c))

    return pl.pallas_call(
        _merge_kernel,
        grid=(b, s // tm),
        in_specs=[col(0), col(0), col(0),
                  col(0), col(1), col(2), halo(1), halo(2),
                  col(0), col(1), col(2),
                  _resident((CONV_WIDTH, d)), _resident((1, d)), _resident((d, d))],
        out_specs=col(0),
        out_shape=jax.ShapeDtypeStruct((b, s, d), F32),
        compiler_params=_params(2),
        name="merge",
    )(x, o_nsa, o_mem, conv_in, conv_in, conv_in, conv_in, conv_in,
      merge_g, merge_g, merge_g, conv_w, conv_b.reshape(1, d), w_out.astype(BF16))


def _layer(x, mem, ffn1_norm_g, ffn1_w_in, ffn1_w_out, mix_norm_g, w_in, q_norm_g, k_norm_g,
           cmp_pe_k, cmp_w1_k, cmp_w2_k, cmp_pe_v, cmp_w1_v, cmp_w2_v, conv_w, conv_b,
           mem_norm_g, w_mem_kv, mem_q_norm_g, mem_k_norm_g, w_out,
           ffn2_norm_g, ffn2_w_in, ffn2_w_out, near_bias, cmp_bias):
    b, s, d = x.shape
    assert s % T == 0 and s % MERGE_TM == 0 and (b * s) % FFN_TM == 0
    assert WINDOW == 2 * T and REL_MAX_DIST <= T // 2
    nq = s // T
    d_q = N_HEADS * HEAD_DIM
    d_kv = N_KV_GROUPS * HEAD_DIM
    d_conv = conv_w.shape[1]
    d_mem = w_mem_kv.shape[1] // 2

    x = _ffn(x.reshape(b * s, d), ffn1_norm_g, ffn1_w_in, ffn1_w_out).reshape(b, s, d)

    o = 0
    w_q = w_in[:, o:o + d_q]; o += d_q
    w_kc, w_vc, w_ks, w_vs, w_kw, w_vw = [w_in[:, o + i * d_kv:o + (i + 1) * d_kv] for i in range(6)]
    o += 6 * d_kv
    w_g = w_in[:, o:o + 3 * N_HEADS]; o += 3 * N_HEADS
    w_conv = w_in[:, o:o + 3 * d_conv]; o += 3 * d_conv
    w_qm = w_in[:, o:o + d_mem]; o += d_mem
    w_mg = w_in[:, o:]

    w_rows = jnp.concatenate([w_ks, w_kw, w_kc, w_vc, w_conv, w_qm, w_mg], axis=1).astype(BF16)
    widths = (d_kv, d_kv, d_kv, d_kv, 3 * d_conv, d_mem, w_mg.shape[1])
    group_of = jnp.arange(d_kv) // HEAD_DIM
    block_diag = (group_of[:, None] == group_of[None, :]).astype(F32) / HEAD_DIM
    k_gain_row = jnp.tile(k_norm_g, N_KV_GROUPS).reshape(1, d_kv)

    def rows_out(wd, dt):
        return (pl.BlockSpec((1, ROW_TM, wd), lambda i, j: (i, j, 0)), jax.ShapeDtypeStruct((b, s, wd), dt))

    specs = [rows_out(wd, BF16 if i < 2 else F32) for i, wd in enumerate(widths)]
    ks, kw, kc, vc, conv_in, q_mem, merge_g = pl.pallas_call(
        functools.partial(_proj_rows_kernel, widths=widths),
        grid=(b, s // ROW_TM),
        in_specs=[pl.BlockSpec((1, ROW_TM, d), lambda i, j: (i, j, 0)),
                  _resident((1, d)), _resident(w_rows.shape), _resident((d_kv, d_kv)), _resident((1, d_kv))],
        out_specs=[sp[0] for sp in specs],
        out_shape=[sp[1] for sp in specs],
        compiler_params=_params(2),
        name="proj_rows",
    )(x, mix_norm_g.reshape(1, d), w_rows, block_diag, k_gain_row)

    w_g_t = w_g.reshape(d, N_HEADS, 3).transpose(2, 1, 0).reshape(3 * N_HEADS, d)
    n_gate_rows = 128
    w_g_t = jnp.pad(w_g_t, ((0, n_gate_rows - 3 * N_HEADS), (0, 0)))
    w_t = jnp.concatenate([w_q.T, w_vs.T, w_vw.T, w_g_t], axis=0).astype(BF16)

    def t_out(rows, dt):
        return (pl.BlockSpec((1, 1, rows, T), lambda i, j: (i, j, 0, 0)),
                jax.ShapeDtypeStruct((b, nq, rows, T), dt))

    t_specs = [t_out(d_q, BF16), t_out(d_kv, BF16), t_out(d_kv, BF16), t_out(n_gate_rows, F32)]
    q_t, vs_t, vw_t, gates_t = pl.pallas_call(
        functools.partial(_proj_t_kernel, d_q=d_q, d_kv=d_kv),
        grid=(b, nq),
        in_specs=[pl.BlockSpec((1, T, d), lambda i, j: (i, j, 0)),
                  _resident((1, d)), _resident(w_t.shape), _resident((HEAD_DIM, 1))],
        out_specs=[sp[0] for sp in t_specs],
        out_shape=[sp[1] for sp in t_specs],
        compiler_params=_params(2),
        name="proj_t",
    )(x, mix_norm_g.reshape(1, d), w_t, q_norm_g.reshape(HEAD_DIM, 1))

    n_chunks = s // CMP_STRIDE

    def chunked(a):
        return a.reshape(b, n_chunks, CMP_STRIDE, N_KV_GROUPS, HEAD_DIM).transpose(0, 3, 1, 2, 4) \
                .reshape(b, N_KV_GROUPS, n_chunks, CMP_STRIDE * HEAD_DIM)

    k_cmp = _compress(chunked(kc), cmp_pe_k, cmp_w1_k, cmp_w2_k, k_norm_g, True)
    v_cmp_t = _compress(chunked(vc), cmp_pe_v, cmp_w1_v, cmp_w2_v, k_norm_g, False)

    cs = jnp.arange(n_chunks)[None, :] * CMP_STRIDE
    ss = jnp.arange(s // SEL_BLOCK)[:, None] * SEL_BLOCK
    ov = jnp.maximum(jnp.minimum(cs + CMP_BLOCK, ss + SEL_BLOCK) - jnp.maximum(cs, ss), 0)
    smap_t = ov.astype(F32) / CMP_BLOCK

    o_nsa = _nsa(q_t, ks.reshape(b, nq, T, d_kv), vs_t.reshape(b, nq, N_KV_GROUPS, HEAD_DIM, T),
                 kw.reshape(b, nq, T, d_kv), vw_t.reshape(b, nq, N_KV_GROUPS, HEAD_DIM, T),
                 k_cmp, v_cmp_t, smap_t, gates_t, near_bias, cmp_bias, s)

    o_mem = _mem_attention(q_mem, mem, mem_norm_g, w_mem_kv, mem_q_norm_g, mem_k_norm_g)
    x = _merge(x, o_nsa, o_mem, conv_in, merge_g, conv_w, conv_b, w_out)
    x = _ffn(x.reshape(b * s, d), ffn2_norm_g, ffn2_w_in, ffn2_w_out).reshape(b, s, d)
    return x


def kernel(x, mem, ffn1_norm_g, ffn1_w_in, ffn1_w_out, mix_norm_g, w_in, q_norm_g, k_norm_g, cmp_pe_k, cmp_w1_k, cmp_w2_k, cmp_pe_v, cmp_w1_v, cmp_w2_v, conv_w, conv_b, mem_norm_g, w_mem_kv, mem_q_norm_g, mem_k_norm_g, w_out, ffn2_norm_g, ffn2_w_in, ffn2_w_out, rel_bias):
    near_bias, cmp_bias = _bias_tiles(rel_bias)
    for l in range(ffn1_norm_g.shape[0]):
        x = _layer(x, mem, ffn1_norm_g[l], ffn1_w_in[l], ffn1_w_out[l], mix_norm_g[l], w_in[l],
                   q_norm_g[l], k_norm_g[l], cmp_pe_k[l], cmp_w1_k[l], cmp_w2_k[l],
                   cmp_pe_v[l], cmp_w1_v[l], cmp_w2_v[l], conv_w[l], conv_b[l],
                   mem_norm_g[l], w_mem_kv[l], mem_q_norm_g[l], mem_k_norm_g[l], w_out[l],
                   ffn2_norm_g[l], ffn2_w_in[l], ffn2_w_out[l], near_bias, cmp_bias)
    return x
```

```python
import functools
import math

import jax
import jax.numpy as jnp
from jax import lax
from jax.experimental import pallas as pl
from jax.experimental.pallas import tpu as pltpu

N_HEADS = 16
HEAD_DIM = 64
N_KV_GROUPS = 4
HEADS_PER_GROUP = N_HEADS // N_KV_GROUPS
GROUP_WIDTH = HEADS_PER_GROUP * HEAD_DIM
CMP_BLOCK = 32
CMP_STRIDE = 16
SEL_BLOCK = 64
N_SELECT = 16
WINDOW = 512
FORCE = 1e4
CONV_WIDTH = 3
MEM_HEADS = 4
REL_BUCKETS = 32
REL_MAX_DIST = 128
EPS = 1e-6
NEG = -1e30

T = 256
BLK_PER_T = T // SEL_BLOCK
CMP_PER_T = T // CMP_STRIDE
CMP_NEAR = 2 * CMP_PER_T
FFN_TM = 512
ROW_TM = 256
MEM_TM = 512
MERGE_TM = 512
HALO = 8
VMEM_LIMIT = 52 * 1024 * 1024

F32 = jnp.float32
BF16 = jnp.bfloat16
HI = lax.Precision.HIGHEST


def _dot(a, b):
    return jnp.dot(a, b, preferred_element_type=F32)


def _dot_nt(a, b):
    return lax.dot_general(a, b, (((1,), (1,)), ((), ())), preferred_element_type=F32)


def _rms_rows(xf, g):
    return xf * lax.rsqrt(jnp.mean(xf * xf, axis=-1, keepdims=True) + EPS) * g


def _sigmoid(x):
    return 1.0 / (1.0 + jnp.exp(-x))


def _resident(shape):
    zeros = (0,) * len(shape)
    return pl.BlockSpec(shape, lambda *_: zeros, pipeline_mode=pl.Buffered(1))


def _params(n_axes):
    return pltpu.CompilerParams(dimension_semantics=("arbitrary",) * n_axes,
                                vmem_limit_bytes=VMEM_LIMIT)


def _ffn_kernel(x_ref, g_ref, wa_ref, wb_ref, wo_ref, o_ref, *, ff_chunk):
    x = x_ref[...]
    h = _rms_rows(x, g_ref[...]).astype(BF16)
    d_ff = wa_ref.shape[1]
    acc = jnp.zeros(x.shape, F32)
    for lo in range(0, d_ff, ff_chunk):
        a = _dot(h, wa_ref[:, lo:lo + ff_chunk])
        b = _dot(h, wb_ref[:, lo:lo + ff_chunk])
        z = (a * _sigmoid(a) * b).astype(BF16)
        acc = acc + _dot(z, wo_ref[lo:lo + ff_chunk, :])
    o_ref[...] = x + 0.5 * acc


def _ffn(x2d, g, w_in, w_out):
    n, d = x2d.shape
    d_ff = w_out.shape[0]
    wa = w_in[:, :d_ff].astype(BF16)
    wb = w_in[:, d_ff:].astype(BF16)
    wo = w_out.astype(BF16)
    ff_chunk = d_ff // 2 if (d_ff // 2) % 128 == 0 else d_ff
    return pl.pallas_call(
        functools.partial(_ffn_kernel, ff_chunk=ff_chunk),
        grid=(n // FFN_TM,),
        in_specs=[pl.BlockSpec((FFN_TM, d), lambda i: (i, 0)),
                  _resident((1, d)), _resident((d, d_ff)), _resident((d, d_ff)), _resident((d_ff, d))],
        out_specs=pl.BlockSpec((FFN_TM, d), lambda i: (i, 0)),
        out_shape=jax.ShapeDtypeStruct((n, d), F32),
        compiler_params=_params(1),
        name="ffn",
    )(x2d, g.reshape(1, d), wa, wb, wo)


def _proj_rows_kernel(x_ref, g_ref, w_ref, bd_ref, kg_ref,
                      ks_o, kw_o, kc_o, vc_o, conv_o, qm_o, mg_o, *, widths):
    h = _rms_rows(x_ref[0], g_ref[...]).astype(BF16)

    def knorm(k):
        ms = jnp.dot(k * k, bd_ref[...], precision=HI, preferred_element_type=F32)
        return (k * lax.rsqrt(ms + EPS) * kg_ref[...]).astype(BF16)

    lo = 0
    outs = (ks_o, kw_o, kc_o, vc_o, conv_o, qm_o, mg_o)
    for idx, (o_ref, wd) in enumerate(zip(outs, widths)):
        y = _dot(h, w_ref[:, lo:lo + wd])
        o_ref[0] = knorm(y) if idx < 2 else y
        lo += wd


def _proj_t_kernel(x_ref, g_ref, wt_ref, qg_ref, q_o, vs_o, vw_o, gt_o, *, d_q, d_kv):
    h = _rms_rows(x_ref[0], g_ref[...]).astype(BF16)
    qg = qg_ref[...] * (HEAD_DIM ** -0.5)
    for hd in range(d_q // HEAD_DIM):
        q = _dot_nt(wt_ref[hd * HEAD_DIM:(hd + 1) * HEAD_DIM, :], h)
        qn = q * lax.rsqrt(jnp.mean(q * q, axis=0, keepdims=True) + EPS) * qg
        q_o[0, 0, hd * HEAD_DIM:(hd + 1) * HEAD_DIM, :] = qn.astype(BF16)
    lo = d_q
    vs_o[0, 0] = _dot_nt(wt_ref[lo:lo + d_kv, :], h).astype(BF16)
    lo += d_kv
    vw_o[0, 0] = _dot_nt(wt_ref[lo:lo + d_kv, :], h).astype(BF16)
    lo += d_kv
    gt_o[0, 0] = _sigmoid(_dot_nt(wt_ref[lo:, :], h))


def _compress_kernel(c_ref, pe_ref, w1_ref, w2_ref, kg_ref, o_ref, *, is_key):
    c = c_ref[0, 0]
    n_chunks, half = c.shape
    a = _dot((c + pe_ref[:, :half]).astype(BF16), w1_ref[:half, :])
    b = _dot((c + pe_ref[:, half:]).astype(BF16), w1_ref[half:, :])
    hid = a + pltpu.roll(b, n_chunks - 1, 0)
    hid = (hid * _sigmoid(hid)).astype(BF16)
    if is_key:
        y = _dot(hid, w2_ref[...])
        y = _rms_rows(y, kg_ref[...])
        row = lax.broadcasted_iota(jnp.int32, y.shape, 0)
        o_ref[0, 0] = jnp.where(row < n_chunks - 1, y, 0.0).astype(BF16)
    else:
        y = _dot_nt(w2_ref[...], hid)
        col = lax.broadcasted_iota(jnp.int32, y.shape, 1)
        o_ref[0, 0] = jnp.where(col < n_chunks - 1, y, 0.0).astype(BF16)


def _compress(c4, pe, w1, w2, k_gain, is_key):
    b, g, n_chunks, width = c4.shape
    hidden = w1.shape[1]
    w2_arg = w2.astype(BF16) if is_key else w2.T.astype(BF16)
    out_block = (1, 1, n_chunks, HEAD_DIM) if is_key else (1, 1, HEAD_DIM, n_chunks)
    return pl.pallas_call(
        functools.partial(_compress_kernel, is_key=is_key),
        grid=(b, g),
        in_specs=[pl.BlockSpec((1, 1, n_chunks, width), lambda i, j: (i, j, 0, 0)),
                  _resident((1, 2 * width)), _resident((2 * width, hidden)),
                  _resident(w2_arg.shape), _resident((1, HEAD_DIM))],
        out_specs=pl.BlockSpec(out_block, lambda i, j: (i, j, 0, 0)),
        out_shape=jax.ShapeDtypeStruct((b, g) + out_block[2:], BF16),
        compiler_params=_params(2),
        name="compress_k" if is_key else "compress_v",
    )(c4, pe.reshape(1, 2 * width), w1.astype(BF16), w2_arg, k_gain.reshape(1, HEAD_DIM))


def _bias_kernel(rb_ref, bkt_near_ref, bkt_cmp_ref, near_o, cmp_o):
    h = pl.program_id(0)
    far = rb_ref[REL_BUCKETS - 1, h]

    def lookup(bkt):
        out = jnp.zeros(bkt.shape, F32)
        for k in range(REL_BUCKETS - 1):
            out = jnp.where(bkt == k, rb_ref[k, h] - far, out)
        return out

    key = lax.broadcasted_iota(jnp.int32, (T, T), 0)
    qry = lax.broadcasted_iota(jnp.int32, (T, T), 1)
    near_o[0, 0] = jnp.where(qry >= key, lookup(bkt_near_ref[0]), NEG)
    near_o[0, 1] = lookup(bkt_near_ref[1])
    cmp_o[0] = lookup(bkt_cmp_ref[...])


def _rel_bucket(dist):
    n = jnp.maximum(dist, 0)
    max_exact = REL_BUCKETS // 2
    nf = jnp.maximum(n, 1).astype(F32)
    large = max_exact + (jnp.log(nf / max_exact) / math.log(REL_MAX_DIST / max_exact)
                         * (REL_BUCKETS - max_exact)).astype(jnp.int32)
    large = jnp.minimum(large, REL_BUCKETS - 1)
    return jnp.where(n < max_exact, n, large)


def _bias_tiles(rel_bias):
    key = jnp.arange(T)[:, None]
    qry = jnp.arange(T)[None, :]
    bkt_near = jnp.stack([_rel_bucket(qry - key), _rel_bucket(qry - key + T)]).astype(jnp.int32)
    j = jnp.arange(CMP_NEAR)[:, None]
    bkt_cmp = _rel_bucket(qry - CMP_STRIDE * (j - CMP_PER_T) - (CMP_BLOCK - 1)).astype(jnp.int32)
    return pl.pallas_call(
        _bias_kernel,
        grid=(N_HEADS,),
        in_specs=[pl.BlockSpec(memory_space=pltpu.SMEM),
                  pl.BlockSpec((2, T, T), lambda h: (0, 0, 0)),
                  pl.BlockSpec((CMP_NEAR, T), lambda h: (0, 0))],
        out_specs=[pl.BlockSpec((1, 2, T, T), lambda h: (h // HEADS_PER_GROUP, 0, 0, h % HEADS_PER_GROUP)),
                   pl.BlockSpec((1, CMP_NEAR, T), lambda h: (h // HEADS_PER_GROUP, 0, h % HEADS_PER_GROUP))],
        out_shape=[jax.ShapeDtypeStruct((N_KV_GROUPS, 2, T, HEADS_PER_GROUP * T), F32),
                   jax.ShapeDtypeStruct((N_KV_GROUPS, CMP_NEAR, HEADS_PER_GROUP * T), F32)],
        compiler_params=_params(1),
        name="bias_tiles",
    )(rel_bias, bkt_near, bkt_cmp)


def _nsa_kernel(q_ref, ks_ref, vs_ref, kw_ref, vw_ref, kc_ref, vc_ref, smap_ref, gt_ref, nb_ref, cb_ref,
                o_ref,
                qcat_ref, sc_ref, selneg_ref, oc_ref,
                ms_ref, ls_ref, accs_ref, mw_ref, lw_ref, accw_ref, *, n_blk, n_cmp):
    g = pl.program_id(1)
    qi = pl.program_id(2)
    R = HEADS_PER_GROUP
    W = R * T
    qry = lax.broadcasted_iota(jnp.int32, (1, W), 1) & (T - 1)
    t = qi * T + qry

    def q_head(r):
        return q_ref[0, 0, r * HEAD_DIM:(r + 1) * HEAD_DIM, :]

    rowgrp = lax.shift_right_logical(lax.broadcasted_iota(jnp.int32, (GROUP_WIDTH, T), 0),
                                     int(math.log2(HEAD_DIM)))
    for r in range(R):
        q4 = jnp.concatenate([q_head(r).astype(F32)] * N_KV_GROUPS, axis=0)
        qcat_ref[:, r * T:(r + 1) * T] = jnp.where(rowgrp == g, q4, 0.0).astype(BF16)

    c_idx = lax.broadcasted_iota(jnp.int32, (n_cmp, W), 0)
    valid_c = (c_idx * CMP_STRIDE + (CMP_BLOCK - 1) <= t) & (c_idx < n_cmp - 1)
    near0 = pl.multiple_of(qi * CMP_PER_T, CMP_PER_T)
    sc_ref[0:CMP_PER_T, :] = jnp.zeros((CMP_PER_T, W), F32)
    sc_ref[CMP_PER_T:, :] = _dot(kc_ref[0, 0], jnp.concatenate([q_head(r) for r in range(R)], axis=1))
    sc_ref[pl.ds(near0, CMP_NEAR), :] = sc_ref[pl.ds(near0, CMP_NEAR), :] + cb_ref[0]
    s = jnp.where(valid_c, sc_ref[CMP_PER_T:, :], NEG)
    m = jnp.max(s, axis=0, keepdims=True)
    p = jnp.where(valid_c, jnp.exp(s - m), 0.0)
    l = jnp.sum(p, axis=0, keepdims=True)
    p = p * jnp.where(l > 0.0, 1.0 / jnp.where(l > 0.0, l, 1.0), 0.0)
    oc_ref[...] = _dot(vc_ref[0, 0], p.astype(BF16))
    psum = p[:, 0:T]
    for r in range(1, R):
        psum = psum + p[:, r * T:(r + 1) * T]

    imp = jnp.dot(smap_ref[...], psum, precision=HI, preferred_element_type=F32)
    blk = lax.broadcasted_iota(jnp.int32, (n_blk, T), 0)
    cur = lax.shift_right_logical(t[:, 0:T], int(math.log2(SEL_BLOCK)))
    forced = (blk == 0) | (blk == cur) | (blk == cur - 1)
    score = jnp.where(blk <= cur, imp + jnp.where(forced, FORCE, 0.0), -FORCE)
    selneg = jnp.full((n_blk, T), NEG, F32)
    for _ in range(min(N_SELECT, n_blk)):
        best = jnp.max(score, axis=0, keepdims=True)
        first = jnp.min(jnp.where(score == best, blk, n_blk), axis=0, keepdims=True)
        pick = blk == first
        selneg = jnp.where(pick, 0.0, selneg)
        score = jnp.where(pick, -jnp.inf, score)
    selneg_ref[...] = jnp.concatenate([selneg] * R, axis=1)

    for m_ref, l_ref, acc_ref in ((ms_ref, ls_ref, accs_ref), (mw_ref, lw_ref, accw_ref)):
        m_ref[...] = jnp.full(m_ref.shape, NEG, F32)
        l_ref[...] = jnp.zeros(l_ref.shape, F32)
        acc_ref[...] = jnp.zeros(acc_ref.shape, F32)

    def attend(k_tile, vt_tile, add, m_ref, l_ref, acc_ref):
        s = _dot(k_tile, qcat_ref[...]) + add
        m_old = m_ref[...]
        m_new = jnp.maximum(m_old, jnp.max(s, axis=0, keepdims=True))
        alpha = jnp.exp(m_old - m_new)
        p = jnp.exp(s - m_new)
        l_ref[...] = alpha * l_ref[...] + jnp.sum(p, axis=0, keepdims=True)
        acc_ref[...] = alpha * acc_ref[...] + _dot(vt_tile, p.astype(BF16))
        m_ref[...] = m_new

    def sel_add(kj):
        rows = [jnp.broadcast_to(selneg_ref[pl.ds(kj * BLK_PER_T + j, 1), :], (SEL_BLOCK, W))
                for j in range(BLK_PER_T)]
        return jnp.concatenate(rows, axis=0)

    attend(ks_ref[0, qi], vs_ref[0, qi, 0], nb_ref[0, 0] + sel_add(qi), ms_ref, ls_ref, accs_ref)
    attend(kw_ref[0, qi], vw_ref[0, qi, 0], nb_ref[0, 0], mw_ref, lw_ref, accw_ref)

    @pl.when(qi >= 1)
    def _():
        attend(ks_ref[0, qi - 1], vs_ref[0, qi - 1, 0], nb_ref[0, 1] + sel_add(qi - 1), ms_ref, ls_ref, accs_ref)
        attend(kw_ref[0, qi - 1], vw_ref[0, qi - 1, 0], nb_ref[0, 1], mw_ref, lw_ref, accw_ref)

    @pl.when(qi >= 2)
    def _():
        key = lax.broadcasted_iota(jnp.int32, (T, W), 0)
        tail = jnp.where(key > qry, 0.0, NEG)
        attend(kw_ref[0, qi - 2], vw_ref[0, qi - 2, 0], tail, mw_ref, lw_ref, accw_ref)

    def far_tile(kj, carry):
        attend(ks_ref[0, kj], vs_ref[0, kj, 0], sel_add(kj), ms_ref, ls_ref, accs_ref)
        return carry

    lax.fori_loop(0, qi - 1, far_tile, 0)

    o_s = accs_ref[...] / ls_ref[...]
    o_w = accw_ref[...] / lw_ref[...]
    outs = []
    for r in range(R):
        head = g * R + r
        sl = slice(r * T, (r + 1) * T)
        g_c = gt_ref[0, 0, pl.ds(head, 1), :]
        g_s = gt_ref[0, 0, pl.ds(N_HEADS + head, 1), :]
        g_w = gt_ref[0, 0, pl.ds(2 * N_HEADS + head, 1), :]
        outs.append(g_c * oc_ref[:, sl] + g_s * o_s[:, sl] + g_w * o_w[:, sl])
    o_ref[0] = jnp.concatenate(outs, axis=0).T


def _nsa(q_t, ks, vs_t, kw, vw_t, kc, vc_t, smap_t, gates_t, near_bias, cmp_bias, seq):
    b, nq = q_t.shape[0], q_t.shape[1]
    n_blk = seq // SEL_BLOCK
    n_cmp = seq // CMP_STRIDE
    W = HEADS_PER_GROUP * T
    kv_spec = pl.BlockSpec((1, nq, T, GROUP_WIDTH), lambda i, j, k: (i, 0, 0, 0))
    vt_spec = pl.BlockSpec((1, nq, 1, HEAD_DIM, T), lambda i, j, k: (i, 0, j, 0, 0))
    stat = pltpu.VMEM((1, W), F32)
    acc = pltpu.VMEM((HEAD_DIM, W), F32)
    return pl.pallas_call(
        functools.partial(_nsa_kernel, n_blk=n_blk, n_cmp=n_cmp),
        grid=(b, N_KV_GROUPS, nq),
        in_specs=[pl.BlockSpec((1, 1, GROUP_WIDTH, T), lambda i, j, k: (i, k, j, 0)),
                  kv_spec, vt_spec, kv_spec, vt_spec,
                  pl.BlockSpec((1, 1, n_cmp, HEAD_DIM), lambda i, j, k: (i, j, 0, 0)),
                  pl.BlockSpec((1, 1, HEAD_DIM, n_cmp), lambda i, j, k: (i, j, 0, 0)),
                  pl.BlockSpec((n_blk, n_cmp), lambda i, j, k: (0, 0)),
                  pl.BlockSpec((1, 1, gates_t.shape[2], T), lambda i, j, k: (i, k, 0, 0)),
                  pl.BlockSpec((1, 2, T, W), lambda i, j, k: (j, 0, 0, 0)),
                  pl.BlockSpec((1, CMP_NEAR, W), lambda i, j, k: (j, 0, 0))],
        out_specs=pl.BlockSpec((1, T, GROUP_WIDTH), lambda i, j, k: (i, k, j)),
        out_shape=jax.ShapeDtypeStruct((b, seq, N_HEADS * HEAD_DIM), F32),
        scratch_shapes=[pltpu.VMEM((GROUP_WIDTH, W), BF16),
                        pltpu.VMEM((n_cmp + CMP_PER_T, W), F32),
                        pltpu.VMEM((n_blk, W), F32),
                        acc, stat, stat, acc, stat, stat, acc],
        compiler_params=_params(3),
        name="nsa",
    )(q_t, ks, vs_t, kw, vw_t, kc, vc_t, smap_t, gates_t, near_bias, cmp_bias)


def _mem_kv_kernel(mem_ref, g_ref, w_ref, kg_ref, k_o, v_o):
    h = _rms_rows(mem_ref[0], g_ref[...]).astype(BF16)
    width = k_o.shape[2]
    hd = width // MEM_HEADS
    k = _dot(h, w_ref[:, :width])
    for i in range(MEM_HEADS):
        k_o[0, :, i * hd:(i + 1) * hd] = _rms_rows(k[:, i * hd:(i + 1) * hd], kg_ref[...]).astype(BF16)
    v_o[0] = _dot(h, w_ref[:, width:]).astype(BF16)


def _mem_attn_kernel(q_ref, k_ref, v_ref, qg_ref, o_ref):
    width = q_ref.shape[2]
    hd = width // MEM_HEADS
    for i in range(MEM_HEADS):
        sl = slice(i * hd, (i + 1) * hd)
        q = (_rms_rows(q_ref[0, :, sl], qg_ref[...]) * (hd ** -0.5)).astype(BF16)
        s = _dot_nt(q, k_ref[0, :, sl])
        p = jnp.exp(s - jnp.max(s, axis=-1, keepdims=True))
        p = p / jnp.sum(p, axis=-1, keepdims=True)
        o_ref[0, :, sl] = _dot(p.astype(BF16), v_ref[0, :, sl])


def _mem_attention(q_mem, mem, mem_norm_g, w_mem_kv, q_g, k_g):
    b, s, width = q_mem.shape
    n_mem, d = mem.shape[1], mem.shape[2]
    hd = width // MEM_HEADS
    kv_shape = jax.ShapeDtypeStruct((b, n_mem, width), BF16)
    kv_block = pl.BlockSpec((1, n_mem, width), lambda i: (i, 0, 0))
    km, vm = pl.pallas_call(
        _mem_kv_kernel,
        grid=(b,),
        in_specs=[pl.BlockSpec((1, n_mem, d), lambda i: (i, 0, 0)),
                  _resident((1, d)), _resident((d, 2 * width)), _resident((1, hd))],
        out_specs=[kv_block, kv_block],
        out_shape=[kv_shape, kv_shape],
        compiler_params=_params(1),
        name="mem_kv",
    )(mem, mem_norm_g.reshape(1, d), w_mem_kv.astype(BF16), k_g.reshape(1, hd))
    kv_block2 = pl.BlockSpec((1, n_mem, width), lambda i, j: (i, 0, 0))
    return pl.pallas_call(
        _mem_attn_kernel,
        grid=(b, s // MEM_TM),
        in_specs=[pl.BlockSpec((1, MEM_TM, width), lambda i, j: (i, j, 0)),
                  kv_block2, kv_block2, _resident((1, hd))],
        out_specs=pl.BlockSpec((1, MEM_TM, width), lambda i, j: (i, j, 0)),
        out_shape=jax.ShapeDtypeStruct((b, s, width), F32),
        compiler_params=_params(2),
        name="mem_attn",
    )(q_mem, km, vm, q_g.reshape(1, hd))


def _merge_kernel(x_ref, nsa_ref, mem_ref, cb_ref, cc_ref, cx_ref, hc_ref, hx_ref,
                  g1_ref, g2_ref, g3_ref, cw_ref, bias_ref, wo_ref, o_ref):
    j = pl.program_id(1)
    u = cc_ref[0] * cx_ref[0]
    halo = jnp.where(j > 0, hc_ref[0] * hx_ref[0], 0.0)
    prev1 = halo[HALO - 1:HALO, :]
    prev2 = halo[HALO - 2:HALO - 1, :]
    row = lax.broadcasted_iota(jnp.int32, u.shape, 0)
    u1 = jnp.where(row == 0, prev1, pltpu.roll(u, 1, 0))
    u2 = jnp.where(row == 0, prev2, jnp.where(row == 1, prev1, pltpu.roll(u, 2, 0)))
    y = cw_ref[0:1, :] * u2 + cw_ref[1:2, :] * u1 + cw_ref[2:3, :] * u
    o_conv = cb_ref[0] * (y + bias_ref[...])
    merged = (_sigmoid(g1_ref[0]) * nsa_ref[0] + _sigmoid(g2_ref[0]) * o_conv
              + _sigmoid(g3_ref[0]) * mem_ref[0])
    o_ref[0] = x_ref[0] + _dot(merged.astype(BF16), wo_ref[...])


def _merge(x, o_nsa, o_mem, conv_in, merge_g, conv_w, conv_b, w_out):
    b, s, d = x.shape
    tm = MERGE_TM

    def col(c):
        return pl.BlockSpec((1, tm, d), lambda i, j: (i, j, c))

    def halo(c):
        return pl.BlockSpec((1, HALO, d), lambda i, j: (i, jnp.maximum(j * (tm // HALO) - 1, 0), c))

    return pl.pallas_call(
        _merge_kernel,
        grid=(b, s // tm),
        in_specs=[col(0), col(0), col(0),
                  col(0), col(1), col(2), halo(1), halo(2),
                  col(0), col(1), col(2),
                  _resident((CONV_WIDTH, d)), _resident((1, d)), _resident((d, d))],
        out_specs=col(0),
        out_shape=jax.ShapeDtypeStruct((b, s, d), F32),
        compiler_params=_params(2),
        name="merge",
    )(x, o_nsa, o_mem, conv_in, conv_in, conv_in, conv_in, conv_in,
      merge_g, merge_g, merge_g, conv_w, conv_b.reshape(1, d), w_out.astype(BF16))


def _layer(x, mem, ffn1_norm_g, ffn1_w_in, ffn1_w_out, mix_norm_g, w_in, q_norm_g, k_norm_g,
           cmp_pe_k, cmp_w1_k, cmp_w2_k, cmp_pe_v, cmp_w1_v, cmp_w2_v, conv_w, conv_b,
           mem_norm_g, w_mem_kv, mem_q_norm_g, mem_k_norm_g, w_out,
           ffn2_norm_g, ffn2_w_in, ffn2_w_out, near_bias, cmp_bias):
    b, s, d = x.shape
    assert s % T == 0 and s % MERGE_TM == 0 and (b * s) % FFN_TM == 0
    assert WINDOW == 2 * T and REL_MAX_DIST <= T // 2
    nq = s // T
    d_q = N_HEADS * HEAD_DIM
    d_kv = N_KV_GROUPS * HEAD_DIM
    d_conv = conv_w.shape[1]
    d_mem = w_mem_kv.shape[1] // 2

    x = _ffn(x.reshape(b * s, d), ffn1_norm_g, ffn1_w_in, ffn1_w_out).reshape(b, s, d)

    o = 0
    w_q = w_in[:, o:o + d_q]; o += d_q
    w_kc, w_vc, w_ks, w_vs, w_kw, w_vw = [w_in[:, o + i * d_kv:o + (i + 1) * d_kv] for i in range(6)]
    o += 6 * d_kv
    w_g = w_in[:, o:o + 3 * N_HEADS]; o += 3 * N_HEADS
    w_conv = w_in[:, o:o + 3 * d_conv]; o += 3 * d_conv
    w_qm = w_in[:, o:o + d_mem]; o += d_mem
    w_mg = w_in[:, o:]

    w_rows = jnp.concatenate([w_ks, w_kw, w_kc, w_vc, w_conv, w_qm, w_mg], axis=1).astype(BF16)
    widths = (d_kv, d_kv, d_kv, d_kv, 3 * d_conv, d_mem, w_mg.shape[1])
    group_of = jnp.arange(d_kv) // HEAD_DIM
    block_diag = (group_of[:, None] == group_of[None, :]).astype(F32) / HEAD_DIM
    k_gain_row = jnp.tile(k_norm_g, N_KV_GROUPS).reshape(1, d_kv)

    def rows_out(wd, dt):
        return (pl.BlockSpec((1, ROW_TM, wd), lambda i, j: (i, j, 0)), jax.ShapeDtypeStruct((b, s, wd), dt))

    specs = [rows_out(wd, BF16 if i < 2 else F32) for i, wd in enumerate(widths)]
    ks, kw, kc, vc, conv_in, q_mem, merge_g = pl.pallas_call(
        functools.partial(_proj_rows_kernel, widths=widths),
        grid=(b, s // ROW_TM),
        in_specs=[pl.BlockSpec((1, ROW_TM, d), lambda i, j: (i, j, 0)),
                  _resident((1, d)), _resident(w_rows.shape), _resident((d_kv, d_kv)), _resident((1, d_kv))],
        out_specs=[sp[0] for sp in specs],
        out_shape=[sp[1] for sp in specs],
        compiler_params=_params(2),
        name="proj_rows",
    )(x, mix_norm_g.reshape(1, d), w_rows, block_diag, k_gain_row)

    w_g_t = w_g.reshape(d, N_HEADS, 3).transpose(2, 1, 0).reshape(3 * N_HEADS, d)
    n_gate_rows = 128
    w_g_t = jnp.pad(w_g_t, ((0, n_gate_rows - 3 * N_HEADS), (0, 0)))
    w_t = jnp.concatenate([w_q.T, w_vs.T, w_vw.T, w_g_t], axis=0).astype(BF16)

    def t_out(rows, dt):
        return (pl.BlockSpec((1, 1, rows, T), lambda i, j: (i, j, 0, 0)),
                jax.ShapeDtypeStruct((b, nq, rows, T), dt))

    t_specs = [t_out(d_q, BF16), t_out(d_kv, BF16), t_out(d_kv, BF16), t_out(n_gate_rows, F32)]
    q_t, vs_t, vw_t, gates_t = pl.pallas_call(
        functools.partial(_proj_t_kernel, d_q=d_q, d_kv=d_kv),
        grid=(b, nq),
        in_specs=[pl.BlockSpec((1, T, d), lambda i, j: (i, j, 0)),
                  _resident((1, d)), _resident(w_t.shape), _resident((HEAD_DIM, 1))],
        out_specs=[sp[0] for sp in t_specs],
        out_shape=[sp[1] for sp in t_specs],
        compiler_params=_params(2),
        name="proj_t",
    )(x, mix_norm_g.reshape(1, d), w_t, q_norm_g.reshape(HEAD_DIM, 1))

    n_chunks = s // CMP_STRIDE

    def chunked(a):
        return a.reshape(b, n_chunks, CMP_STRIDE, N_KV_GROUPS, HEAD_DIM).transpose(0, 3, 1, 2, 4) \
                .reshape(b, N_KV_GROUPS, n_chunks, CMP_STRIDE * HEAD_DIM)

    k_cmp = _compress(chunked(kc), cmp_pe_k, cmp_w1_k, cmp_w2_k, k_norm_g, True)
    v_cmp_t = _compress(chunked(vc), cmp_pe_v, cmp_w1_v, cmp_w2_v, k_norm_g, False)

    cs = jnp.arange(n_chunks)[None, :] * CMP_STRIDE
    ss = jnp.arange(s // SEL_BLOCK)[:, None] * SEL_BLOCK
    ov = jnp.maximum(jnp.minimum(cs + CMP_BLOCK, ss + SEL_BLOCK) - jnp.maximum(cs, ss), 0)
    smap_t = ov.astype(F32) / CMP_BLOCK

    o_nsa = _nsa(q_t, ks.reshape(b, nq, T, d_kv), vs_t.reshape(b, nq, N_KV_GROUPS, HEAD_DIM, T),
                 kw.reshape(b, nq, T, d_kv), vw_t.reshape(b, nq, N_KV_GROUPS, HEAD_DIM, T),
                 k_cmp, v_cmp_t, smap_t, gates_t, near_bias, cmp_bias, s)

    o_mem = _mem_attention(q_mem, mem, mem_norm_g, w_mem_kv, mem_q_norm_g, mem_k_norm_g)
    x = _merge(x, o_nsa, o_mem, conv_in, merge_g, conv_w, conv_b, w_out)
    x = _ffn(x.reshape(b * s, d), ffn2_norm_g, ffn2_w_in, ffn2_w_out).reshape(b, s, d)
    return x


def kernel(x, mem, ffn1_norm_g, ffn1_w_in, ffn1_w_out, mix_norm_g, w_in, q_norm_g, k_norm_g, cmp_pe_k, cmp_w1_k, cmp_w2_k, cmp_pe_v, cmp_w1_v, cmp_w2_v, conv_w, conv_b, mem_norm_g, w_mem_kv, mem_q_norm_g, mem_k_norm_g, w_out, ffn2_norm_g, ffn2_w_in, ffn2_w_out, rel_bias):
    near_bias, cmp_bias = _bias_tiles(rel_bias)
    for l in range(ffn1_norm_g.shape[0]):
        x = _layer(x, mem, ffn1_norm_g[l], ffn1_w_in[l], ffn1_w_out[l], mix_norm_g[l], w_in[l],
                   q_norm_g[l], k_norm_g[l], cmp_pe_k[l], cmp_w1_k[l], cmp_w2_k[l],
                   cmp_pe_v[l], cmp_w1_v[l], cmp_w2_v[l], conv_w[l], conv_b[l],
                   mem_norm_g[l], w_mem_kv[l], mem_q_norm_g[l], mem_k_norm_g[l], w_out[l],
                   ffn2_norm_g[l], ffn2_w_in[l], ffn2_w_out[l], near_bias, cmp_bias)
    return x
```

```python
import functools
import math

import jax
import jax.numpy as jnp
from jax import lax
from jax.experimental import pallas as pl
from jax.experimental.pallas import tpu as pltpu

N_HEADS = 16
HEAD_DIM = 64
N_KV_GROUPS = 4
HEADS_PER_GROUP = N_HEADS // N_KV_GROUPS
GROUP_WIDTH = HEADS_PER_GROUP * HEAD_DIM
CMP_BLOCK = 32
CMP_STRIDE = 16
SEL_BLOCK = 64
N_SELECT = 16
WINDOW = 512
FORCE = 1e4
CONV_WIDTH = 3
MEM_HEADS = 4
REL_BUCKETS = 32
REL_MAX_DIST = 128
EPS = 1e-6
NEG = -1e30

T = 256
BLK_PER_T = T // SEL_BLOCK
CMP_PER_T = T // CMP_STRIDE
CMP_NEAR = 2 * CMP_PER_T
FFN_TM = 512
ROW_TM = 256
MEM_TM = 512
MERGE_TM = 512
HALO = 8
BF16_SUBLANES = 16
V_ROWS = HEAD_DIM + BF16_SUBLANES
LOG2E = math.log2(math.e)
VMEM_LIMIT = 52 * 1024 * 1024

F32 = jnp.float32
BF16 = jnp.bfloat16
HI = lax.Precision.HIGHEST


def _dot(a, b):
    return jnp.dot(a, b, preferred_element_type=F32)


def _dot_nt(a, b):
    return lax.dot_general(a, b, (((1,), (1,)), ((), ())), preferred_element_type=F32)


def _rms_rows(xf, g):
    return xf * lax.rsqrt(jnp.mean(xf * xf, axis=-1, keepdims=True) + EPS) * g


def _sigmoid(x):
    return 1.0 / (1.0 + jnp.exp(-x))


def _resident(shape):
    zeros = (0,) * len(shape)
    return pl.BlockSpec(shape, lambda *_: zeros, pipeline_mode=pl.Buffered(1))


def _params(n_axes):
    return pltpu.CompilerParams(dimension_semantics=("arbitrary",) * n_axes,
                                vmem_limit_bytes=VMEM_LIMIT)


def _ffn_kernel(x_ref, g_ref, wa_ref, wb_ref, wo_ref, o_ref, *, ff_chunk):
    x = x_ref[...]
    h = _rms_rows(x, g_ref[...]).astype(BF16)
    d_ff = wa_ref.shape[1]
    acc = jnp.zeros(x.shape, F32)
    for lo in range(0, d_ff, ff_chunk):
        a = _dot(h, wa_ref[:, lo:lo + ff_chunk])
        b = _dot(h, wb_ref[:, lo:lo + ff_chunk])
        z = (a * _sigmoid(a) * b).astype(BF16)
        acc = acc + _dot(z, wo_ref[lo:lo + ff_chunk, :])
    o_ref[...] = x + 0.5 * acc


def _ffn(x2d, g, w_in, w_out):
    n, d = x2d.shape
    d_ff = w_out.shape[0]
    wa = w_in[:, :d_ff].astype(BF16)
    wb = w_in[:, d_ff:].astype(BF16)
    wo = w_out.astype(BF16)
    ff_chunk = d_ff // 2 if (d_ff // 2) % 128 == 0 else d_ff
    return pl.pallas_call(
        functools.partial(_ffn_kernel, ff_chunk=ff_chunk),
        grid=(n // FFN_TM,),
        in_specs=[pl.BlockSpec((FFN_TM, d), lambda i: (i, 0)),
                  _resident((1, d)), _resident((d, d_ff)), _resident((d, d_ff)), _resident((d_ff, d))],
        out_specs=pl.BlockSpec((FFN_TM, d), lambda i: (i, 0)),
        out_shape=jax.ShapeDtypeStruct((n, d), F32),
        compiler_params=_params(1),
        name="ffn",
    )(x2d, g.reshape(1, d), wa, wb, wo)


def _proj_rows_kernel(x_ref, g_ref, w_ref, bd_ref, kg_ref,
                      ks_o, kw_o, kc_o, vc_o, conv_o, qm_o, mg_o, *, widths):
    h = _rms_rows(x_ref[0], g_ref[...]).astype(BF16)

    def knorm(k):
        ms = jnp.dot(k * k, bd_ref[...], precision=HI, preferred_element_type=F32)
        return (k * lax.rsqrt(ms + EPS) * kg_ref[...]).astype(BF16)

    lo = 0
    outs = (ks_o, kw_o, kc_o, vc_o, conv_o, qm_o, mg_o)
    for idx, (o_ref, wd) in enumerate(zip(outs, widths)):
        y = _dot(h, w_ref[:, lo:lo + wd])
        o_ref[0] = knorm(y) if idx < 2 else y
        lo += wd


def _proj_t_kernel(x_ref, g_ref, wt_ref, qg_ref, q_o, vs_o, vw_o, gt_o, *, d_q, d_kv):
    h = _rms_rows(x_ref[0], g_ref[...]).astype(BF16)
    qg = qg_ref[...] * (HEAD_DIM ** -0.5 * LOG2E)
    for hd in range(d_q // HEAD_DIM):
        q = _dot_nt(wt_ref[hd * HEAD_DIM:(hd + 1) * HEAD_DIM, :], h)
        qn = q * lax.rsqrt(jnp.mean(q * q, axis=0, keepdims=True) + EPS) * qg
        q_o[0, 0, hd * HEAD_DIM:(hd + 1) * HEAD_DIM, :] = qn.astype(BF16)
    lo = d_q
    for v_o in (vs_o, vw_o):
        y = _dot_nt(wt_ref[lo:lo + d_kv, :], h).astype(BF16)
        for grp in range(d_kv // HEAD_DIM):
            v_o[0, 0, grp * V_ROWS:grp * V_ROWS + HEAD_DIM, :] = y[grp * HEAD_DIM:(grp + 1) * HEAD_DIM]
            v_o[0, 0, grp * V_ROWS + HEAD_DIM:(grp + 1) * V_ROWS, :] = jnp.ones((V_ROWS - HEAD_DIM, T), BF16)
        lo += d_kv
    gt_o[0, 0] = _sigmoid(_dot_nt(wt_ref[lo:, :], h))


def _compress_kernel(c_ref, pe_ref, w1_ref, w2_ref, kg_ref, o_ref, *, is_key):
    c = c_ref[0, 0]
    n_chunks, half = c.shape
    a = _dot((c + pe_ref[:, :half]).astype(BF16), w1_ref[:half, :])
    b = _dot((c + pe_ref[:, half:]).astype(BF16), w1_ref[half:, :])
    hid = a + pltpu.roll(b, n_chunks - 1, 0)
    hid = (hid * _sigmoid(hid)).astype(BF16)
    if is_key:
        y = _dot(hid, w2_ref[...])
        y = _rms_rows(y, kg_ref[...])
        row = lax.broadcasted_iota(jnp.int32, y.shape, 0)
        o_ref[0, 0] = jnp.where(row < n_chunks - 1, y, 0.0).astype(BF16)
    else:
        y = _dot_nt(w2_ref[...], hid)
        col = lax.broadcasted_iota(jnp.int32, y.shape, 1)
        o_ref[0, 0] = jnp.where(col < n_chunks - 1, y, 0.0).astype(BF16)


def _compress(c4, pe, w1, w2, k_gain, is_key):
    b, g, n_chunks, width = c4.shape
    hidden = w1.shape[1]
    w2_arg = w2.astype(BF16) if is_key else w2.T.astype(BF16)
    out_block = (1, 1, n_chunks, HEAD_DIM) if is_key else (1, 1, HEAD_DIM, n_chunks)
    return pl.pallas_call(
        functools.partial(_compress_kernel, is_key=is_key),
        grid=(b, g),
        in_specs=[pl.BlockSpec((1, 1, n_chunks, width), lambda i, j: (i, j, 0, 0)),
                  _resident((1, 2 * width)), _resident((2 * width, hidden)),
                  _resident(w2_arg.shape), _resident((1, HEAD_DIM))],
        out_specs=pl.BlockSpec(out_block, lambda i, j: (i, j, 0, 0)),
        out_shape=jax.ShapeDtypeStruct((b, g) + out_block[2:], BF16),
        compiler_params=_params(2),
        name="compress_k" if is_key else "compress_v",
    )(c4, pe.reshape(1, 2 * width), w1.astype(BF16), w2_arg, k_gain.reshape(1, HEAD_DIM))


def _bias_kernel(rb_ref, bkt_near_ref, bkt_cmp_ref, near_o, cmp_o):
    h = pl.program_id(0)
    far = rb_ref[REL_BUCKETS - 1, h]

    def lookup(bkt):
        out = jnp.zeros(bkt.shape, F32)
        for k in range(REL_BUCKETS - 1):
            out = jnp.where(bkt == k, (rb_ref[k, h] - far) * LOG2E, out)
        return out

    key = lax.broadcasted_iota(jnp.int32, (T, T), 0)
    qry = lax.broadcasted_iota(jnp.int32, (T, T), 1)
    near_o[0, 0] = jnp.where(qry >= key, lookup(bkt_near_ref[0]), NEG)
    near_o[0, 1] = lookup(bkt_near_ref[1])
    cmp_o[0] = lookup(bkt_cmp_ref[...])


def _rel_bucket(dist):
    n = jnp.maximum(dist, 0)
    max_exact = REL_BUCKETS // 2
    nf = jnp.maximum(n, 1).astype(F32)
    large = max_exact + (jnp.log(nf / max_exact) / math.log(REL_MAX_DIST / max_exact)
                         * (REL_BUCKETS - max_exact)).astype(jnp.int32)
    large = jnp.minimum(large, REL_BUCKETS - 1)
    return jnp.where(n < max_exact, n, large)


def _bias_tiles(rel_bias):
    key = jnp.arange(T)[:, None]
    qry = jnp.arange(T)[None, :]
    bkt_near = jnp.stack([_rel_bucket(qry - key), _rel_bucket(qry - key + T)]).astype(jnp.int32)
    j = jnp.arange(CMP_NEAR)[:, None]
    bkt_cmp = _rel_bucket(qry - CMP_STRIDE * (j - CMP_PER_T) - (CMP_BLOCK - 1)).astype(jnp.int32)
    return pl.pallas_call(
        _bias_kernel,
        grid=(N_HEADS,),
        in_specs=[pl.BlockSpec(memory_space=pltpu.SMEM),
                  pl.BlockSpec((2, T, T), lambda h: (0, 0, 0)),
                  pl.BlockSpec((CMP_NEAR, T), lambda h: (0, 0))],
        out_specs=[pl.BlockSpec((1, 2, T, T), lambda h: (h // HEADS_PER_GROUP, 0, 0, h % HEADS_PER_GROUP)),
                   pl.BlockSpec((1, CMP_NEAR, T), lambda h: (h // HEADS_PER_GROUP, 0, h % HEADS_PER_GROUP))],
        out_shape=[jax.ShapeDtypeStruct((N_KV_GROUPS, 2, T, HEADS_PER_GROUP * T), F32),
                   jax.ShapeDtypeStruct((N_KV_GROUPS, CMP_NEAR, HEADS_PER_GROUP * T), F32)],
        compiler_params=_params(1),
        name="bias_tiles",
    )(rel_bias, bkt_near, bkt_cmp)


def _nsa_kernel(q_ref, ks_ref, vs_ref, kw_ref, vw_ref, kc_ref, vc_ref, smap_ref, gt_ref, nb_ref, cb_ref,
                o_ref,
                qcat_ref, sc_ref, selneg_ref, oc_ref, s0_ref, s1_ref, p0_ref, p1_ref,
                ms_ref, accs_ref, mw_ref, accw_ref, *, n_blk, n_cmp):
    g = pl.program_id(1)
    qi = pl.program_id(2)
    R = HEADS_PER_GROUP
    W = R * T
    qry = lax.broadcasted_iota(jnp.int32, (1, W), 1) & (T - 1)
    t = qi * T + qry

    def q_head(r):
        return q_ref[0, 0, r * HEAD_DIM:(r + 1) * HEAD_DIM, :]

    rowgrp = lax.shift_right_logical(lax.broadcasted_iota(jnp.int32, (GROUP_WIDTH, T), 0),
                                     int(math.log2(HEAD_DIM)))
    for r in range(R):
        q4 = jnp.concatenate([q_head(r).astype(F32)] * N_KV_GROUPS, axis=0)
        qcat_ref[:, r * T:(r + 1) * T] = jnp.where(rowgrp == g, q4, 0.0).astype(BF16)

    c_idx = lax.broadcasted_iota(jnp.int32, (n_cmp, W), 0)
    valid_c = (c_idx * CMP_STRIDE + (CMP_BLOCK - 1) <= t) & (c_idx < n_cmp - 1)
    near0 = pl.multiple_of(qi * CMP_PER_T, CMP_PER_T)
    sc_ref[0:CMP_PER_T, :] = jnp.zeros((CMP_PER_T, W), F32)
    sc_ref[CMP_PER_T:, :] = _dot(kc_ref[0, 0], jnp.concatenate([q_head(r) for r in range(R)], axis=1))
    sc_ref[pl.ds(near0, CMP_NEAR), :] = sc_ref[pl.ds(near0, CMP_NEAR), :] + cb_ref[0]
    s = jnp.where(valid_c, sc_ref[CMP_PER_T:, :], NEG)
    m = jnp.max(s, axis=0, keepdims=True)
    p = jnp.where(valid_c, jnp.exp2(s - m), 0.0)
    l = jnp.sum(p, axis=0, keepdims=True)
    p = p * jnp.where(l > 0.0, 1.0 / jnp.where(l > 0.0, l, 1.0), 0.0)
    oc_ref[...] = _dot(vc_ref[0, 0], p.astype(BF16))
    psum = p[:, 0:T]
    for r in range(1, R):
        psum = psum + p[:, r * T:(r + 1) * T]

    imp = jnp.dot(smap_ref[...], psum, precision=HI, preferred_element_type=F32)
    blk = lax.broadcasted_iota(jnp.int32, (n_blk, T), 0)
    cur = lax.shift_right_logical(t[:, 0:T], int(math.log2(SEL_BLOCK)))
    forced = (blk == 0) | (blk == cur) | (blk == cur - 1)
    score = jnp.where(blk <= cur, imp + jnp.where(forced, FORCE, 0.0), -FORCE)
    selneg = jnp.full((n_blk, T), NEG, F32)
    for _ in range(min(N_SELECT, n_blk)):
        best = jnp.max(score, axis=0, keepdims=True)
        first = jnp.min(jnp.where(score == best, blk, n_blk), axis=0, keepdims=True)
        pick = blk == first
        selneg = jnp.where(pick, 0.0, selneg)
        score = jnp.where(pick, -jnp.inf, score)
    selneg_ref[...] = jnp.concatenate([selneg] * R, axis=1)

    for m_ref, acc_ref in ((ms_ref, accs_ref), (mw_ref, accw_ref)):
        m_ref[...] = jnp.full(m_ref.shape, NEG, F32)
        acc_ref[...] = jnp.zeros(acc_ref.shape, F32)

    def softmax_tile(s_ref, p_ref, m_ref, kj):
        rows = None
        if kj is not None:
            rows = [selneg_ref[pl.ds(kj * BLK_PER_T + j, 1), :] for j in range(BLK_PER_T)]
        top = None
        for j in range(BLK_PER_T):
            blk_max = s_ref[j * SEL_BLOCK:(j + 1) * SEL_BLOCK, :].reshape(SEL_BLOCK // 8, 8, W).max(axis=0)
            if rows is not None:
                blk_max = blk_max + rows[j]
            top = blk_max if top is None else jnp.maximum(top, blk_max)
        m_old = m_ref[...]
        m_new = jnp.maximum(m_old, jnp.max(top, axis=0, keepdims=True))
        for j in range(BLK_PER_T):
            shift = m_new if rows is None else m_new - rows[j]
            sl = slice(j * SEL_BLOCK, (j + 1) * SEL_BLOCK)
            p_ref[sl, :] = jnp.exp2(s_ref[sl, :] - shift).astype(BF16)
        m_ref[...] = m_new
        return jnp.exp2(m_old - m_new)

    def attend_now(k_tile, vt_tile, bias, kj, s_ref, p_ref, m_ref, acc_ref):
        s = _dot(k_tile, qcat_ref[...])
        s_ref[...] = s if bias is None else s + bias
        alpha = softmax_tile(s_ref, p_ref, m_ref, kj)
        acc_ref[...] = alpha * acc_ref[...] + _dot(vt_tile, p_ref[...])

    slc = (s0_ref, p0_ref, ms_ref, accs_ref)
    win = (s1_ref, p1_ref, mw_ref, accw_ref)
    attend_now(ks_ref[0, qi], vs_ref[0, qi, 0], nb_ref[0, 0], qi, *slc)
    attend_now(kw_ref[0, qi], vw_ref[0, qi, 0], nb_ref[0, 0], None, *win)

    @pl.when(qi >= 1)
    def _():
        attend_now(ks_ref[0, qi - 1], vs_ref[0, qi - 1, 0], nb_ref[0, 1], qi - 1, *slc)
        attend_now(kw_ref[0, qi - 1], vw_ref[0, qi - 1, 0], nb_ref[0, 1], None, *win)

    @pl.when(qi >= 2)
    def _():
        key = lax.broadcasted_iota(jnp.int32, (T, W), 0)
        tail = jnp.where(key > qry, 0.0, NEG)
        attend_now(kw_ref[0, qi - 2], vw_ref[0, qi - 2, 0], tail, None, *win)

    n_far = qi - 1
    n_pairs = jnp.maximum(n_far, 0) >> 1

    @pl.when((n_far > 0) & ((n_far & 1) == 1))
    def _():
        attend_now(ks_ref[0, n_far - 1], vs_ref[0, n_far - 1, 0], None, n_far - 1, *slc)

    p1_ref[...] = jnp.zeros(p1_ref.shape, BF16)
    s0_ref[...] = _dot(ks_ref[0, 0], qcat_ref[...])

    def pair(i, carry):
        a = 2 * i
        b = a + 1
        pv_prev = _dot(vs_ref[0, jnp.maximum(b - 2, 0), 0], p1_ref[...])
        s1_ref[...] = _dot(ks_ref[0, b], qcat_ref[...])
        alpha_a = softmax_tile(s0_ref, p0_ref, ms_ref, a)
        accs_ref[...] = alpha_a * (accs_ref[...] + pv_prev)
        pv_a = _dot(vs_ref[0, a, 0], p0_ref[...])
        s0_ref[...] = _dot(ks_ref[0, jnp.minimum(a + 2, qi)], qcat_ref[...])
        alpha_b = softmax_tile(s1_ref, p1_ref, ms_ref, b)
        accs_ref[...] = alpha_b * (accs_ref[...] + pv_a)
        return carry

    lax.fori_loop(0, n_pairs, pair, 0)
    accs_ref[...] = accs_ref[...] + _dot(vs_ref[0, jnp.maximum(2 * n_pairs - 1, 0), 0], p1_ref[...])

    o_s = accs_ref[0:HEAD_DIM, :] / accs_ref[HEAD_DIM:HEAD_DIM + 1, :]
    o_w = accw_ref[0:HEAD_DIM, :] / accw_ref[HEAD_DIM:HEAD_DIM + 1, :]
    outs = []
    for r in range(R):
        head = g * R + r
        sl = slice(r * T, (r + 1) * T)
        g_c = gt_ref[0, 0, pl.ds(head, 1), :]
        g_s = gt_ref[0, 0, pl.ds(N_HEADS + head, 1), :]
        g_w = gt_ref[0, 0, pl.ds(2 * N_HEADS + head, 1), :]
        outs.append(g_c * oc_ref[:, sl] + g_s * o_s[:, sl] + g_w * o_w[:, sl])
    o_ref[0] = jnp.concatenate(outs, axis=0).T


def _nsa(q_t, ks, vs_t, kw, vw_t, kc, vc_t, smap_t, gates_t, near_bias, cmp_bias, seq):
    b, nq = q_t.shape[0], q_t.shape[1]
    n_blk = seq // SEL_BLOCK
    n_cmp = seq // CMP_STRIDE
    W = HEADS_PER_GROUP * T
    kv_spec = pl.BlockSpec((1, nq, T, GROUP_WIDTH), lambda i, j, k: (i, 0, 0, 0))
    vt_spec = pl.BlockSpec((1, nq, 1, V_ROWS, T), lambda i, j, k: (i, 0, j, 0, 0))
    stat = pltpu.VMEM((1, W), F32)
    acc = pltpu.VMEM((V_ROWS, W), F32)
    scores = pltpu.VMEM((T, W), F32)
    probs = pltpu.VMEM((T, W), BF16)
    return pl.pallas_call(
        functools.partial(_nsa_kernel, n_blk=n_blk, n_cmp=n_cmp),
        grid=(b, N_KV_GROUPS, nq),
        in_specs=[pl.BlockSpec((1, 1, GROUP_WIDTH, T), lambda i, j, k: (i, k, j, 0)),
                  kv_spec, vt_spec, kv_spec, vt_spec,
                  pl.BlockSpec((1, 1, n_cmp, HEAD_DIM), lambda i, j, k: (i, j, 0, 0)),
                  pl.BlockSpec((1, 1, HEAD_DIM, n_cmp), lambda i, j, k: (i, j, 0, 0)),
                  pl.BlockSpec((n_blk, n_cmp), lambda i, j, k: (0, 0)),
                  pl.BlockSpec((1, 1, gates_t.shape[2], T), lambda i, j, k: (i, k, 0, 0)),
                  pl.BlockSpec((1, 2, T, W), lambda i, j, k: (j, 0, 0, 0)),
                  pl.BlockSpec((1, CMP_NEAR, W), lambda i, j, k: (j, 0, 0))],
        out_specs=pl.BlockSpec((1, T, GROUP_WIDTH), lambda i, j, k: (i, k, j)),
        out_shape=jax.ShapeDtypeStruct((b, seq, N_HEADS * HEAD_DIM), F32),
        scratch_shapes=[pltpu.VMEM((GROUP_WIDTH, W), BF16),
                        pltpu.VMEM((n_cmp + CMP_PER_T, W), F32),
                        pltpu.VMEM((n_blk, W), F32),
                        pltpu.VMEM((HEAD_DIM, W), F32), scores, scores, probs, probs,
                        stat, acc, stat, acc],
        compiler_params=_params(3),
        name="nsa",
    )(q_t, ks, vs_t, kw, vw_t, kc, vc_t, smap_t, gates_t, near_bias, cmp_bias)


def _mem_kv_kernel(mem_ref, g_ref, w_ref, kg_ref, k_o, v_o):
    h = _rms_rows(mem_ref[0], g_ref[...]).astype(BF16)
    width = k_o.shape[2]
    hd = width // MEM_HEADS
    k = _dot(h, w_ref[:, :width])
    for i in range(MEM_HEADS):
        k_o[0, :, i * hd:(i + 1) * hd] = _rms_rows(k[:, i * hd:(i + 1) * hd], kg_ref[...]).astype(BF16)
    v_o[0] = _dot(h, w_ref[:, width:]).astype(BF16)


def _mem_attn_kernel(q_ref, k_ref, v_ref, qg_ref, o_ref):
    width = q_ref.shape[2]
    hd = width // MEM_HEADS
    for i in range(MEM_HEADS):
        sl = slice(i * hd, (i + 1) * hd)
        q = (_rms_rows(q_ref[0, :, sl], qg_ref[...]) * (hd ** -0.5)).astype(BF16)
        s = _dot_nt(q, k_ref[0, :, sl])
        p = jnp.exp(s - jnp.max(s, axis=-1, keepdims=True))
        p = p / jnp.sum(p, axis=-1, keepdims=True)
        o_ref[0, :, sl] = _dot(p.astype(BF16), v_ref[0, :, sl])


def _mem_attention(q_mem, mem, mem_norm_g, w_mem_kv, q_g, k_g):
    b, s, width = q_mem.shape
    n_mem, d = mem.shape[1], mem.shape[2]
    hd = width // MEM_HEADS
    kv_shape = jax.ShapeDtypeStruct((b, n_mem, width), BF16)
    kv_block = pl.BlockSpec((1, n_mem, width), lambda i: (i, 0, 0))
    km, vm = pl.pallas_call(
        _mem_kv_kernel,
        grid=(b,),
        in_specs=[pl.BlockSpec((1, n_mem, d), lambda i: (i, 0, 0)),
                  _resident((1, d)), _resident((d, 2 * width)), _resident((1, hd))],
        out_specs=[kv_block, kv_block],
        out_shape=[kv_shape, kv_shape],
        compiler_params=_params(1),
        name="mem_kv",
    )(mem, mem_norm_g.reshape(1, d), w_mem_kv.astype(BF16), k_g.reshape(1, hd))
    kv_block2 = pl.BlockSpec((1, n_mem, width), lambda i, j: (i, 0, 0))
    return pl.pallas_call(
        _mem_attn_kernel,
        grid=(b, s // MEM_TM),
        in_specs=[pl.BlockSpec((1, MEM_TM, width), lambda i, j: (i, j, 0)),
                  kv_block2, kv_block2, _resident((1, hd))],
        out_specs=pl.BlockSpec((1, MEM_TM, width), lambda i, j: (i, j, 0)),
        out_shape=jax.ShapeDtypeStruct((b, s, width), F32),
        compiler_params=_params(2),
        name="mem_attn",
    )(q_mem, km, vm, q_g.reshape(1, hd))


def _merge_kernel(x_ref, nsa_ref, mem_ref, cb_ref, cc_ref, cx_ref, hc_ref, hx_ref,
                  g1_ref, g2_ref, g3_ref, cw_ref, bias_ref, wo_ref, o_ref):
    j = pl.program_id(1)
    u = cc_ref[0] * cx_ref[0]
    halo = jnp.where(j > 0, hc_ref[0] * hx_ref[0], 0.0)
    prev1 = halo[HALO - 1:HALO, :]
    prev2 = halo[HALO - 2:HALO - 1, :]
    row = lax.broadcasted_iota(jnp.int32, u.shape, 0)
    u1 = jnp.where(row == 0, prev1, pltpu.roll(u, 1, 0))
    u2 = jnp.where(row == 0, prev2, jnp.where(row == 1, prev1, pltpu.roll(u, 2, 0)))
    y = cw_ref[0:1, :] * u2 + cw_ref[1:2, :] * u1 + cw_ref[2:3, :] * u
    o_conv = cb_ref[0] * (y + bias_ref[...])
    merged = (_sigmoid(g1_ref[0]) * nsa_ref[0] + _sigmoid(g2_ref[0]) * o_conv
              + _sigmoid(g3_ref[0]) * mem_ref[0])
    o_ref[0] = x_ref[0] + _dot(merged.astype(BF16), wo_ref[...])


def _merge(x, o_nsa, o_mem, conv_in, merge_g, conv_w, conv_b, w_out):
    b, s, d = x.shape
    tm = MERGE_TM

    def col(c):
        return pl.BlockSpec((1, tm, d), lambda i, j: (i, j, c))

    def halo(c):
        return pl.BlockSpec((1, HALO, d), lambda i, j: (i, jnp.maximum(j * (tm // HALO) - 1, 0), c))

    return pl.pallas_call(
        _merge_kernel,
        grid=(b, s // tm),
        in_specs=[col(0), col(0), col(0),
                  col(0), col(1), col(2), halo(1), halo(2),
                  col(0), col(1), col(2),
                  _resident((CONV_WIDTH, d)), _resident((1, d)), _resident((d, d))],
        out_specs=col(0),
        out_shape=jax.ShapeDtypeStruct((b, s, d), F32),
        compiler_params=_params(2),
        name="merge",
    )(x, o_nsa, o_mem, conv_in, conv_in, conv_in, conv_in, conv_in,
      merge_g, merge_g, merge_g, conv_w, conv_b.reshape(1, d), w_out.astype(BF16))


def _layer(x, mem, ffn1_norm_g, ffn1_w_in, ffn1_w_out, mix_norm_g, w_in, q_norm_g, k_norm_g,
           cmp_pe_k, cmp_w1_k, cmp_w2_k, cmp_pe_v, cmp_w1_v, cmp_w2_v, conv_w, conv_b,
           mem_norm_g, w_mem_kv, mem_q_norm_g, mem_k_norm_g, w_out,
           ffn2_norm_g, ffn2_w_in, ffn2_w_out, near_bias, cmp_bias):
    b, s, d = x.shape
    assert s % T == 0 and s % MERGE_TM == 0 and (b * s) % FFN_TM == 0
    assert WINDOW == 2 * T and REL_MAX_DIST <= T // 2
    nq = s // T
    d_q = N_HEADS * HEAD_DIM
    d_kv = N_KV_GROUPS * HEAD_DIM
    d_conv = conv_w.shape[1]
    d_mem = w_mem_kv.shape[1] // 2

    x = _ffn(x.reshape(b * s, d), ffn1_norm_g, ffn1_w_in, ffn1_w_out).reshape(b, s, d)

    o = 0
    w_q = w_in[:, o:o + d_q]; o += d_q
    w_kc, w_vc, w_ks, w_vs, w_kw, w_vw = [w_in[:, o + i * d_kv:o + (i + 1) * d_kv] for i in range(6)]
    o += 6 * d_kv
    w_g = w_in[:, o:o + 3 * N_HEADS]; o += 3 * N_HEADS
    w_conv = w_in[:, o:o + 3 * d_conv]; o += 3 * d_conv
    w_qm = w_in[:, o:o + d_mem]; o += d_mem
    w_mg = w_in[:, o:]

    w_rows = jnp.concatenate([w_ks, w_kw, w_kc, w_vc, w_conv, w_qm, w_mg], axis=1).astype(BF16)
    widths = (d_kv, d_kv, d_kv, d_kv, 3 * d_conv, d_mem, w_mg.shape[1])
    group_of = jnp.arange(d_kv) // HEAD_DIM
    block_diag = (group_of[:, None] == group_of[None, :]).astype(F32) / HEAD_DIM
    k_gain_row = jnp.tile(k_norm_g, N_KV_GROUPS).reshape(1, d_kv)

    def rows_out(wd, dt):
        return (pl.BlockSpec((1, ROW_TM, wd), lambda i, j: (i, j, 0)), jax.ShapeDtypeStruct((b, s, wd), dt))

    specs = [rows_out(wd, BF16 if i < 2 else F32) for i, wd in enumerate(widths)]
    ks, kw, kc, vc, conv_in, q_mem, merge_g = pl.pallas_call(
        functools.partial(_proj_rows_kernel, widths=widths),
        grid=(b, s // ROW_TM),
        in_specs=[pl.BlockSpec((1, ROW_TM, d), lambda i, j: (i, j, 0)),
                  _resident((1, d)), _resident(w_rows.shape), _resident((d_kv, d_kv)), _resident((1, d_kv))],
        out_specs=[sp[0] for sp in specs],
        out_shape=[sp[1] for sp in specs],
        compiler_params=_params(2),
        name="proj_rows",
    )(x, mix_norm_g.reshape(1, d), w_rows, block_diag, k_gain_row)

    w_g_t = w_g.reshape(d, N_HEADS, 3).transpose(2, 1, 0).reshape(3 * N_HEADS, d)
    n_gate_rows = 128
    w_g_t = jnp.pad(w_g_t, ((0, n_gate_rows - 3 * N_HEADS), (0, 0)))
    w_t = jnp.concatenate([w_q.T, w_vs.T, w_vw.T, w_g_t], axis=0).astype(BF16)

    def t_out(rows, dt):
        return (pl.BlockSpec((1, 1, rows, T), lambda i, j: (i, j, 0, 0)),
                jax.ShapeDtypeStruct((b, nq, rows, T), dt))

    v_rows = N_KV_GROUPS * V_ROWS
    t_specs = [t_out(d_q, BF16), t_out(v_rows, BF16), t_out(v_rows, BF16), t_out(n_gate_rows, F32)]
    q_t, vs_t, vw_t, gates_t = pl.pallas_call(
        functools.partial(_proj_t_kernel, d_q=d_q, d_kv=d_kv),
        grid=(b, nq),
        in_specs=[pl.BlockSpec((1, T, d), lambda i, j: (i, j, 0)),
                  _resident((1, d)), _resident(w_t.shape), _resident((HEAD_DIM, 1))],
        out_specs=[sp[0] for sp in t_specs],
        out_shape=[sp[1] for sp in t_specs],
        compiler_params=_params(2),
        name="proj_t",
    )(x, mix_norm_g.reshape(1, d), w_t, q_norm_g.reshape(HEAD_DIM, 1))

    n_chunks = s // CMP_STRIDE

    def chunked(a):
        return a.reshape(b, n_chunks, CMP_STRIDE, N_KV_GROUPS, HEAD_DIM).transpose(0, 3, 1, 2, 4) \
                .reshape(b, N_KV_GROUPS, n_chunks, CMP_STRIDE * HEAD_DIM)

    k_cmp = _compress(chunked(kc), cmp_pe_k, cmp_w1_k, cmp_w2_k, k_norm_g, True)
    v_cmp_t = _compress(chunked(vc), cmp_pe_v, cmp_w1_v, cmp_w2_v, k_norm_g, False)

    cs = jnp.arange(n_chunks)[None, :] * CMP_STRIDE
    ss = jnp.arange(s // SEL_BLOCK)[:, None] * SEL_BLOCK
    ov = jnp.maximum(jnp.minimum(cs + CMP_BLOCK, ss + SEL_BLOCK) - jnp.maximum(cs, ss), 0)
    smap_t = ov.astype(F32) / CMP_BLOCK

    o_nsa = _nsa(q_t, ks.reshape(b, nq, T, d_kv), vs_t.reshape(b, nq, N_KV_GROUPS, V_ROWS, T),
                 kw.reshape(b, nq, T, d_kv), vw_t.reshape(b, nq, N_KV_GROUPS, V_ROWS, T),
                 k_cmp, v_cmp_t, smap_t, gates_t, near_bias, cmp_bias, s)

    o_mem = _mem_attention(q_mem, mem, mem_norm_g, w_mem_kv, mem_q_norm_g, mem_k_norm_g)
    x = _merge(x, o_nsa, o_mem, conv_in, merge_g, conv_w, conv_b, w_out)
    x = _ffn(x.reshape(b * s, d), ffn2_norm_g, ffn2_w_in, ffn2_w_out).reshape(b, s, d)
    return x


def kernel(x, mem, ffn1_norm_g, ffn1_w_in, ffn1_w_out, mix_norm_g, w_in, q_norm_g, k_norm_g, cmp_pe_k, cmp_w1_k, cmp_w2_k, cmp_pe_v, cmp_w1_v, cmp_w2_v, conv_w, conv_b, mem_norm_g, w_mem_kv, mem_q_norm_g, mem_k_norm_g, w_out, ffn2_norm_g, ffn2_w_in, ffn2_w_out, rel_bias):
    near_bias, cmp_bias = _bias_tiles(rel_bias)
    for l in range(ffn1_norm_g.shape[0]):
        x = _layer(x, mem, ffn1_norm_g[l], ffn1_w_in[l], ffn1_w_out[l], mix_norm_g[l], w_in[l],
                   q_norm_g[l], k_norm_g[l], cmp_pe_k[l], cmp_w1_k[l], cmp_w2_k[l],
                   cmp_pe_v[l], cmp_w1_v[l], cmp_w2_v[l], conv_w[l], conv_b[l],
                   mem_norm_g[l], w_mem_kv[l], mem_q_norm_g[l], mem_k_norm_g[l], w_out[l],
                   ffn2_norm_g[l], ffn2_w_in[l], ffn2_w_out[l], near_bias, cmp_bias)
    return x
```

```python
import functools
import math

import jax
import jax.numpy as jnp
from jax import lax
from jax.experimental import pallas as pl
from jax.experimental.pallas import tpu as pltpu

N_HEADS = 16
HEAD_DIM = 64
N_KV_GROUPS = 4
HEADS_PER_GROUP = N_HEADS // N_KV_GROUPS
GROUP_WIDTH = HEADS_PER_GROUP * HEAD_DIM
CMP_BLOCK = 32
CMP_STRIDE = 16
SEL_BLOCK = 64
N_SELECT = 16
WINDOW = 512
FORCE = 1e4
CONV_WIDTH = 3
MEM_HEADS = 4
REL_BUCKETS = 32
REL_MAX_DIST = 128
EPS = 1e-6
NEG = -1e30

T = 256
BLK_PER_T = T // SEL_BLOCK
CMP_PER_T = T // CMP_STRIDE
CMP_NEAR = 2 * CMP_PER_T
N_NEAR = WINDOW // T + 1
FFN_TM = 512
ROW_TM = 256
MEM_TM = 512
MERGE_TM = 512
HALO = 8
LANES = 128
BF16_SUBLANES = 16
V_ROWS = HEAD_DIM + BF16_SUBLANES
LOG2E = math.log2(math.e)
VMEM_LIMIT = 52 * 1024 * 1024

F32 = jnp.float32
BF16 = jnp.bfloat16
HI = lax.Precision.HIGHEST


def _dot(a, b):
    return jnp.dot(a, b, preferred_element_type=F32)


def _dot_nt(a, b):
    return lax.dot_general(a, b, (((1,), (1,)), ((), ())), preferred_element_type=F32)


def _rms_rows(xf, g):
    return xf * lax.rsqrt(jnp.mean(xf * xf, axis=-1, keepdims=True) + EPS) * g


def _sigmoid(x):
    return 1.0 / (1.0 + jnp.exp(-x))


def _resident(shape):
    zeros = (0,) * len(shape)
    return pl.BlockSpec(shape, lambda *_: zeros, pipeline_mode=pl.Buffered(1))


def _params(n_axes):
    return pltpu.CompilerParams(dimension_semantics=("arbitrary",) * n_axes,
                                vmem_limit_bytes=VMEM_LIMIT)


def _ffn_kernel(x_ref, g_ref, wa_ref, wb_ref, wo_ref, o_ref, *, ff_chunk):
    x = x_ref[...]
    h = _rms_rows(x, g_ref[...]).astype(BF16)
    d_ff = wa_ref.shape[1]
    acc = jnp.zeros(x.shape, F32)
    for lo in range(0, d_ff, ff_chunk):
        a = _dot(h, wa_ref[:, lo:lo + ff_chunk])
        b = _dot(h, wb_ref[:, lo:lo + ff_chunk])
        z = (a * _sigmoid(a) * b).astype(BF16)
        acc = acc + _dot(z, wo_ref[lo:lo + ff_chunk, :])
    o_ref[...] = x + 0.5 * acc


def _ffn(x2d, g, w_in, w_out):
    n, d = x2d.shape
    d_ff = w_out.shape[0]
    wa = w_in[:, :d_ff].astype(BF16)
    wb = w_in[:, d_ff:].astype(BF16)
    wo = w_out.astype(BF16)
    ff_chunk = d_ff // 2 if (d_ff // 2) % 128 == 0 else d_ff
    return pl.pallas_call(
        functools.partial(_ffn_kernel, ff_chunk=ff_chunk),
        grid=(n // FFN_TM,),
        in_specs=[pl.BlockSpec((FFN_TM, d), lambda i: (i, 0)),
                  _resident((1, d)), _resident((d, d_ff)), _resident((d, d_ff)), _resident((d_ff, d))],
        out_specs=pl.BlockSpec((FFN_TM, d), lambda i: (i, 0)),
        out_shape=jax.ShapeDtypeStruct((n, d), F32),
        compiler_params=_params(1),
        name="ffn",
    )(x2d, g.reshape(1, d), wa, wb, wo)


def _proj_rows_kernel(x_ref, g_ref, w_ref, bd_ref, kg_ref,
                      ks_o, kw_o, kc_o, vc_o, conv_o, qm_o, mg_o, *, widths):
    h = _rms_rows(x_ref[0], g_ref[...]).astype(BF16)

    def knorm(k):
        ms = jnp.dot(k * k, bd_ref[...], precision=HI, preferred_element_type=F32)
        return (k * lax.rsqrt(ms + EPS) * kg_ref[...]).astype(BF16)

    lo = 0
    outs = (ks_o, kw_o, kc_o, vc_o, conv_o, qm_o, mg_o)
    for idx, (o_ref, wd) in enumerate(zip(outs, widths)):
        y = _dot(h, w_ref[:, lo:lo + wd])
        o_ref[0] = knorm(y) if idx < 2 else y
        lo += wd


def _proj_t_kernel(x_ref, g_ref, wt_ref, qg_ref, q_o, vs_o, vw_o, gt_o, *, d_q, d_kv):
    h = _rms_rows(x_ref[0], g_ref[...]).astype(BF16)
    qg = qg_ref[...] * (HEAD_DIM ** -0.5 * LOG2E)
    for hd in range(d_q // HEAD_DIM):
        q = _dot_nt(wt_ref[hd * HEAD_DIM:(hd + 1) * HEAD_DIM, :], h)
        qn = q * lax.rsqrt(jnp.mean(q * q, axis=0, keepdims=True) + EPS) * qg
        q_o[0, 0, hd * HEAD_DIM:(hd + 1) * HEAD_DIM, :] = qn.astype(BF16)
    lo = d_q
    for v_o in (vs_o, vw_o):
        y = _dot_nt(wt_ref[lo:lo + d_kv, :], h).astype(BF16)
        for grp in range(d_kv // HEAD_DIM):
            v_o[0, 0, grp * V_ROWS:grp * V_ROWS + HEAD_DIM, :] = y[grp * HEAD_DIM:(grp + 1) * HEAD_DIM]
            v_o[0, 0, grp * V_ROWS + HEAD_DIM:(grp + 1) * V_ROWS, :] = jnp.ones((V_ROWS - HEAD_DIM, T), BF16)
        lo += d_kv
    gt_o[0, 0] = _sigmoid(_dot_nt(wt_ref[lo:, :], h))


def _compress_kernel(c_ref, pe_ref, w1_ref, w2_ref, kg_ref, o_ref, *, is_key):
    c = c_ref[0, 0]
    n_chunks, half = c.shape
    a = _dot((c + pe_ref[:, :half]).astype(BF16), w1_ref[:half, :])
    b = _dot((c + pe_ref[:, half:]).astype(BF16), w1_ref[half:, :])
    hid = a + pltpu.roll(b, n_chunks - 1, 0)
    hid = (hid * _sigmoid(hid)).astype(BF16)
    if is_key:
        y = _dot(hid, w2_ref[...])
        y = _rms_rows(y, kg_ref[...])
        row = lax.broadcasted_iota(jnp.int32, y.shape, 0)
        o_ref[0, 0] = jnp.where(row < n_chunks - 1, y, 0.0).astype(BF16)
    else:
        y = _dot_nt(w2_ref[...], hid)
        col = lax.broadcasted_iota(jnp.int32, y.shape, 1)
        o_ref[0, 0] = jnp.where(col < n_chunks - 1, y, 0.0).astype(BF16)


def _compress(c4, pe, w1, w2, k_gain, is_key):
    b, g, n_chunks, width = c4.shape
    hidden = w1.shape[1]
    w2_arg = w2.astype(BF16) if is_key else w2.T.astype(BF16)
    out_block = (1, 1, n_chunks, HEAD_DIM) if is_key else (1, 1, HEAD_DIM, n_chunks)
    return pl.pallas_call(
        functools.partial(_compress_kernel, is_key=is_key),
        grid=(b, g),
        in_specs=[pl.BlockSpec((1, 1, n_chunks, width), lambda i, j: (i, j, 0, 0)),
                  _resident((1, 2 * width)), _resident((2 * width, hidden)),
                  _resident(w2_arg.shape), _resident((1, HEAD_DIM))],
        out_specs=pl.BlockSpec(out_block, lambda i, j: (i, j, 0, 0)),
        out_shape=jax.ShapeDtypeStruct((b, g) + out_block[2:], BF16),
        compiler_params=_params(2),
        name="compress_k" if is_key else "compress_v",
    )(c4, pe.reshape(1, 2 * width), w1.astype(BF16), w2_arg, k_gain.reshape(1, HEAD_DIM))


def _bias_kernel(rb_ref, bkt_near_ref, bkt_cmp_ref, near_o, cmp_o):
    h = pl.program_id(0)
    far = rb_ref[REL_BUCKETS - 1, h]

    def lookup(bkt):
        out = jnp.zeros(bkt.shape, F32)
        for k in range(REL_BUCKETS - 1):
            out = jnp.where(bkt == k, (rb_ref[k, h] - far) * LOG2E, out)
        return out

    key = lax.broadcasted_iota(jnp.int32, (T, T), 0)
    qry = lax.broadcasted_iota(jnp.int32, (T, T), 1)
    near_o[0, 0] = jnp.where(qry >= key, lookup(bkt_near_ref[0]), NEG)
    near_o[0, 1] = lookup(bkt_near_ref[1])
    near_o[0, 2] = jnp.where(key > qry, 0.0, NEG)
    cmp_o[0] = lookup(bkt_cmp_ref[...])


def _rel_bucket(dist):
    n = jnp.maximum(dist, 0)
    max_exact = REL_BUCKETS // 2
    nf = jnp.maximum(n, 1).astype(F32)
    large = max_exact + (jnp.log(nf / max_exact) / math.log(REL_MAX_DIST / max_exact)
                         * (REL_BUCKETS - max_exact)).astype(jnp.int32)
    large = jnp.minimum(large, REL_BUCKETS - 1)
    return jnp.where(n < max_exact, n, large)


def _bias_tiles(rel_bias):
    key = jnp.arange(T)[:, None]
    qry = jnp.arange(T)[None, :]
    bkt_near = jnp.stack([_rel_bucket(qry - key), _rel_bucket(qry - key + T)]).astype(jnp.int32)
    j = jnp.arange(CMP_NEAR)[:, None]
    bkt_cmp = _rel_bucket(qry - CMP_STRIDE * (j - CMP_PER_T) - (CMP_BLOCK - 1)).astype(jnp.int32)
    return pl.pallas_call(
        _bias_kernel,
        grid=(N_HEADS,),
        in_specs=[pl.BlockSpec(memory_space=pltpu.SMEM),
                  pl.BlockSpec((2, T, T), lambda h: (0, 0, 0)),
                  pl.BlockSpec((CMP_NEAR, T), lambda h: (0, 0))],
        out_specs=[pl.BlockSpec((1, N_NEAR, T, T), lambda h: (h // HEADS_PER_GROUP, 0, 0, h % HEADS_PER_GROUP)),
                   pl.BlockSpec((1, CMP_NEAR, T), lambda h: (h // HEADS_PER_GROUP, 0, h % HEADS_PER_GROUP))],
        out_shape=[jax.ShapeDtypeStruct((N_KV_GROUPS, N_NEAR, T, HEADS_PER_GROUP * T), F32),
                   jax.ShapeDtypeStruct((N_KV_GROUPS, CMP_NEAR, HEADS_PER_GROUP * T), F32)],
        compiler_params=_params(1),
        name="bias_tiles",
    )(rel_bias, bkt_near, bkt_cmp)


def _nsa_kernel(q_ref, ks_ref, vs_ref, kw_ref, vw_ref, kc_ref, vc_ref, gt_ref, nb_ref, cb_ref,
                o_ref,
                qcat_ref, sc_ref, psum_ref, selneg_ref, oc_ref, s0_ref, s1_ref, s2_ref, p0_ref, p1_ref, p2_ref,
                ms_ref, accs_ref, mw_ref, accw_ref, *, n_blk, n_cmp):
    g = pl.program_id(1)
    qi = pl.program_id(2)
    R = HEADS_PER_GROUP
    W = R * T
    qry = lax.broadcasted_iota(jnp.int32, (1, W), 1) & (T - 1)
    t = qi * T + qry

    def q_head(r):
        return q_ref[0, 0, r * HEAD_DIM:(r + 1) * HEAD_DIM, :]

    rowgrp = lax.shift_right_logical(lax.broadcasted_iota(jnp.int32, (GROUP_WIDTH, T), 0),
                                     int(math.log2(HEAD_DIM)))
    for r in range(R):
        q4 = jnp.concatenate([q_head(r).astype(F32)] * N_KV_GROUPS, axis=0)
        qcat_ref[:, r * T:(r + 1) * T] = jnp.where(rowgrp == g, q4, 0.0).astype(BF16)

    c_idx = lax.broadcasted_iota(jnp.int32, (n_cmp, W), 0)
    valid_c = (c_idx * CMP_STRIDE + (CMP_BLOCK - 1) <= t) & (c_idx < n_cmp - 1)
    near0 = pl.multiple_of(qi * CMP_PER_T, CMP_PER_T)
    sc_ref[0:CMP_PER_T, :] = jnp.zeros((CMP_PER_T, W), F32)
    sc_ref[CMP_PER_T:, :] = _dot(kc_ref[0, 0], jnp.concatenate([q_head(r) for r in range(R)], axis=1))
    sc_ref[pl.ds(near0, CMP_NEAR), :] = sc_ref[pl.ds(near0, CMP_NEAR), :] + cb_ref[0]
    s = jnp.where(valid_c, sc_ref[CMP_PER_T:, :], NEG)
    m = jnp.max(s, axis=0, keepdims=True)
    p = jnp.where(valid_c, jnp.exp2(s - m), 0.0)
    l = jnp.sum(p, axis=0, keepdims=True)
    p = p * jnp.where(l > 0.0, 1.0 / jnp.where(l > 0.0, l, 1.0), 0.0)
    oc_ref[...] = _dot(vc_ref[0, 0], p.astype(BF16))
    psum = p[:, 0:T]
    for r in range(1, R):
        psum = psum + p[:, r * T:(r + 1) * T]

    for ln in range(T // LANES):
        psum_ref[ln, 0:8, :] = jnp.zeros((8, LANES), F32)
        psum_ref[ln, 8:, :] = psum[:, ln * LANES:(ln + 1) * LANES]

    def every_fourth(off):
        return jnp.concatenate([psum_ref[ln, pl.ds(8 + off, n_blk, stride=SEL_BLOCK // CMP_STRIDE), :]
                                for ln in range(T // LANES)], axis=1)

    imp = every_fourth(0) + every_fourth(1) + every_fourth(2) + 0.5 * (every_fourth(3) + every_fourth(-1))
    blk = lax.broadcasted_iota(jnp.int32, (n_blk, T), 0)
    cur = lax.shift_right_logical(t[:, 0:T], int(math.log2(SEL_BLOCK)))
    forced = (blk == 0) | (blk == cur) | (blk == cur - 1)
    score = jnp.where(blk <= cur, imp + jnp.where(forced, FORCE, 0.0), -FORCE)
    selneg = jnp.full((n_blk, T), NEG, F32)
    for _ in range(min(N_SELECT, n_blk)):
        best = jnp.max(score, axis=0, keepdims=True)
        first = jnp.min(jnp.where(score == best, blk, n_blk), axis=0, keepdims=True)
        pick = blk == first
        selneg = jnp.where(pick, 0.0, selneg)
        score = jnp.where(pick, -jnp.inf, score)
    selneg_ref[...] = jnp.concatenate([selneg] * R, axis=1)

    for m_ref, acc_ref in ((ms_ref, accs_ref), (mw_ref, accw_ref)):
        m_ref[...] = jnp.full(m_ref.shape, NEG, F32)
        acc_ref[...] = jnp.zeros(acc_ref.shape, F32)

    def softmax_tile(s_ref, p_ref, m_ref, rows):
        top = None
        for j in range(BLK_PER_T):
            blk_max = s_ref[j * SEL_BLOCK:(j + 1) * SEL_BLOCK, :].reshape(SEL_BLOCK // 8, 8, W).max(axis=0)
            if rows is not None:
                blk_max = blk_max + rows[j]
            top = blk_max if top is None else jnp.maximum(top, blk_max)
        m_old = m_ref[...]
        m_new = jnp.maximum(m_old, jnp.max(top, axis=0, keepdims=True))
        for j in range(BLK_PER_T):
            shift = m_new if rows is None else m_new - rows[j]
            sl = slice(j * SEL_BLOCK, (j + 1) * SEL_BLOCK)
            p_ref[sl, :] = jnp.exp2(s_ref[sl, :] - shift).astype(BF16)
        m_ref[...] = m_new
        return jnp.exp2(m_old - m_new)

    s_bufs = (s0_ref, s1_ref, s2_ref)
    p_bufs = (p0_ref, p1_ref, p2_ref)
    neg_row = jnp.full((1, W), NEG, F32)

    for back in range(WINDOW // T + 1):
        kj = jnp.maximum(qi - back, 0)
        s_bufs[back][...] = _dot(kw_ref[0, kj], qcat_ref[...]) + nb_ref[0, back]
        rows = None if back == 0 else [jnp.where(qi >= back, 0.0, neg_row)] * BLK_PER_T
        alpha = softmax_tile(s_bufs[back], p_bufs[back], mw_ref, rows)
        accw_ref[...] = alpha * accw_ref[...] + _dot(vw_ref[0, kj, 0], p_bufs[back][...])

    def tile_of(y):
        return jnp.where(y == 0, qi, jnp.where(y == 1, jnp.maximum(qi - 1, 0), jnp.clip(y - 2, 0, qi)))

    def sel_rows(y):
        kj = tile_of(y)
        return [jnp.where(y <= qi, selneg_ref[pl.ds(kj * BLK_PER_T + j, 1), :], neg_row)
                for j in range(BLK_PER_T)]

    s0_ref[...] = _dot(ks_ref[0, tile_of(0)], qcat_ref[...]) + nb_ref[0, 0]
    s1_ref[...] = _dot(ks_ref[0, tile_of(1)], qcat_ref[...]) + nb_ref[0, 1]
    p2_ref[...] = jnp.zeros(p2_ref.shape, BF16)

    def trip(i, carry):
        for k in range(3):
            y = 3 * i + k
            pv = _dot(vs_ref[0, tile_of(y - 1), 0], p_bufs[(k + 2) % 3][...])
            s_bufs[(k + 2) % 3][...] = _dot(ks_ref[0, tile_of(y + 2)], qcat_ref[...])
            alpha = softmax_tile(s_bufs[k], p_bufs[k], ms_ref, sel_rows(y))
            accs_ref[...] = alpha * (accs_ref[...] + pv)
        return carry

    n_trips = lax.div(qi + 3, 3)
    lax.fori_loop(0, n_trips, trip, 0)
    accs_ref[...] = accs_ref[...] + _dot(vs_ref[0, tile_of(3 * n_trips - 1), 0], p2_ref[...])

    o_s = accs_ref[0:HEAD_DIM, :] / accs_ref[HEAD_DIM:HEAD_DIM + 1, :]
    o_w = accw_ref[0:HEAD_DIM, :] / accw_ref[HEAD_DIM:HEAD_DIM + 1, :]
    outs = []
    for r in range(R):
        head = g * R + r
        sl = slice(r * T, (r + 1) * T)
        g_c = gt_ref[0, 0, pl.ds(head, 1), :]
        g_s = gt_ref[0, 0, pl.ds(N_HEADS + head, 1), :]
        g_w = gt_ref[0, 0, pl.ds(2 * N_HEADS + head, 1), :]
        outs.append(g_c * oc_ref[:, sl] + g_s * o_s[:, sl] + g_w * o_w[:, sl])
    o_ref[0] = jnp.concatenate(outs, axis=0).T


def _nsa(q_t, ks, vs_t, kw, vw_t, kc, vc_t, gates_t, near_bias, cmp_bias, seq):
    b, nq = q_t.shape[0], q_t.shape[1]
    n_blk = seq // SEL_BLOCK
    n_cmp = seq // CMP_STRIDE
    W = HEADS_PER_GROUP * T
    kv_spec = pl.BlockSpec((1, nq, T, GROUP_WIDTH), lambda i, j, k: (i, 0, 0, 0))
    vt_spec = pl.BlockSpec((1, nq, 1, V_ROWS, T), lambda i, j, k: (i, 0, j, 0, 0))
    stat = pltpu.VMEM((1, W), F32)
    acc = pltpu.VMEM((V_ROWS, W), F32)
    scores = pltpu.VMEM((T, W), F32)
    probs = pltpu.VMEM((T, W), BF16)
    return pl.pallas_call(
        functools.partial(_nsa_kernel, n_blk=n_blk, n_cmp=n_cmp),
        grid=(b, N_KV_GROUPS, nq),
        in_specs=[pl.BlockSpec((1, 1, GROUP_WIDTH, T), lambda i, j, k: (i, k, j, 0)),
                  kv_spec, vt_spec, kv_spec, vt_spec,
                  pl.BlockSpec((1, 1, n_cmp, HEAD_DIM), lambda i, j, k: (i, j, 0, 0)),
                  pl.BlockSpec((1, 1, HEAD_DIM, n_cmp), lambda i, j, k: (i, j, 0, 0)),
                  pl.BlockSpec((1, 1, gates_t.shape[2], T), lambda i, j, k: (i, k, 0, 0)),
                  pl.BlockSpec((1, N_NEAR, T, W), lambda i, j, k: (j, 0, 0, 0)),
                  pl.BlockSpec((1, CMP_NEAR, W), lambda i, j, k: (j, 0, 0))],
        out_specs=pl.BlockSpec((1, T, GROUP_WIDTH), lambda i, j, k: (i, k, j)),
        out_shape=jax.ShapeDtypeStruct((b, seq, N_HEADS * HEAD_DIM), F32),
        scratch_shapes=[pltpu.VMEM((GROUP_WIDTH, W), BF16),
                        pltpu.VMEM((n_cmp + CMP_PER_T, W), F32),
                        pltpu.VMEM((T // LANES, n_cmp + 8, LANES), F32),
                        pltpu.VMEM((n_blk, W), F32),
                        pltpu.VMEM((HEAD_DIM, W), F32), scores, scores, scores, probs, probs, probs,
                        stat, acc, stat, acc],
        compiler_params=_params(3),
        name="nsa",
    )(q_t, ks, vs_t, kw, vw_t, kc, vc_t, gates_t, near_bias, cmp_bias)


def _mem_kv_kernel(mem_ref, g_ref, w_ref, kg_ref, k_o, v_o):
    h = _rms_rows(mem_ref[0], g_ref[...]).astype(BF16)
    width = k_o.shape[2]
    hd = width // MEM_HEADS
    k = _dot(h, w_ref[:, :width])
    for i in range(MEM_HEADS):
        k_o[0, :, i * hd:(i + 1) * hd] = _rms_rows(k[:, i * hd:(i + 1) * hd], kg_ref[...]).astype(BF16)
    v_o[0] = _dot(h, w_ref[:, width:]).astype(BF16)


def _mem_attn_kernel(q_ref, k_ref, v_ref, qg_ref, o_ref):
    width = q_ref.shape[2]
    hd = width // MEM_HEADS
    for i in range(MEM_HEADS):
        sl = slice(i * hd, (i + 1) * hd)
        q = (_rms_rows(q_ref[0, :, sl], qg_ref[...]) * (hd ** -0.5)).astype(BF16)
        s = _dot_nt(q, k_ref[0, :, sl])
        p = jnp.exp(s - jnp.max(s, axis=-1, keepdims=True))
        p = p / jnp.sum(p, axis=-1, keepdims=True)
        o_ref[0, :, sl] = _dot(p.astype(BF16), v_ref[0, :, sl])


def _mem_attention(q_mem, mem, mem_norm_g, w_mem_kv, q_g, k_g):
    b, s, width = q_mem.shape
    n_mem, d = mem.shape[1], mem.shape[2]
    hd = width // MEM_HEADS
    kv_shape = jax.ShapeDtypeStruct((b, n_mem, width), BF16)
    kv_block = pl.BlockSpec((1, n_mem, width), lambda i: (i, 0, 0))
    km, vm = pl.pallas_call(
        _mem_kv_kernel,
        grid=(b,),
        in_specs=[pl.BlockSpec((1, n_mem, d), lambda i: (i, 0, 0)),
                  _resident((1, d)), _resident((d, 2 * width)), _resident((1, hd))],
        out_specs=[kv_block, kv_block],
        out_shape=[kv_shape, kv_shape],
        compiler_params=_params(1),
        name="mem_kv",
    )(mem, mem_norm_g.reshape(1, d), w_mem_kv.astype(BF16), k_g.reshape(1, hd))
    kv_block2 = pl.BlockSpec((1, n_mem, width), lambda i, j: (i, 0, 0))
    return pl.pallas_call(
        _mem_attn_kernel,
        grid=(b, s // MEM_TM),
        in_specs=[pl.BlockSpec((1, MEM_TM, width), lambda i, j: (i, j, 0)),
                  kv_block2, kv_block2, _resident((1, hd))],
        out_specs=pl.BlockSpec((1, MEM_TM, width), lambda i, j: (i, j, 0)),
        out_shape=jax.ShapeDtypeStruct((b, s, width), F32),
        compiler_params=_params(2),
        name="mem_attn",
    )(q_mem, km, vm, q_g.reshape(1, hd))


def _merge_kernel(x_ref, nsa_ref, mem_ref, cb_ref, cc_ref, cx_ref, hc_ref, hx_ref,
                  g1_ref, g2_ref, g3_ref, cw_ref, bias_ref, wo_ref, o_ref):
    j = pl.program_id(1)
    u = cc_ref[0] * cx_ref[0]
    halo = jnp.where(j > 0, hc_ref[0] * hx_ref[0], 0.0)
    prev1 = halo[HALO - 1:HALO, :]
    prev2 = halo[HALO - 2:HALO - 1, :]
    row = lax.broadcasted_iota(jnp.int32, u.shape, 0)
    u1 = jnp.where(row == 0, prev1, pltpu.roll(u, 1, 0))
    u2 = jnp.where(row == 0, prev2, jnp.where(row == 1, prev1, pltpu.roll(u, 2, 0)))
    y = cw_ref[0:1, :] * u2 + cw_ref[1:2, :] * u1 + cw_ref[2:3, :] * u
    o_conv = cb_ref[0] * (y + bias_ref[...])
    merged = (_sigmoid(g1_ref[0]) * nsa_ref[0] + _sigmoid(g2_ref[0]) * o_conv
              + _sigmoid(g3_ref[0]) * mem_ref[0])
    o_ref[0] = x_ref[0] + _dot(merged.astype(BF16), wo_ref[...])


def _merge(x, o_nsa, o_mem, conv_in, merge_g, conv_w, conv_b, w_out):
    b, s, d = x.shape
    tm = MERGE_TM

    def col(c):
        return pl.BlockSpec((1, tm, d), lambda i, j: (i, j, c))

    def halo(c):
        return pl.BlockSpec((1, HALO, d), lambda i, j: (i, jnp.maximum(j * (tm // HALO) - 1, 0), c))

    return pl.pallas_call(
        _merge_kernel,
        grid=(b, s // tm),
        in_specs=[col(0), col(0), col(0),
                  col(0), col(1), col(2), halo(1), halo(2),
                  col(0), col(1), col(2),
                  _resident((CONV_WIDTH, d)), _resident((1, d)), _resident((d, d))],
        out_specs=col(0),
        out_shape=jax.ShapeDtypeStruct((b, s, d), F32),
        compiler_params=_params(2),
        name="merge",
    )(x, o_nsa, o_mem, conv_in, conv_in, conv_in, conv_in, conv_in,
      merge_g, merge_g, merge_g, conv_w, conv_b.reshape(1, d), w_out.astype(BF16))


def _layer(x, mem, ffn1_norm_g, ffn1_w_in, ffn1_w_out, mix_norm_g, w_in, q_norm_g, k_norm_g,
           cmp_pe_k, cmp_w1_k, cmp_w2_k, cmp_pe_v, cmp_w1_v, cmp_w2_v, conv_w, conv_b,
           mem_norm_g, w_mem_kv, mem_q_norm_g, mem_k_norm_g, w_out,
           ffn2_norm_g, ffn2_w_in, ffn2_w_out, near_bias, cmp_bias):
    b, s, d = x.shape
    assert s % T == 0 and s % MERGE_TM == 0 and (b * s) % FFN_TM == 0
    assert WINDOW == 2 * T and REL_MAX_DIST <= T // 2
    assert SEL_BLOCK == 4 * CMP_STRIDE and CMP_BLOCK == 2 * CMP_STRIDE
    nq = s // T
    d_q = N_HEADS * HEAD_DIM
    d_kv = N_KV_GROUPS * HEAD_DIM
    d_conv = conv_w.shape[1]
    d_mem = w_mem_kv.shape[1] // 2

    x = _ffn(x.reshape(b * s, d), ffn1_norm_g, ffn1_w_in, ffn1_w_out).reshape(b, s, d)

    o = 0
    w_q = w_in[:, o:o + d_q]; o += d_q
    w_kc, w_vc, w_ks, w_vs, w_kw, w_vw = [w_in[:, o + i * d_kv:o + (i + 1) * d_kv] for i in range(6)]
    o += 6 * d_kv
    w_g = w_in[:, o:o + 3 * N_HEADS]; o += 3 * N_HEADS
    w_conv = w_in[:, o:o + 3 * d_conv]; o += 3 * d_conv
    w_qm = w_in[:, o:o + d_mem]; o += d_mem
    w_mg = w_in[:, o:]

    w_rows = jnp.concatenate([w_ks, w_kw, w_kc, w_vc, w_conv, w_qm, w_mg], axis=1).astype(BF16)
    widths = (d_kv, d_kv, d_kv, d_kv, 3 * d_conv, d_mem, w_mg.shape[1])
    group_of = jnp.arange(d_kv) // HEAD_DIM
    block_diag = (group_of[:, None] == group_of[None, :]).astype(F32) / HEAD_DIM
    k_gain_row = jnp.tile(k_norm_g, N_KV_GROUPS).reshape(1, d_kv)

    def rows_out(wd, dt):
        return (pl.BlockSpec((1, ROW_TM, wd), lambda i, j: (i, j, 0)), jax.ShapeDtypeStruct((b, s, wd), dt))

    specs = [rows_out(wd, BF16 if i < 2 else F32) for i, wd in enumerate(widths)]
    ks, kw, kc, vc, conv_in, q_mem, merge_g = pl.pallas_call(
        functools.partial(_proj_rows_kernel, widths=widths),
        grid=(b, s // ROW_TM),
        in_specs=[pl.BlockSpec((1, ROW_TM, d), lambda i, j: (i, j, 0)),
                  _resident((1, d)), _resident(w_rows.shape), _resident((d_kv, d_kv)), _resident((1, d_kv))],
        out_specs=[sp[0] for sp in specs],
        out_shape=[sp[1] for sp in specs],
        compiler_params=_params(2),
        name="proj_rows",
    )(x, mix_norm_g.reshape(1, d), w_rows, block_diag, k_gain_row)

    w_g_t = w_g.reshape(d, N_HEADS, 3).transpose(2, 1, 0).reshape(3 * N_HEADS, d)
    n_gate_rows = 128
    w_g_t = jnp.pad(w_g_t, ((0, n_gate_rows - 3 * N_HEADS), (0, 0)))
    w_t = jnp.concatenate([w_q.T, w_vs.T, w_vw.T, w_g_t], axis=0).astype(BF16)

    def t_out(rows, dt):
        return (pl.BlockSpec((1, 1, rows, T), lambda i, j: (i, j, 0, 0)),
                jax.ShapeDtypeStruct((b, nq, rows, T), dt))

    v_rows = N_KV_GROUPS * V_ROWS
    t_specs = [t_out(d_q, BF16), t_out(v_rows, BF16), t_out(v_rows, BF16), t_out(n_gate_rows, F32)]
    q_t, vs_t, vw_t, gates_t = pl.pallas_call(
        functools.partial(_proj_t_kernel, d_q=d_q, d_kv=d_kv),
        grid=(b, nq),
        in_specs=[pl.BlockSpec((1, T, d), lambda i, j: (i, j, 0)),
                  _resident((1, d)), _resident(w_t.shape), _resident((HEAD_DIM, 1))],
        out_specs=[sp[0] for sp in t_specs],
        out_shape=[sp[1] for sp in t_specs],
        compiler_params=_params(2),
        name="proj_t",
    )(x, mix_norm_g.reshape(1, d), w_t, q_norm_g.reshape(HEAD_DIM, 1))

    n_chunks = s // CMP_STRIDE

    def chunked(a):
        return a.reshape(b, n_chunks, CMP_STRIDE, N_KV_GROUPS, HEAD_DIM).transpose(0, 3, 1, 2, 4) \
                .reshape(b, N_KV_GROUPS, n_chunks, CMP_STRIDE * HEAD_DIM)

    k_cmp = _compress(chunked(kc), cmp_pe_k, cmp_w1_k, cmp_w2_k, k_norm_g, True)
    v_cmp_t = _compress(chunked(vc), cmp_pe_v, cmp_w1_v, cmp_w2_v, k_norm_g, False)

    o_nsa = _nsa(q_t, ks.reshape(b, nq, T, d_kv), vs_t.reshape(b, nq, N_KV_GROUPS, V_ROWS, T),
                 kw.reshape(b, nq, T, d_kv), vw_t.reshape(b, nq, N_KV_GROUPS, V_ROWS, T),
                 k_cmp, v_cmp_t, gates_t, near_bias, cmp_bias, s)

    o_mem = _mem_attention(q_mem, mem, mem_norm_g, w_mem_kv, mem_q_norm_g, mem_k_norm_g)
    x = _merge(x, o_nsa, o_mem, conv_in, merge_g, conv_w, conv_b, w_out)
    x = _ffn(x.reshape(b * s, d), ffn2_norm_g, ffn2_w_in, ffn2_w_out).reshape(b, s, d)
    return x


def kernel(x, mem, ffn1_norm_g, ffn1_w_in, ffn1_w_out, mix_norm_g, w_in, q_norm_g, k_norm_g, cmp_pe_k, cmp_w1_k, cmp_w2_k, cmp_pe_v, cmp_w1_v, cmp_w2_v, conv_w, conv_b, mem_norm_g, w_mem_kv, mem_q_norm_g, mem_k_norm_g, w_out, ffn2_norm_g, ffn2_w_in, ffn2_w_out, rel_bias):
    near_bias, cmp_bias = _bias_tiles(rel_bias)
    for l in range(ffn1_norm_g.shape[0]):
        x = _layer(x, mem, ffn1_norm_g[l], ffn1_w_in[l], ffn1_w_out[l], mix_norm_g[l], w_in[l],
                   q_norm_g[l], k_norm_g[l], cmp_pe_k[l], cmp_w1_k[l], cmp_w2_k[l],
                   cmp_pe_v[l], cmp_w1_v[l], cmp_w2_v[l], conv_w[l], conv_b[l],
                   mem_norm_g[l], w_mem_kv[l], mem_q_norm_g[l], mem_k_norm_g[l], w_out[l],
                   ffn2_norm_g[l], ffn2_w_in[l], ffn2_w_out[l], near_bias, cmp_bias)
    return x
```

```python
import functools
import math

import jax
import jax.numpy as jnp
from jax import lax
from jax.experimental import pallas as pl
from jax.experimental.pallas import tpu as pltpu

N_HEADS = 16
HEAD_DIM = 64
N_KV_GROUPS = 4
HEADS_PER_GROUP = N_HEADS // N_KV_GROUPS
GROUP_WIDTH = HEADS_PER_GROUP * HEAD_DIM
CMP_BLOCK = 32
CMP_STRIDE = 16
SEL_BLOCK = 64
N_SELECT = 16
WINDOW = 512
FORCE = 1e4
CONV_WIDTH = 3
MEM_HEADS = 4
REL_BUCKETS = 32
REL_MAX_DIST = 128
EPS = 1e-6
NEG = -1e30

T = 256
BLK_PER_T = T // SEL_BLOCK
CMP_PER_T = T // CMP_STRIDE
CMP_NEAR = 2 * CMP_PER_T
N_NEAR = WINDOW // T + 1
N_BUF = 4
FFN_TM = 512
ROW_TM = 256
MEM_TM = 512
MERGE_TM = 512
HALO = 16
LANES = 128
BF16_SUBLANES = 16
V_ROWS = HEAD_DIM + BF16_SUBLANES
LOG2E = math.log2(math.e)
VMEM_LIMIT = 52 * 1024 * 1024

F32 = jnp.float32
BF16 = jnp.bfloat16
HI = lax.Precision.HIGHEST


def _dot(a, b):
    return jnp.dot(a, b, preferred_element_type=F32)


def _dot_nt(a, b):
    return lax.dot_general(a, b, (((1,), (1,)), ((), ())), preferred_element_type=F32)


def _rms_rows(xf, g):
    return xf * lax.rsqrt(jnp.mean(xf * xf, axis=-1, keepdims=True) + EPS) * g


def _sigmoid(x):
    return 1.0 / (1.0 + jnp.exp(-x))


def _resident(shape):
    zeros = (0,) * len(shape)
    return pl.BlockSpec(shape, lambda *_: zeros, pipeline_mode=pl.Buffered(1))


def _params(n_axes):
    return pltpu.CompilerParams(dimension_semantics=("arbitrary",) * n_axes,
                                vmem_limit_bytes=VMEM_LIMIT)


def _ffn_kernel(x_ref, g_ref, wa_ref, wb_ref, wo_ref, o_ref, *, ff_chunk):
    x = x_ref[...]
    h = _rms_rows(x, g_ref[...]).astype(BF16)
    d_ff = wa_ref.shape[1]
    acc = jnp.zeros(x.shape, F32)
    for lo in range(0, d_ff, ff_chunk):
        a = _dot(h, wa_ref[:, lo:lo + ff_chunk])
        b = _dot(h, wb_ref[:, lo:lo + ff_chunk])
        z = (a * _sigmoid(a) * b).astype(BF16)
        acc = acc + _dot(z, wo_ref[lo:lo + ff_chunk, :])
    o_ref[...] = x + 0.5 * acc


def _ffn(x2d, g, w_in, w_out):
    n, d = x2d.shape
    d_ff = w_out.shape[0]
    wa = w_in[:, :d_ff].astype(BF16)
    wb = w_in[:, d_ff:].astype(BF16)
    wo = w_out.astype(BF16)
    ff_chunk = d_ff // 2 if (d_ff // 2) % 128 == 0 else d_ff
    return pl.pallas_call(
        functools.partial(_ffn_kernel, ff_chunk=ff_chunk),
        grid=(n // FFN_TM,),
        in_specs=[pl.BlockSpec((FFN_TM, d), lambda i: (i, 0)),
                  _resident((1, d)), _resident((d, d_ff)), _resident((d, d_ff)), _resident((d_ff, d))],
        out_specs=pl.BlockSpec((FFN_TM, d), lambda i: (i, 0)),
        out_shape=jax.ShapeDtypeStruct((n, d), F32),
        compiler_params=_params(1),
        name="ffn",
    )(x2d, g.reshape(1, d), wa, wb, wo)


def _proj_rows_kernel(x_ref, g_ref, w_ref, bd_ref, kg_ref,
                      ks_o, kw_o, kc_o, vc_o, conv_o, qm_o, mg_o, *, widths):
    h = _rms_rows(x_ref[0], g_ref[...]).astype(BF16)

    def knorm(k):
        ms = jnp.dot(k * k, bd_ref[...], precision=HI, preferred_element_type=F32)
        return (k * lax.rsqrt(ms + EPS) * kg_ref[...]).astype(BF16)

    lo = 0
    outs = (ks_o, kw_o, kc_o, vc_o, conv_o, qm_o, mg_o)
    for idx, (o_ref, wd) in enumerate(zip(outs, widths)):
        y = _dot(h, w_ref[:, lo:lo + wd])
        o_ref[0] = knorm(y) if idx < 2 else y.astype(o_ref.dtype)
        lo += wd


def _proj_t_kernel(x_ref, g_ref, wt_ref, qg_ref, q_o, vs_o, vw_o, gt_o, *, d_q, d_kv):
    h = _rms_rows(x_ref[0], g_ref[...]).astype(BF16)
    qg = qg_ref[...] * (HEAD_DIM ** -0.5 * LOG2E)
    out_t = _dot_nt(wt_ref[...], h)
    for hd in range(d_q // HEAD_DIM):
        q = out_t[hd * HEAD_DIM:(hd + 1) * HEAD_DIM, :]
        qn = q * lax.rsqrt(jnp.mean(q * q, axis=0, keepdims=True) + EPS) * qg
        q_o[0, 0, hd * HEAD_DIM:(hd + 1) * HEAD_DIM, :] = qn.astype(BF16)
    lo = d_q
    for v_o in (vs_o, vw_o):
        y = out_t[lo:lo + d_kv, :].astype(BF16)
        for grp in range(d_kv // HEAD_DIM):
            v_o[0, 0, grp * V_ROWS:grp * V_ROWS + HEAD_DIM, :] = y[grp * HEAD_DIM:(grp + 1) * HEAD_DIM]
            v_o[0, 0, grp * V_ROWS + HEAD_DIM:(grp + 1) * V_ROWS, :] = jnp.ones((V_ROWS - HEAD_DIM, T), BF16)
        lo += d_kv
    gt_o[0, 0] = _sigmoid(out_t[lo:, :])


def _compress_kernel(c_ref, pe_ref, w1_ref, w2_ref, kg_ref, o_ref, *, is_key):
    c = c_ref[0, 0]
    n_chunks, half = c.shape
    a = _dot((c + pe_ref[:, :half]).astype(BF16), w1_ref[:half, :])
    b = _dot((c + pe_ref[:, half:]).astype(BF16), w1_ref[half:, :])
    hid = a + pltpu.roll(b, n_chunks - 1, 0)
    hid = (hid * _sigmoid(hid)).astype(BF16)
    if is_key:
        y = _dot(hid, w2_ref[...])
        y = _rms_rows(y, kg_ref[...])
        row = lax.broadcasted_iota(jnp.int32, y.shape, 0)
        o_ref[0, 0] = jnp.where(row < n_chunks - 1, y, 0.0).astype(BF16)
    else:
        y = _dot_nt(w2_ref[...], hid)
        col = lax.broadcasted_iota(jnp.int32, y.shape, 1)
        o_ref[0, 0] = jnp.where(col < n_chunks - 1, y, 0.0).astype(BF16)


def _compress(c4, pe, w1, w2, k_gain, is_key):
    b, g, n_chunks, width = c4.shape
    hidden = w1.shape[1]
    w2_arg = w2.astype(BF16) if is_key else w2.T.astype(BF16)
    out_block = (1, 1, n_chunks, HEAD_DIM) if is_key else (1, 1, HEAD_DIM, n_chunks)
    return pl.pallas_call(
        functools.partial(_compress_kernel, is_key=is_key),
        grid=(b, g),
        in_specs=[pl.BlockSpec((1, 1, n_chunks, width), lambda i, j: (i, j, 0, 0)),
                  _resident((1, 2 * width)), _resident((2 * width, hidden)),
                  _resident(w2_arg.shape), _resident((1, HEAD_DIM))],
        out_specs=pl.BlockSpec(out_block, lambda i, j: (i, j, 0, 0)),
        out_shape=jax.ShapeDtypeStruct((b, g) + out_block[2:], BF16),
        compiler_params=_params(2),
        name="compress_k" if is_key else "compress_v",
    )(c4, pe.reshape(1, 2 * width), w1.astype(BF16), w2_arg, k_gain.reshape(1, HEAD_DIM))


def _bias_kernel(rb_ref, bkt_near_ref, bkt_cmp_ref, near_o, cmp_o):
    h = pl.program_id(0)
    far = rb_ref[REL_BUCKETS - 1, h]

    def lookup(bkt):
        out = jnp.zeros(bkt.shape, F32)
        for k in range(REL_BUCKETS - 1):
            out = jnp.where(bkt == k, (rb_ref[k, h] - far) * LOG2E, out)
        return out

    key = lax.broadcasted_iota(jnp.int32, (T, T), 0)
    qry = lax.broadcasted_iota(jnp.int32, (T, T), 1)
    near_o[0, 0] = jnp.where(qry >= key, lookup(bkt_near_ref[0]), NEG)
    near_o[0, 1] = lookup(bkt_near_ref[1])
    near_o[0, 2] = jnp.where(key > qry, 0.0, NEG)
    cmp_o[0] = lookup(bkt_cmp_ref[...])


def _rel_bucket(dist):
    n = jnp.maximum(dist, 0)
    max_exact = REL_BUCKETS // 2
    nf = jnp.maximum(n, 1).astype(F32)
    large = max_exact + (jnp.log(nf / max_exact) / math.log(REL_MAX_DIST / max_exact)
                         * (REL_BUCKETS - max_exact)).astype(jnp.int32)
    large = jnp.minimum(large, REL_BUCKETS - 1)
    return jnp.where(n < max_exact, n, large)


def _bias_tiles(rel_bias):
    key = jnp.arange(T)[:, None]
    qry = jnp.arange(T)[None, :]
    bkt_near = jnp.stack([_rel_bucket(qry - key), _rel_bucket(qry - key + T)]).astype(jnp.int32)
    j = jnp.arange(CMP_NEAR)[:, None]
    bkt_cmp = _rel_bucket(qry - CMP_STRIDE * (j - CMP_PER_T) - (CMP_BLOCK - 1)).astype(jnp.int32)
    return pl.pallas_call(
        _bias_kernel,
        grid=(N_HEADS,),
        in_specs=[pl.BlockSpec(memory_space=pltpu.SMEM),
                  pl.BlockSpec((2, T, T), lambda h: (0, 0, 0)),
                  pl.BlockSpec((CMP_NEAR, T), lambda h: (0, 0))],
        out_specs=[pl.BlockSpec((1, N_NEAR, T, T), lambda h: (h // HEADS_PER_GROUP, 0, 0, h % HEADS_PER_GROUP)),
                   pl.BlockSpec((1, CMP_NEAR, T), lambda h: (h // HEADS_PER_GROUP, 0, h % HEADS_PER_GROUP))],
        out_shape=[jax.ShapeDtypeStruct((N_KV_GROUPS, N_NEAR, T, HEADS_PER_GROUP * T), F32),
                   jax.ShapeDtypeStruct((N_KV_GROUPS, CMP_NEAR, HEADS_PER_GROUP * T), F32)],
        compiler_params=_params(1),
        name="bias_tiles",
    )(rel_bias, bkt_near, bkt_cmp)


def _nsa_kernel(q_ref, ks_ref, vs_ref, kw_ref, vw_ref, kc_ref, vc_ref, gt_ref, nb_ref, cb_ref,
                o_ref,
                qcat_ref, sc_ref, psum_ref, selneg_ref, oc_ref,
                s0_ref, s1_ref, s2_ref, s3_ref, p0_ref, p1_ref, p2_ref, p3_ref,
                ms_ref, accs_ref, mw_ref, accw_ref, *, n_blk, n_cmp):
    g = pl.program_id(1)
    qi = pl.program_id(2)
    R = HEADS_PER_GROUP
    W = R * T
    qry = lax.broadcasted_iota(jnp.int32, (1, W), 1) & (T - 1)
    t = qi * T + qry

    def q_head(r):
        return q_ref[0, 0, r * HEAD_DIM:(r + 1) * HEAD_DIM, :]

    rowgrp = lax.shift_right_logical(lax.broadcasted_iota(jnp.int32, (GROUP_WIDTH, T), 0),
                                     int(math.log2(HEAD_DIM)))
    for r in range(R):
        q4 = jnp.concatenate([q_head(r).astype(F32)] * N_KV_GROUPS, axis=0)
        qcat_ref[:, r * T:(r + 1) * T] = jnp.where(rowgrp == g, q4, 0.0).astype(BF16)

    for m_ref, acc_ref in ((ms_ref, accs_ref), (mw_ref, accw_ref)):
        m_ref[...] = jnp.full(m_ref.shape, NEG, F32)
        acc_ref[...] = jnp.zeros(acc_ref.shape, F32)

    def softmax_tile(s_ref, p_ref, m_ref, rows):
        top = None
        for j in range(BLK_PER_T):
            blk_max = s_ref[j * SEL_BLOCK:(j + 1) * SEL_BLOCK, :].reshape(SEL_BLOCK // 8, 8, W).max(axis=0)
            if rows is not None:
                blk_max = blk_max + rows[j]
            top = blk_max if top is None else jnp.maximum(top, blk_max)
        m_old = m_ref[...]
        m_new = jnp.maximum(m_old, jnp.max(top, axis=0, keepdims=True))
        for j in range(BLK_PER_T):
            shift = m_new if rows is None else m_new - rows[j]
            sl = slice(j * SEL_BLOCK, (j + 1) * SEL_BLOCK)
            p_ref[sl, :] = jnp.exp2(s_ref[sl, :] - shift).astype(BF16)
        m_ref[...] = m_new
        return jnp.exp2(m_old - m_new)

    s_bufs = (s0_ref, s1_ref, s2_ref, s3_ref)
    p_bufs = (p0_ref, p1_ref, p2_ref, p3_ref)
    neg_row = jnp.full((1, W), NEG, F32)

    for back in range(WINDOW // T + 1):
        kj = jnp.maximum(qi - back, 0)
        s_bufs[back][...] = _dot(kw_ref[0, kj], qcat_ref[...]) + nb_ref[0, back]
        rows = None if back == 0 else [jnp.where(qi >= back, 0.0, neg_row)] * BLK_PER_T
        alpha = softmax_tile(s_bufs[back], p_bufs[back], mw_ref, rows)
        accw_ref[...] = alpha * accw_ref[...] + _dot(vw_ref[0, kj, 0], p_bufs[back][...])

    c_idx = lax.broadcasted_iota(jnp.int32, (n_cmp, W), 0)
    valid_c = (c_idx * CMP_STRIDE + (CMP_BLOCK - 1) <= t) & (c_idx < n_cmp - 1)
    near0 = pl.multiple_of(qi * CMP_PER_T, CMP_PER_T)
    sc_ref[0:CMP_PER_T, :] = jnp.zeros((CMP_PER_T, W), F32)
    sc_ref[CMP_PER_T:, :] = _dot(kc_ref[0, 0], jnp.concatenate([q_head(r) for r in range(R)], axis=1))
    sc_ref[pl.ds(near0, CMP_NEAR), :] = sc_ref[pl.ds(near0, CMP_NEAR), :] + cb_ref[0]
    s = jnp.where(valid_c, sc_ref[CMP_PER_T:, :], NEG)
    m = jnp.max(s, axis=0, keepdims=True)
    p = jnp.where(valid_c, jnp.exp2(s - m), 0.0)
    l = jnp.sum(p, axis=0, keepdims=True)
    p = p * jnp.where(l > 0.0, 1.0 / jnp.where(l > 0.0, l, 1.0), 0.0)
    oc_ref[...] = _dot(vc_ref[0, 0], p.astype(BF16))
    psum = p[:, 0:T]
    for r in range(1, R):
        psum = psum + p[:, r * T:(r + 1) * T]

    for ln in range(T // LANES):
        psum_ref[ln, 0:8, :] = jnp.zeros((8, LANES), F32)
        psum_ref[ln, 8:, :] = psum[:, ln * LANES:(ln + 1) * LANES]

    def every_fourth(off):
        return jnp.concatenate([psum_ref[ln, pl.ds(8 + off, n_blk, stride=SEL_BLOCK // CMP_STRIDE), :]
                                for ln in range(T // LANES)], axis=1)

    imp = every_fourth(0) + every_fourth(1) + every_fourth(2) + 0.5 * (every_fourth(3) + every_fourth(-1))
    blk = lax.broadcasted_iota(jnp.int32, (n_blk, T), 0)
    cur = lax.shift_right_logical(t[:, 0:T], int(math.log2(SEL_BLOCK)))
    forced = (blk == 0) | (blk == cur) | (blk == cur - 1)
    score = jnp.where(blk <= cur, imp + jnp.where(forced, FORCE, 0.0), -FORCE)
    for _ in range(min(N_SELECT, n_blk)):
        best = jnp.max(score, axis=0, keepdims=True)
        first = jnp.min(jnp.where(score == best, blk, n_blk), axis=0, keepdims=True)
        score = jnp.where(blk == first, -jnp.inf, score)
    selneg = jnp.where(score == -jnp.inf, 0.0, NEG)
    selneg_ref[...] = jnp.concatenate([selneg] * R, axis=1)

    def tile_of(y):
        return jnp.where(y == 0, qi, jnp.where(y == 1, jnp.maximum(qi - 1, 0), jnp.clip(y - 2, 0, qi)))

    def sel_rows(y):
        kj = tile_of(y)
        return [jnp.where(y <= qi, selneg_ref[pl.ds(kj * BLK_PER_T + j, 1), :], neg_row)
                for j in range(BLK_PER_T)]

    s0_ref[...] = _dot(ks_ref[0, tile_of(0)], qcat_ref[...]) + nb_ref[0, 0]
    s1_ref[...] = _dot(ks_ref[0, tile_of(1)], qcat_ref[...]) + nb_ref[0, 1]
    p_bufs[N_BUF - 1][...] = jnp.zeros(p0_ref.shape, BF16)

    def trip(i, carry):
        for k in range(N_BUF):
            y = N_BUF * i + k
            pv = _dot(vs_ref[0, tile_of(y - 1), 0], p_bufs[(k - 1) % N_BUF][...])
            s_bufs[(k + 2) % N_BUF][...] = _dot(ks_ref[0, tile_of(y + 2)], qcat_ref[...])
            alpha = softmax_tile(s_bufs[k], p_bufs[k], ms_ref, sel_rows(y))
            accs_ref[...] = alpha * (accs_ref[...] + pv)
        return carry

    n_trips = lax.div(qi + N_BUF, N_BUF)
    lax.fori_loop(0, n_trips, trip, 0)
    accs_ref[...] = accs_ref[...] + _dot(vs_ref[0, tile_of(N_BUF * n_trips - 1), 0], p_bufs[N_BUF - 1][...])

    o_s = accs_ref[0:HEAD_DIM, :] / accs_ref[HEAD_DIM:HEAD_DIM + 1, :]
    o_w = accw_ref[0:HEAD_DIM, :] / accw_ref[HEAD_DIM:HEAD_DIM + 1, :]
    outs = []
    for r in range(R):
        head = g * R + r
        sl = slice(r * T, (r + 1) * T)
        g_c = gt_ref[0, 0, pl.ds(head, 1), :]
        g_s = gt_ref[0, 0, pl.ds(N_HEADS + head, 1), :]
        g_w = gt_ref[0, 0, pl.ds(2 * N_HEADS + head, 1), :]
        outs.append(g_c * oc_ref[:, sl] + g_s * o_s[:, sl] + g_w * o_w[:, sl])
    o_ref[0] = jnp.concatenate(outs, axis=0).T.astype(o_ref.dtype)


def _nsa(q_t, ks, vs_t, kw, vw_t, kc, vc_t, gates_t, near_bias, cmp_bias, seq):
    b, nq = q_t.shape[0], q_t.shape[1]
    n_blk = seq // SEL_BLOCK
    n_cmp = seq // CMP_STRIDE
    W = HEADS_PER_GROUP * T
    kv_spec = pl.BlockSpec((1, nq, T, GROUP_WIDTH), lambda i, j, k: (i, 0, 0, 0))
    vt_spec = pl.BlockSpec((1, nq, 1, V_ROWS, T), lambda i, j, k: (i, 0, j, 0, 0))
    stat = pltpu.VMEM((1, W), F32)
    acc = pltpu.VMEM((V_ROWS, W), F32)
    scores = pltpu.VMEM((T, W), F32)
    probs = pltpu.VMEM((T, W), BF16)
    return pl.pallas_call(
        functools.partial(_nsa_kernel, n_blk=n_blk, n_cmp=n_cmp),
        grid=(b, N_KV_GROUPS, nq),
        in_specs=[pl.BlockSpec((1, 1, GROUP_WIDTH, T), lambda i, j, k: (i, k, j, 0)),
                  kv_spec, vt_spec, kv_spec, vt_spec,
                  pl.BlockSpec((1, 1, n_cmp, HEAD_DIM), lambda i, j, k: (i, j, 0, 0)),
                  pl.BlockSpec((1, 1, HEAD_DIM, n_cmp), lambda i, j, k: (i, j, 0, 0)),
                  pl.BlockSpec((1, 1, gates_t.shape[2], T), lambda i, j, k: (i, k, 0, 0)),
                  pl.BlockSpec((1, N_NEAR, T, W), lambda i, j, k: (j, 0, 0, 0)),
                  pl.BlockSpec((1, CMP_NEAR, W), lambda i, j, k: (j, 0, 0))],
        out_specs=pl.BlockSpec((1, T, GROUP_WIDTH), lambda i, j, k: (i, k, j)),
        out_shape=jax.ShapeDtypeStruct((b, seq, N_HEADS * HEAD_DIM), BF16),
        scratch_shapes=[pltpu.VMEM((GROUP_WIDTH, W), BF16),
                        pltpu.VMEM((n_cmp + CMP_PER_T, W), F32),
                        pltpu.VMEM((T // LANES, n_cmp + 8, LANES), F32),
                        pltpu.VMEM((n_blk, W), F32),
                        pltpu.VMEM((HEAD_DIM, W), F32), *([scores] * N_BUF), *([probs] * N_BUF),
                        stat, acc, stat, acc],
        compiler_params=_params(3),
        name="nsa",
    )(q_t, ks, vs_t, kw, vw_t, kc, vc_t, gates_t, near_bias, cmp_bias)


def _mem_kv_kernel(mem_ref, g_ref, w_ref, kg_ref, k_o, v_o):
    h = _rms_rows(mem_ref[0], g_ref[...]).astype(BF16)
    width = k_o.shape[2]
    hd = width // MEM_HEADS
    k = _dot(h, w_ref[:, :width])
    for i in range(MEM_HEADS):
        k_o[0, :, i * hd:(i + 1) * hd] = _rms_rows(k[:, i * hd:(i + 1) * hd], kg_ref[...]).astype(BF16)
    v_o[0] = _dot(h, w_ref[:, width:]).astype(BF16)


def _mem_attn_kernel(q_ref, k_ref, v_ref, qg_ref, o_ref):
    width = q_ref.shape[2]
    hd = width // MEM_HEADS
    for i in range(MEM_HEADS):
        sl = slice(i * hd, (i + 1) * hd)
        q = (_rms_rows(q_ref[0, :, sl].astype(F32), qg_ref[...]) * (hd ** -0.5)).astype(BF16)
        s = _dot_nt(q, k_ref[0, :, sl])
        p = jnp.exp(s - jnp.max(s, axis=-1, keepdims=True))
        p = p / jnp.sum(p, axis=-1, keepdims=True)
        o_ref[0, :, sl] = _dot(p.astype(BF16), v_ref[0, :, sl]).astype(o_ref.dtype)


def _mem_attention(q_mem, mem, mem_norm_g, w_mem_kv, q_g, k_g):
    b, s, width = q_mem.shape
    n_mem, d = mem.shape[1], mem.shape[2]
    hd = width // MEM_HEADS
    kv_shape = jax.ShapeDtypeStruct((b, n_mem, width), BF16)
    kv_block = pl.BlockSpec((1, n_mem, width), lambda i: (i, 0, 0))
    km, vm = pl.pallas_call(
        _mem_kv_kernel,
        grid=(b,),
        in_specs=[pl.BlockSpec((1, n_mem, d), lambda i: (i, 0, 0)),
                  _resident((1, d)), _resident((d, 2 * width)), _resident((1, hd))],
        out_specs=[kv_block, kv_block],
        out_shape=[kv_shape, kv_shape],
        compiler_params=_params(1),
        name="mem_kv",
    )(mem, mem_norm_g.reshape(1, d), w_mem_kv.astype(BF16), k_g.reshape(1, hd))
    kv_block2 = pl.BlockSpec((1, n_mem, width), lambda i, j: (i, 0, 0))
    return pl.pallas_call(
        _mem_attn_kernel,
        grid=(b, s // MEM_TM),
        in_specs=[pl.BlockSpec((1, MEM_TM, width), lambda i, j: (i, j, 0)),
                  kv_block2, kv_block2, _resident((1, hd))],
        out_specs=pl.BlockSpec((1, MEM_TM, width), lambda i, j: (i, j, 0)),
        out_shape=jax.ShapeDtypeStruct((b, s, width), BF16),
        compiler_params=_params(2),
        name="mem_attn",
    )(q_mem, km, vm, q_g.reshape(1, hd))


def _merge_kernel(x_ref, nsa_ref, mem_ref, cb_ref, cc_ref, cx_ref, hc_ref, hx_ref,
                  g1_ref, g2_ref, g3_ref, cw_ref, bias_ref, wo_ref, o_ref):
    j = pl.program_id(1)

    def f32(ref):
        return ref[0].astype(F32)

    u = f32(cc_ref) * f32(cx_ref)
    halo = jnp.where(j > 0, f32(hc_ref) * f32(hx_ref), 0.0)
    prev1 = halo[HALO - 1:HALO, :]
    prev2 = halo[HALO - 2:HALO - 1, :]
    row = lax.broadcasted_iota(jnp.int32, u.shape, 0)
    u1 = jnp.where(row == 0, prev1, pltpu.roll(u, 1, 0))
    u2 = jnp.where(row == 0, prev2, jnp.where(row == 1, prev1, pltpu.roll(u, 2, 0)))
    y = cw_ref[0:1, :] * u2 + cw_ref[1:2, :] * u1 + cw_ref[2:3, :] * u
    o_conv = f32(cb_ref) * (y + bias_ref[...])
    merged = (_sigmoid(f32(g1_ref)) * f32(nsa_ref) + _sigmoid(f32(g2_ref)) * o_conv
              + _sigmoid(f32(g3_ref)) * f32(mem_ref))
    o_ref[0] = x_ref[0] + _dot(merged.astype(BF16), wo_ref[...])


def _merge(x, o_nsa, o_mem, conv_in, merge_g, conv_w, conv_b, w_out):
    b, s, d = x.shape
    tm = MERGE_TM

    def col(c):
        return pl.BlockSpec((1, tm, d), lambda i, j: (i, j, c))

    def halo(c):
        return pl.BlockSpec((1, HALO, d), lambda i, j: (i, jnp.maximum(j * (tm // HALO) - 1, 0), c))

    return pl.pallas_call(
        _merge_kernel,
        grid=(b, s // tm),
        in_specs=[col(0), col(0), col(0),
                  col(0), col(1), col(2), halo(1), halo(2),
                  col(0), col(1), col(2),
                  _resident((CONV_WIDTH, d)), _resident((1, d)), _resident((d, d))],
        out_specs=col(0),
        out_shape=jax.ShapeDtypeStruct((b, s, d), F32),
        compiler_params=_params(2),
        name="merge",
    )(x, o_nsa, o_mem, conv_in, conv_in, conv_in, conv_in, conv_in,
      merge_g, merge_g, merge_g, conv_w, conv_b.reshape(1, d), w_out.astype(BF16))


def _layer(x, mem, ffn1_norm_g, ffn1_w_in, ffn1_w_out, mix_norm_g, w_in, q_norm_g, k_norm_g,
           cmp_pe_k, cmp_w1_k, cmp_w2_k, cmp_pe_v, cmp_w1_v, cmp_w2_v, conv_w, conv_b,
           mem_norm_g, w_mem_kv, mem_q_norm_g, mem_k_norm_g, w_out,
           ffn2_norm_g, ffn2_w_in, ffn2_w_out, near_bias, cmp_bias):
    b, s, d = x.shape
    assert s % T == 0 and s % MERGE_TM == 0 and (b * s) % FFN_TM == 0
    assert WINDOW == 2 * T and REL_MAX_DIST <= T // 2
    assert SEL_BLOCK == 4 * CMP_STRIDE and CMP_BLOCK == 2 * CMP_STRIDE
    nq = s // T
    d_q = N_HEADS * HEAD_DIM
    d_kv = N_KV_GROUPS * HEAD_DIM
    d_conv = conv_w.shape[1]
    d_mem = w_mem_kv.shape[1] // 2

    x = _ffn(x.reshape(b * s, d), ffn1_norm_g, ffn1_w_in, ffn1_w_out).reshape(b, s, d)

    o = 0
    w_q = w_in[:, o:o + d_q]; o += d_q
    w_kc, w_vc, w_ks, w_vs, w_kw, w_vw = [w_in[:, o + i * d_kv:o + (i + 1) * d_kv] for i in range(6)]
    o += 6 * d_kv
    w_g = w_in[:, o:o + 3 * N_HEADS]; o += 3 * N_HEADS
    w_conv = w_in[:, o:o + 3 * d_conv]; o += 3 * d_conv
    w_qm = w_in[:, o:o + d_mem]; o += d_mem
    w_mg = w_in[:, o:]

    w_rows = jnp.concatenate([w_ks, w_kw, w_kc, w_vc, w_conv, w_qm, w_mg], axis=1).astype(BF16)
    widths = (d_kv, d_kv, d_kv, d_kv, 3 * d_conv, d_mem, w_mg.shape[1])
    group_of = jnp.arange(d_kv) // HEAD_DIM
    block_diag = (group_of[:, None] == group_of[None, :]).astype(F32) / HEAD_DIM
    k_gain_row = jnp.tile(k_norm_g, N_KV_GROUPS).reshape(1, d_kv)

    def rows_out(wd, dt):
        return (pl.BlockSpec((1, ROW_TM, wd), lambda i, j: (i, j, 0)), jax.ShapeDtypeStruct((b, s, wd), dt))

    specs = [rows_out(wd, BF16) for wd in widths]
    ks, kw, kc, vc, conv_in, q_mem, merge_g = pl.pallas_call(
        functools.partial(_proj_rows_kernel, widths=widths),
        grid=(b, s // ROW_TM),
        in_specs=[pl.BlockSpec((1, ROW_TM, d), lambda i, j: (i, j, 0)),
                  _resident((1, d)), _resident(w_rows.shape), _resident((d_kv, d_kv)), _resident((1, d_kv))],
        out_specs=[sp[0] for sp in specs],
        out_shape=[sp[1] for sp in specs],
        compiler_params=_params(2),
        name="proj_rows",
    )(x, mix_norm_g.reshape(1, d), w_rows, block_diag, k_gain_row)

    w_g_t = w_g.reshape(d, N_HEADS, 3).transpose(2, 1, 0).reshape(3 * N_HEADS, d)
    n_gate_rows = 128
    w_g_t = jnp.pad(w_g_t, ((0, n_gate_rows - 3 * N_HEADS), (0, 0)))
    w_t = jnp.concatenate([w_q.T, w_vs.T, w_vw.T, w_g_t], axis=0).astype(BF16)

    def t_out(rows, dt):
        return (pl.BlockSpec((1, 1, rows, T), lambda i, j: (i, j, 0, 0)),
                jax.ShapeDtypeStruct((b, nq, rows, T), dt))

    v_rows = N_KV_GROUPS * V_ROWS
    t_specs = [t_out(d_q, BF16), t_out(v_rows, BF16), t_out(v_rows, BF16), t_out(n_gate_rows, F32)]
    q_t, vs_t, vw_t, gates_t = pl.pallas_call(
        functools.partial(_proj_t_kernel, d_q=d_q, d_kv=d_kv),
        grid=(b, nq),
        in_specs=[pl.BlockSpec((1, T, d), lambda i, j: (i, j, 0)),
                  _resident((1, d)), _resident(w_t.shape), _resident((HEAD_DIM, 1))],
        out_specs=[sp[0] for sp in t_specs],
        out_shape=[sp[1] for sp in t_specs],
        compiler_params=_params(2),
        name="proj_t",
    )(x, mix_norm_g.reshape(1, d), w_t, q_norm_g.reshape(HEAD_DIM, 1))

    n_chunks = s // CMP_STRIDE

    def chunked(a):
        return a.reshape(b, n_chunks, CMP_STRIDE, N_KV_GROUPS, HEAD_DIM).transpose(0, 3, 1, 2, 4) \
                .reshape(b, N_KV_GROUPS, n_chunks, CMP_STRIDE * HEAD_DIM)

    k_cmp = _compress(chunked(kc), cmp_pe_k, cmp_w1_k, cmp_w2_k, k_norm_g, True)
    v_cmp_t = _compress(chunked(vc), cmp_pe_v, cmp_w1_v, cmp_w2_v, k_norm_g, False)

    o_nsa = _nsa(q_t, ks.reshape(b, nq, T, d_kv), vs_t.reshape(b, nq, N_KV_GROUPS, V_ROWS, T),
                 kw.reshape(b, nq, T, d_kv), vw_t.reshape(b, nq, N_KV_GROUPS, V_ROWS, T),
                 k_cmp, v_cmp_t, gates_t, near_bias, cmp_bias, s)

    o_mem = _mem_attention(q_mem, mem, mem_norm_g, w_mem_kv, mem_q_norm_g, mem_k_norm_g)
    x = _merge(x, o_nsa, o_mem, conv_in, merge_g, conv_w, conv_b, w_out)
    x = _ffn(x.reshape(b * s, d), ffn2_norm_g, ffn2_w_in, ffn2_w_out).reshape(b, s, d)
    return x


def kernel(x, mem, ffn1_norm_g, ffn1_w_in, ffn1_w_out, mix_norm_g, w_in, q_norm_g, k_norm_g, cmp_pe_k, cmp_w1_k, cmp_w2_k, cmp_pe_v, cmp_w1_v, cmp_w2_v, conv_w, conv_b, mem_norm_g, w_mem_kv, mem_q_norm_g, mem_k_norm_g, w_out, ffn2_norm_g, ffn2_w_in, ffn2_w_out, rel_bias):
    near_bias, cmp_bias = _bias_tiles(rel_bias)
    for l in range(ffn1_norm_g.shape[0]):
        x = _layer(x, mem, ffn1_norm_g[l], ffn1_w_in[l], ffn1_w_out[l], mix_norm_g[l], w_in[l],
                   q_norm_g[l], k_norm_g[l], cmp_pe_k[l], cmp_w1_k[l], cmp_w2_k[l],
                   cmp_pe_v[l], cmp_w1_v[l], cmp_w2_v[l], conv_w[l], conv_b[l],
                   mem_norm_g[l], w_mem_kv[l], mem_q_norm_g[l], mem_k_norm_g[l], w_out[l],
                   ffn2_norm_g[l], ffn2_w_in[l], ffn2_w_out[l], near_bias, cmp_bias)
    return x
```

```python
import functools
import math

import jax
import jax.numpy as jnp
from jax import lax
from jax.experimental import pallas as pl
from jax.experimental.pallas import tpu as pltpu

N_HEADS = 16
HEAD_DIM = 64
N_KV_GROUPS = 4
HEADS_PER_GROUP = N_HEADS // N_KV_GROUPS
GROUP_WIDTH = HEADS_PER_GROUP * HEAD_DIM
CMP_BLOCK = 32
CMP_STRIDE = 16
SEL_BLOCK = 64
N_SELECT = 16
WINDOW = 512
FORCE = 1e4
CONV_WIDTH = 3
MEM_HEADS = 4
REL_BUCKETS = 32
REL_MAX_DIST = 128
EPS = 1e-6
NEG = -1e30

T = 256
BLK_PER_T = T // SEL_BLOCK
CMP_PER_T = T // CMP_STRIDE
CMP_NEAR = 2 * CMP_PER_T
N_NEAR = WINDOW // T + 1
N_BUF = 4
LONG_TRIP = 2 * N_BUF
FFN_TM = 512
ROW_TM = 256
MEM_TM = 512
MERGE_TM = 512
HALO = 16
LANES = 128
BF16_SUBLANES = 16
V_ROWS = HEAD_DIM + BF16_SUBLANES
LOG2E = math.log2(math.e)
VMEM_LIMIT = 52 * 1024 * 1024

F32 = jnp.float32
BF16 = jnp.bfloat16
HI = lax.Precision.HIGHEST


def _dot(a, b):
    return jnp.dot(a, b, preferred_element_type=F32)


def _dot_nt(a, b):
    return lax.dot_general(a, b, (((1,), (1,)), ((), ())), preferred_element_type=F32)


def _rms_rows(xf, g):
    return xf * lax.rsqrt(jnp.mean(xf * xf, axis=-1, keepdims=True) + EPS) * g


def _sigmoid(x):
    return 1.0 / (1.0 + jnp.exp(-x))


def _resident(shape):
    zeros = (0,) * len(shape)
    return pl.BlockSpec(shape, lambda *_: zeros, pipeline_mode=pl.Buffered(1))


def _params(n_axes):
    return pltpu.CompilerParams(dimension_semantics=("arbitrary",) * n_axes,
                                vmem_limit_bytes=VMEM_LIMIT)


def _ffn_kernel(x_ref, g_ref, wa_ref, wb_ref, wo_ref, o_ref, *, ff_chunk):
    x = x_ref[...]
    h = _rms_rows(x, g_ref[...]).astype(BF16)
    d_ff = wa_ref.shape[1]
    acc = jnp.zeros(x.shape, F32)
    for lo in range(0, d_ff, ff_chunk):
        a = _dot(h, wa_ref[:, lo:lo + ff_chunk])
        b = _dot(h, wb_ref[:, lo:lo + ff_chunk])
        z = (a * _sigmoid(a) * b).astype(BF16)
        acc = acc + _dot(z, wo_ref[lo:lo + ff_chunk, :])
    o_ref[...] = x + 0.5 * acc


def _ffn(x2d, g, w_in, w_out):
    n, d = x2d.shape
    d_ff = w_out.shape[0]
    wa = w_in[:, :d_ff].astype(BF16)
    wb = w_in[:, d_ff:].astype(BF16)
    wo = w_out.astype(BF16)
    ff_chunk = d_ff // 2 if (d_ff // 2) % 128 == 0 else d_ff
    return pl.pallas_call(
        functools.partial(_ffn_kernel, ff_chunk=ff_chunk),
        grid=(n // FFN_TM,),
        in_specs=[pl.BlockSpec((FFN_TM, d), lambda i: (i, 0)),
                  _resident((1, d)), _resident((d, d_ff)), _resident((d, d_ff)), _resident((d_ff, d))],
        out_specs=pl.BlockSpec((FFN_TM, d), lambda i: (i, 0)),
        out_shape=jax.ShapeDtypeStruct((n, d), F32),
        compiler_params=_params(1),
        name="ffn",
    )(x2d, g.reshape(1, d), wa, wb, wo)


def _proj_rows_kernel(x_ref, g_ref, w_ref, bd_ref, kg_ref,
                      ks_o, kw_o, kc_o, vc_o, conv_o, qm_o, mg_o, *, widths):
    h = _rms_rows(x_ref[0], g_ref[...]).astype(BF16)

    def knorm(k):
        ms = jnp.dot(k * k, bd_ref[...], precision=HI, preferred_element_type=F32)
        return (k * lax.rsqrt(ms + EPS) * kg_ref[...]).astype(BF16)

    lo = 0
    outs = (ks_o, kw_o, kc_o, vc_o, conv_o, qm_o, mg_o)
    for idx, (o_ref, wd) in enumerate(zip(outs, widths)):
        y = _dot(h, w_ref[:, lo:lo + wd])
        o_ref[0] = knorm(y) if idx < 2 else y.astype(o_ref.dtype)
        lo += wd


def _proj_t_kernel(x_ref, g_ref, wt_ref, qg_ref, q_o, vs_o, vw_o, gt_o, *, d_q, d_kv):
    h = _rms_rows(x_ref[0], g_ref[...]).astype(BF16)
    qg = qg_ref[...] * (HEAD_DIM ** -0.5 * LOG2E)
    out_t = _dot_nt(wt_ref[...], h)
    for hd in range(d_q // HEAD_DIM):
        q = out_t[hd * HEAD_DIM:(hd + 1) * HEAD_DIM, :]
        qn = q * lax.rsqrt(jnp.mean(q * q, axis=0, keepdims=True) + EPS) * qg
        q_o[0, 0, hd * HEAD_DIM:(hd + 1) * HEAD_DIM, :] = qn.astype(BF16)
    lo = d_q
    for v_o in (vs_o, vw_o):
        y = out_t[lo:lo + d_kv, :].astype(BF16)
        for grp in range(d_kv // HEAD_DIM):
            v_o[0, 0, grp * V_ROWS:grp * V_ROWS + HEAD_DIM, :] = y[grp * HEAD_DIM:(grp + 1) * HEAD_DIM]
            v_o[0, 0, grp * V_ROWS + HEAD_DIM:(grp + 1) * V_ROWS, :] = jnp.ones((V_ROWS - HEAD_DIM, T), BF16)
        lo += d_kv
    gt_o[0, 0] = _sigmoid(out_t[lo:, :])


def _compress_kernel(c_ref, pe_ref, w1_ref, w2_ref, kg_ref, o_ref, *, is_key):
    c = c_ref[0, 0]
    n_chunks, half = c.shape
    a = _dot((c + pe_ref[:, :half]).astype(BF16), w1_ref[:half, :])
    b = _dot((c + pe_ref[:, half:]).astype(BF16), w1_ref[half:, :])
    hid = a + pltpu.roll(b, n_chunks - 1, 0)
    hid = (hid * _sigmoid(hid)).astype(BF16)
    if is_key:
        y = _dot(hid, w2_ref[...])
        y = _rms_rows(y, kg_ref[...])
        row = lax.broadcasted_iota(jnp.int32, y.shape, 0)
        o_ref[0, 0] = jnp.where(row < n_chunks - 1, y, 0.0).astype(BF16)
    else:
        y = _dot_nt(w2_ref[...], hid)
        col = lax.broadcasted_iota(jnp.int32, y.shape, 1)
        o_ref[0, 0] = jnp.where(col < n_chunks - 1, y, 0.0).astype(BF16)


def _compress(c4, pe, w1, w2, k_gain, is_key):
    b, g, n_chunks, width = c4.shape
    hidden = w1.shape[1]
    w2_arg = w2.astype(BF16) if is_key else w2.T.astype(BF16)
    out_block = (1, 1, n_chunks, HEAD_DIM) if is_key else (1, 1, HEAD_DIM, n_chunks)
    return pl.pallas_call(
        functools.partial(_compress_kernel, is_key=is_key),
        grid=(b, g),
        in_specs=[pl.BlockSpec((1, 1, n_chunks, width), lambda i, j: (i, j, 0, 0)),
                  _resident((1, 2 * width)), _resident((2 * width, hidden)),
                  _resident(w2_arg.shape), _resident((1, HEAD_DIM))],
        out_specs=pl.BlockSpec(out_block, lambda i, j: (i, j, 0, 0)),
        out_shape=jax.ShapeDtypeStruct((b, g) + out_block[2:], BF16),
        compiler_params=_params(2),
        name="compress_k" if is_key else "compress_v",
    )(c4, pe.reshape(1, 2 * width), w1.astype(BF16), w2_arg, k_gain.reshape(1, HEAD_DIM))


def _bias_kernel(rb_ref, bkt_near_ref, bkt_cmp_ref, near_o, cmp_o):
    h = pl.program_id(0)
    far = rb_ref[REL_BUCKETS - 1, h]

    def lookup(bkt):
        out = jnp.zeros(bkt.shape, F32)
        for k in range(REL_BUCKETS - 1):
            out = jnp.where(bkt == k, (rb_ref[k, h] - far) * LOG2E, out)
        return out

    key = lax.broadcasted_iota(jnp.int32, (T, T), 0)
    qry = lax.broadcasted_iota(jnp.int32, (T, T), 1)
    near_o[0, 0] = jnp.where(qry >= key, lookup(bkt_near_ref[0]), NEG)
    near_o[0, 1] = lookup(bkt_near_ref[1])
    near_o[0, 2] = jnp.where(key > qry, 0.0, NEG)
    cmp_o[0] = lookup(bkt_cmp_ref[...])


def _rel_bucket(dist):
    n = jnp.maximum(dist, 0)
    max_exact = REL_BUCKETS // 2
    nf = jnp.maximum(n, 1).astype(F32)
    large = max_exact + (jnp.log(nf / max_exact) / math.log(REL_MAX_DIST / max_exact)
                         * (REL_BUCKETS - max_exact)).astype(jnp.int32)
    large = jnp.minimum(large, REL_BUCKETS - 1)
    return jnp.where(n < max_exact, n, large)


def _bias_tiles(rel_bias):
    key = jnp.arange(T)[:, None]
    qry = jnp.arange(T)[None, :]
    bkt_near = jnp.stack([_rel_bucket(qry - key), _rel_bucket(qry - key + T)]).astype(jnp.int32)
    j = jnp.arange(CMP_NEAR)[:, None]
    bkt_cmp = _rel_bucket(qry - CMP_STRIDE * (j - CMP_PER_T) - (CMP_BLOCK - 1)).astype(jnp.int32)
    return pl.pallas_call(
        _bias_kernel,
        grid=(N_HEADS,),
        in_specs=[pl.BlockSpec(memory_space=pltpu.SMEM),
                  pl.BlockSpec((2, T, T), lambda h: (0, 0, 0)),
                  pl.BlockSpec((CMP_NEAR, T), lambda h: (0, 0))],
        out_specs=[pl.BlockSpec((1, N_NEAR, T, T), lambda h: (h // HEADS_PER_GROUP, 0, 0, h % HEADS_PER_GROUP)),
                   pl.BlockSpec((1, CMP_NEAR, T), lambda h: (h // HEADS_PER_GROUP, 0, h % HEADS_PER_GROUP))],
        out_shape=[jax.ShapeDtypeStruct((N_KV_GROUPS, N_NEAR, T, HEADS_PER_GROUP * T), F32),
                   jax.ShapeDtypeStruct((N_KV_GROUPS, CMP_NEAR, HEADS_PER_GROUP * T), F32)],
        compiler_params=_params(1),
        name="bias_tiles",
    )(rel_bias, bkt_near, bkt_cmp)


def _nsa_kernel(q_ref, ks_ref, vs_ref, kw_ref, vw_ref, kc_ref, vc_ref, gt_ref, nb_ref, cb_ref,
                o_ref,
                qcat_ref, sc_ref, psum_ref, selneg_ref, oc_ref,
                s0_ref, s1_ref, s2_ref, s3_ref, p0_ref, p1_ref, p2_ref, p3_ref, w0_ref, w1_ref, w2_ref,
                ms_ref, accs_ref, mw_ref, accw_ref, *, n_blk, n_cmp):
    g = pl.program_id(1)
    qi = pl.program_id(2)
    R = HEADS_PER_GROUP
    W = R * T
    qry = lax.broadcasted_iota(jnp.int32, (1, W), 1) & (T - 1)
    t = qi * T + qry

    def q_head(r):
        return q_ref[0, 0, r * HEAD_DIM:(r + 1) * HEAD_DIM, :]

    rowgrp = lax.shift_right_logical(lax.broadcasted_iota(jnp.int32, (GROUP_WIDTH, T), 0),
                                     int(math.log2(HEAD_DIM)))
    for r in range(R):
        q4 = jnp.concatenate([q_head(r).astype(F32)] * N_KV_GROUPS, axis=0)
        qcat_ref[:, r * T:(r + 1) * T] = jnp.where(rowgrp == g, q4, 0.0).astype(BF16)

    for m_ref, acc_ref in ((ms_ref, accs_ref), (mw_ref, accw_ref)):
        m_ref[...] = jnp.full(m_ref.shape, NEG, F32)
        acc_ref[...] = jnp.zeros(acc_ref.shape, F32)

    def softmax_tile(s_ref, p_ref, m_ref, rows):
        top = None
        for j in range(BLK_PER_T):
            blk_max = s_ref[j * SEL_BLOCK:(j + 1) * SEL_BLOCK, :].reshape(SEL_BLOCK // 8, 8, W).max(axis=0)
            if rows is not None:
                blk_max = blk_max + rows[j]
            top = blk_max if top is None else jnp.maximum(top, blk_max)
        m_old = m_ref[...]
        m_new = jnp.maximum(m_old, jnp.max(top, axis=0, keepdims=True))
        for j in range(BLK_PER_T):
            shift = m_new if rows is None else m_new - rows[j]
            sl = slice(j * SEL_BLOCK, (j + 1) * SEL_BLOCK)
            p_ref[sl, :] = jnp.exp2(s_ref[sl, :] - shift).astype(BF16)
        m_ref[...] = m_new
        return jnp.exp2(m_old - m_new)

    s_bufs = (s0_ref, s1_ref, s2_ref, s3_ref)
    p_bufs = (p0_ref, p1_ref, p2_ref, p3_ref)
    w_bufs = (w0_ref, w1_ref, w2_ref)
    neg_row = jnp.full((1, W), NEG, F32)

    def tile_of(y):
        return jnp.where(y == 0, qi, jnp.where(y == 1, jnp.maximum(qi - 1, 0), jnp.clip(y - 2, 0, qi)))

    win_tiles = [jnp.maximum(qi - back, 0) for back in range(N_NEAR)]
    for back in range(N_NEAR):
        w_bufs[back][...] = _dot(kw_ref[0, win_tiles[back]], qcat_ref[...]) + nb_ref[0, back]
    near0 = pl.multiple_of(qi * CMP_PER_T, CMP_PER_T)
    sc_ref[0:CMP_PER_T, :] = jnp.zeros((CMP_PER_T, W), F32)
    sc_ref[CMP_PER_T:, :] = _dot(kc_ref[0, 0], jnp.concatenate([q_head(r) for r in range(R)], axis=1))
    sc_ref[pl.ds(near0, CMP_NEAR), :] = sc_ref[pl.ds(near0, CMP_NEAR), :] + cb_ref[0]
    s0_ref[...] = _dot(ks_ref[0, tile_of(0)], qcat_ref[...]) + nb_ref[0, 0]
    s1_ref[...] = _dot(ks_ref[0, tile_of(1)], qcat_ref[...]) + nb_ref[0, 1]

    for back in range(N_NEAR):
        rows = None if back == 0 else [jnp.where(qi >= back, 0.0, neg_row)] * BLK_PER_T
        alpha = softmax_tile(w_bufs[back], p_bufs[back], mw_ref, rows)
        accw_ref[...] = alpha * accw_ref[...] + _dot(vw_ref[0, win_tiles[back], 0], p_bufs[back][...])

    c_idx = lax.broadcasted_iota(jnp.int32, (n_cmp, W), 0)
    valid_c = (c_idx * CMP_STRIDE + (CMP_BLOCK - 1) <= t) & (c_idx < n_cmp - 1)
    s = jnp.where(valid_c, sc_ref[CMP_PER_T:, :], NEG)
    m = jnp.max(s, axis=0, keepdims=True)
    p = jnp.where(valid_c, jnp.exp2(s - m), 0.0)
    l = jnp.sum(p, axis=0, keepdims=True)
    p = p * jnp.where(l > 0.0, 1.0 / jnp.where(l > 0.0, l, 1.0), 0.0)
    oc_ref[...] = _dot(vc_ref[0, 0], p.astype(BF16))
    psum = p[:, 0:T]
    for r in range(1, R):
        psum = psum + p[:, r * T:(r + 1) * T]

    for ln in range(T // LANES):
        psum_ref[ln, 0:8, :] = jnp.zeros((8, LANES), F32)
        psum_ref[ln, 8:, :] = psum[:, ln * LANES:(ln + 1) * LANES]

    def every_fourth(off):
        return jnp.concatenate([psum_ref[ln, pl.ds(8 + off, n_blk, stride=SEL_BLOCK // CMP_STRIDE), :]
                                for ln in range(T // LANES)], axis=1)

    imp = every_fourth(0) + every_fourth(1) + every_fourth(2) + 0.5 * (every_fourth(3) + every_fourth(-1))
    blk = lax.broadcasted_iota(jnp.int32, (n_blk, T), 0)
    cur = lax.shift_right_logical(t[:, 0:T], int(math.log2(SEL_BLOCK)))
    forced = (blk == 0) | (blk == cur) | (blk == cur - 1)
    score = jnp.where(blk <= cur, imp + jnp.where(forced, FORCE, 0.0), -FORCE)
    for _ in range(min(N_SELECT, n_blk)):
        best = jnp.max(score, axis=0, keepdims=True)
        first = jnp.min(jnp.where(score == best, blk, n_blk), axis=0, keepdims=True)
        score = jnp.where(blk == first, -jnp.inf, score)
    selneg = jnp.where(score == -jnp.inf, 0.0, NEG)
    selneg_ref[...] = jnp.concatenate([selneg] * R, axis=1)

    def sel_rows(y):
        kj = tile_of(y)
        return [jnp.where(y <= qi, selneg_ref[pl.ds(kj * BLK_PER_T + j, 1), :], neg_row)
                for j in range(BLK_PER_T)]

    p_bufs[N_BUF - 1][...] = jnp.zeros(p0_ref.shape, BF16)

    def positions(first, count):
        for k in range(count):
            y = first + k
            pv = _dot(vs_ref[0, tile_of(y - 1), 0], p_bufs[(k - 1) % N_BUF][...])
            s_bufs[(k + 2) % N_BUF][...] = _dot(ks_ref[0, tile_of(y + 2)], qcat_ref[...])
            alpha = softmax_tile(s_bufs[k % N_BUF], p_bufs[k % N_BUF], ms_ref, sel_rows(y))
            accs_ref[...] = alpha * (accs_ref[...] + pv)

    n_long = lax.div(qi + 1, LONG_TRIP)
    n_short = lax.div(qi + 1 - n_long * LONG_TRIP + N_BUF - 1, N_BUF)

    def long_trip(i, carry):
        positions(LONG_TRIP * i, LONG_TRIP)
        return carry

    def short_trip(i, carry):
        positions(LONG_TRIP * n_long + N_BUF * i, N_BUF)
        return carry

    lax.fori_loop(0, n_long, long_trip, 0)
    lax.fori_loop(0, n_short, short_trip, 0)
    last = LONG_TRIP * n_long + N_BUF * n_short - 1
    accs_ref[...] = accs_ref[...] + _dot(vs_ref[0, tile_of(last), 0], p_bufs[N_BUF - 1][...])

    o_s = accs_ref[0:HEAD_DIM, :] / accs_ref[HEAD_DIM:HEAD_DIM + 1, :]
    o_w = accw_ref[0:HEAD_DIM, :] / accw_ref[HEAD_DIM:HEAD_DIM + 1, :]
    outs = []
    for r in range(R):
        head = g * R + r
        sl = slice(r * T, (r + 1) * T)
        g_c = gt_ref[0, 0, pl.ds(head, 1), :]
        g_s = gt_ref[0, 0, pl.ds(N_HEADS + head, 1), :]
        g_w = gt_ref[0, 0, pl.ds(2 * N_HEADS + head, 1), :]
        outs.append(g_c * oc_ref[:, sl] + g_s * o_s[:, sl] + g_w * o_w[:, sl])
    o_ref[0] = jnp.concatenate(outs, axis=0).T.astype(o_ref.dtype)


def _nsa(q_t, ks, vs_t, kw, vw_t, kc, vc_t, gates_t, near_bias, cmp_bias, seq):
    b, nq = q_t.shape[0], q_t.shape[1]
    n_blk = seq // SEL_BLOCK
    n_cmp = seq // CMP_STRIDE
    W = HEADS_PER_GROUP * T
    kv_spec = pl.BlockSpec((1, nq, T, GROUP_WIDTH), lambda i, j, k: (i, 0, 0, 0))
    vt_spec = pl.BlockSpec((1, nq, 1, V_ROWS, T), lambda i, j, k: (i, 0, j, 0, 0))
    stat = pltpu.VMEM((1, W), F32)
    acc = pltpu.VMEM((V_ROWS, W), F32)
    scores = pltpu.VMEM((T, W), F32)
    probs = pltpu.VMEM((T, W), BF16)
    return pl.pallas_call(
        functools.partial(_nsa_kernel, n_blk=n_blk, n_cmp=n_cmp),
        grid=(b, N_KV_GROUPS, nq),
        in_specs=[pl.BlockSpec((1, 1, GROUP_WIDTH, T), lambda i, j, k: (i, k, j, 0)),
                  kv_spec, vt_spec, kv_spec, vt_spec,
                  pl.BlockSpec((1, 1, n_cmp, HEAD_DIM), lambda i, j, k: (i, j, 0, 0)),
                  pl.BlockSpec((1, 1, HEAD_DIM, n_cmp), lambda i, j, k: (i, j, 0, 0)),
                  pl.BlockSpec((1, 1, gates_t.shape[2], T), lambda i, j, k: (i, k, 0, 0)),
                  pl.BlockSpec((1, N_NEAR, T, W), lambda i, j, k: (j, 0, 0, 0)),
                  pl.BlockSpec((1, CMP_NEAR, W), lambda i, j, k: (j, 0, 0))],
        out_specs=pl.BlockSpec((1, T, GROUP_WIDTH), lambda i, j, k: (i, k, j)),
        out_shape=jax.ShapeDtypeStruct((b, seq, N_HEADS * HEAD_DIM), BF16),
        scratch_shapes=[pltpu.VMEM((GROUP_WIDTH, W), BF16),
                        pltpu.VMEM((n_cmp + CMP_PER_T, W), F32),
                        pltpu.VMEM((T // LANES, n_cmp + 8, LANES), F32),
                        pltpu.VMEM((n_blk, W), F32),
                        pltpu.VMEM((HEAD_DIM, W), F32), *([scores] * N_BUF), *([probs] * N_BUF), *([scores] * N_NEAR),
                        stat, acc, stat, acc],
        compiler_params=_params(3),
        name="nsa",
    )(q_t, ks, vs_t, kw, vw_t, kc, vc_t, gates_t, near_bias, cmp_bias)


def _mem_kv_kernel(mem_ref, g_ref, w_ref, kg_ref, k_o, v_o):
    h = _rms_rows(mem_ref[0], g_ref[...]).astype(BF16)
    width = k_o.shape[2]
    hd = width // MEM_HEADS
    k = _dot(h, w_ref[:, :width])
    for i in range(MEM_HEADS):
        k_o[0, :, i * hd:(i + 1) * hd] = _rms_rows(k[:, i * hd:(i + 1) * hd], kg_ref[...]).astype(BF16)
    v_o[0] = _dot(h, w_ref[:, width:]).astype(BF16)


def _mem_attn_kernel(q_ref, k_ref, v_ref, qg_ref, o_ref):
    width = q_ref.shape[2]
    hd = width // MEM_HEADS
    for i in range(MEM_HEADS):
        sl = slice(i * hd, (i + 1) * hd)
        q = (_rms_rows(q_ref[0, :, sl].astype(F32), qg_ref[...]) * (hd ** -0.5)).astype(BF16)
        s = _dot_nt(q, k_ref[0, :, sl])
        p = jnp.exp(s - jnp.max(s, axis=-1, keepdims=True))
        p = p / jnp.sum(p, axis=-1, keepdims=True)
        o_ref[0, :, sl] = _dot(p.astype(BF16), v_ref[0, :, sl]).astype(o_ref.dtype)


def _mem_attention(q_mem, mem, mem_norm_g, w_mem_kv, q_g, k_g):
    b, s, width = q_mem.shape
    n_mem, d = mem.shape[1], mem.shape[2]
    hd = width // MEM_HEADS
    kv_shape = jax.ShapeDtypeStruct((b, n_mem, width), BF16)
    kv_block = pl.BlockSpec((1, n_mem, width), lambda i: (i, 0, 0))
    km, vm = pl.pallas_call(
        _mem_kv_kernel,
        grid=(b,),
        in_specs=[pl.BlockSpec((1, n_mem, d), lambda i: (i, 0, 0)),
                  _resident((1, d)), _resident((d, 2 * width)), _resident((1, hd))],
        out_specs=[kv_block, kv_block],
        out_shape=[kv_shape, kv_shape],
        compiler_params=_params(1),
        name="mem_kv",
    )(mem, mem_norm_g.reshape(1, d), w_mem_kv.astype(BF16), k_g.reshape(1, hd))
    kv_block2 = pl.BlockSpec((1, n_mem, width), lambda i, j: (i, 0, 0))
    return pl.pallas_call(
        _mem_attn_kernel,
        grid=(b, s // MEM_TM),
        in_specs=[pl.BlockSpec((1, MEM_TM, width), lambda i, j: (i, j, 0)),
                  kv_block2, kv_block2, _resident((1, hd))],
        out_specs=pl.BlockSpec((1, MEM_TM, width), lambda i, j: (i, j, 0)),
        out_shape=jax.ShapeDtypeStruct((b, s, width), BF16),
        compiler_params=_params(2),
        name="mem_attn",
    )(q_mem, km, vm, q_g.reshape(1, hd))


def _merge_kernel(x_ref, nsa_ref, mem_ref, cb_ref, cc_ref, cx_ref, hc_ref, hx_ref,
                  g1_ref, g2_ref, g3_ref, cw_ref, bias_ref, wo_ref, o_ref):
    j = pl.program_id(1)

    def f32(ref):
        return ref[0].astype(F32)

    u = f32(cc_ref) * f32(cx_ref)
    halo = jnp.where(j > 0, f32(hc_ref) * f32(hx_ref), 0.0)
    prev1 = halo[HALO - 1:HALO, :]
    prev2 = halo[HALO - 2:HALO - 1, :]
    row = lax.broadcasted_iota(jnp.int32, u.shape, 0)
    u1 = jnp.where(row == 0, prev1, pltpu.roll(u, 1, 0))
    u2 = jnp.where(row == 0, prev2, jnp.where(row == 1, prev1, pltpu.roll(u, 2, 0)))
    y = cw_ref[0:1, :] * u2 + cw_ref[1:2, :] * u1 + cw_ref[2:3, :] * u
    o_conv = f32(cb_ref) * (y + bias_ref[...])
    merged = (_sigmoid(f32(g1_ref)) * f32(nsa_ref) + _sigmoid(f32(g2_ref)) * o_conv
              + _sigmoid(f32(g3_ref)) * f32(mem_ref))
    o_ref[0] = x_ref[0] + _dot(merged.astype(BF16), wo_ref[...])


def _merge(x, o_nsa, o_mem, conv_in, merge_g, conv_w, conv_b, w_out):
    b, s, d = x.shape
    tm = MERGE_TM

    def col(c):
        return pl.BlockSpec((1, tm, d), lambda i, j: (i, j, c))

    def halo(c):
        return pl.BlockSpec((1, HALO, d), lambda i, j: (i, jnp.maximum(j * (tm // HALO) - 1, 0), c))

    return pl.pallas_call(
        _merge_kernel,
        grid=(b, s // tm),
        in_specs=[col(0), col(0), col(0),
                  col(0), col(1), col(2), halo(1), halo(2),
                  col(0), col(1), col(2),
                  _resident((CONV_WIDTH, d)), _resident((1, d)), _resident((d, d))],
        out_specs=col(0),
        out_shape=jax.ShapeDtypeStruct((b, s, d), F32),
        compiler_params=_params(2),
        name="merge",
    )(x, o_nsa, o_mem, conv_in, conv_in, conv_in, conv_in, conv_in,
      merge_g, merge_g, merge_g, conv_w, conv_b.reshape(1, d), w_out.astype(BF16))


def _layer(x, mem, ffn1_norm_g, ffn1_w_in, ffn1_w_out, mix_norm_g, w_in, q_norm_g, k_norm_g,
           cmp_pe_k, cmp_w1_k, cmp_w2_k, cmp_pe_v, cmp_w1_v, cmp_w2_v, conv_w, conv_b,
           mem_norm_g, w_mem_kv, mem_q_norm_g, mem_k_norm_g, w_out,
           ffn2_norm_g, ffn2_w_in, ffn2_w_out, near_bias, cmp_bias):
    b, s, d = x.shape
    assert s % T == 0 and s % MERGE_TM == 0 and (b * s) % FFN_TM == 0
    assert WINDOW == 2 * T and REL_MAX_DIST <= T // 2
    assert SEL_BLOCK == 4 * CMP_STRIDE and CMP_BLOCK == 2 * CMP_STRIDE
    nq = s // T
    d_q = N_HEADS * HEAD_DIM
    d_kv = N_KV_GROUPS * HEAD_DIM
    d_conv = conv_w.shape[1]
    d_mem = w_mem_kv.shape[1] // 2

    x = _ffn(x.reshape(b * s, d), ffn1_norm_g, ffn1_w_in, ffn1_w_out).reshape(b, s, d)

    o = 0
    w_q = w_in[:, o:o + d_q]; o += d_q
    w_kc, w_vc, w_ks, w_vs, w_kw, w_vw = [w_in[:, o + i * d_kv:o + (i + 1) * d_kv] for i in range(6)]
    o += 6 * d_kv
    w_g = w_in[:, o:o + 3 * N_HEADS]; o += 3 * N_HEADS
    w_conv = w_in[:, o:o + 3 * d_conv]; o += 3 * d_conv
    w_qm = w_in[:, o:o + d_mem]; o += d_mem
    w_mg = w_in[:, o:]

    w_rows = jnp.concatenate([w_ks, w_kw, w_kc, w_vc, w_conv, w_qm, w_mg], axis=1).astype(BF16)
    widths = (d_kv, d_kv, d_kv, d_kv, 3 * d_conv, d_mem, w_mg.shape[1])
    group_of = jnp.arange(d_kv) // HEAD_DIM
    block_diag = (group_of[:, None] == group_of[None, :]).astype(F32) / HEAD_DIM
    k_gain_row = jnp.tile(k_norm_g, N_KV_GROUPS).reshape(1, d_kv)

    def rows_out(wd, dt):
        return (pl.BlockSpec((1, ROW_TM, wd), lambda i, j: (i, j, 0)), jax.ShapeDtypeStruct((b, s, wd), dt))

    specs = [rows_out(wd, BF16) for wd in widths]
    ks, kw, kc, vc, conv_in, q_mem, merge_g = pl.pallas_call(
        functools.partial(_proj_rows_kernel, widths=widths),
        grid=(b, s // ROW_TM),
        in_specs=[pl.BlockSpec((1, ROW_TM, d), lambda i, j: (i, j, 0)),
                  _resident((1, d)), _resident(w_rows.shape), _resident((d_kv, d_kv)), _resident((1, d_kv))],
        out_specs=[sp[0] for sp in specs],
        out_shape=[sp[1] for sp in specs],
        compiler_params=_params(2),
        name="proj_rows",
    )(x, mix_norm_g.reshape(1, d), w_rows, block_diag, k_gain_row)

    w_g_t = w_g.reshape(d, N_HEADS, 3).transpose(2, 1, 0).reshape(3 * N_HEADS, d)
    n_gate_rows = 128
    w_g_t = jnp.pad(w_g_t, ((0, n_gate_rows - 3 * N_HEADS), (0, 0)))
    w_t = jnp.concatenate([w_q.T, w_vs.T, w_vw.T, w_g_t], axis=0).astype(BF16)

    def t_out(rows, dt):
        return (pl.BlockSpec((1, 1, rows, T), lambda i, j: (i, j, 0, 0)),
                jax.ShapeDtypeStruct((b, nq, rows, T), dt))

    v_rows = N_KV_GROUPS * V_ROWS
    t_specs = [t_out(d_q, BF16), t_out(v_rows, BF16), t_out(v_rows, BF16), t_out(n_gate_rows, F32)]
    q_t, vs_t, vw_t, gates_t = pl.pallas_call(
        functools.partial(_proj_t_kernel, d_q=d_q, d_kv=d_kv),
        grid=(b, nq),
        in_specs=[pl.BlockSpec((1, T, d), lambda i, j: (i, j, 0)),
                  _resident((1, d)), _resident(w_t.shape), _resident((HEAD_DIM, 1))],
        out_specs=[sp[0] for sp in t_specs],
        out_shape=[sp[1] for sp in t_specs],
        compiler_params=_params(2),
        name="proj_t",
    )(x, mix_norm_g.reshape(1, d), w_t, q_norm_g.reshape(HEAD_DIM, 1))

    n_chunks = s // CMP_STRIDE

    def chunked(a):
        return a.reshape(b, n_chunks, CMP_STRIDE, N_KV_GROUPS, HEAD_DIM).transpose(0, 3, 1, 2, 4) \
                .reshape(b, N_KV_GROUPS, n_chunks, CMP_STRIDE * HEAD_DIM)

    k_cmp = _compress(chunked(kc), cmp_pe_k, cmp_w1_k, cmp_w2_k, k_norm_g, True)
    v_cmp_t = _compress(chunked(vc), cmp_pe_v, cmp_w1_v, cmp_w2_v, k_norm_g, False)

    o_nsa = _nsa(q_t, ks.reshape(b, nq, T, d_kv), vs_t.reshape(b, nq, N_KV_GROUPS, V_ROWS, T),
                 kw.reshape(b, nq, T, d_kv), vw_t.reshape(b, nq, N_KV_GROUPS, V_ROWS, T),
                 k_cmp, v_cmp_t, gates_t, near_bias, cmp_bias, s)

    o_mem = _mem_attention(q_mem, mem, mem_norm_g, w_mem_kv, mem_q_norm_g, mem_k_norm_g)
    x = _merge(x, o_nsa, o_mem, conv_in, merge_g, conv_w, conv_b, w_out)
    x = _ffn(x.reshape(b * s, d), ffn2_norm_g, ffn2_w_in, ffn2_w_out).reshape(b, s, d)
    return x


def kernel(x, mem, ffn1_norm_g, ffn1_w_in, ffn1_w_out, mix_norm_g, w_in, q_norm_g, k_norm_g, cmp_pe_k, cmp_w1_k, cmp_w2_k, cmp_pe_v, cmp_w1_v, cmp_w2_v, conv_w, conv_b, mem_norm_g, w_mem_kv, mem_q_norm_g, mem_k_norm_g, w_out, ffn2_norm_g, ffn2_w_in, ffn2_w_out, rel_bias):
    near_bias, cmp_bias = _bias_tiles(rel_bias)
    for l in range(ffn1_norm_g.shape[0]):
        x = _layer(x, mem, ffn1_norm_g[l], ffn1_w_in[l], ffn1_w_out[l], mix_norm_g[l], w_in[l],
                   q_norm_g[l], k_norm_g[l], cmp_pe_k[l], cmp_w1_k[l], cmp_w2_k[l],
                   cmp_pe_v[l], cmp_w1_v[l], cmp_w2_v[l], conv_w[l], conv_b[l],
                   mem_norm_g[l], w_mem_kv[l], mem_q_norm_g[l], mem_k_norm_g[l], w_out[l],
                   ffn2_norm_g[l], ffn2_w_in[l], ffn2_w_out[l], near_bias, cmp_bias)
    return x
```

```python
import functools
import math

import jax
import jax.numpy as jnp
from jax import lax
from jax.experimental import pallas as pl
from jax.experimental.pallas import tpu as pltpu

N_HEADS = 16
HEAD_DIM = 64
N_KV_GROUPS = 4
HEADS_PER_GROUP = N_HEADS // N_KV_GROUPS
GROUP_WIDTH = HEADS_PER_GROUP * HEAD_DIM
CMP_BLOCK = 32
CMP_STRIDE = 16
SEL_BLOCK = 64
N_SELECT = 16
WINDOW = 512
FORCE = 1e4
CONV_WIDTH = 3
MEM_HEADS = 4
REL_BUCKETS = 32
REL_MAX_DIST = 128
EPS = 1e-6
NEG = -1e30

T = 256
BLK_PER_T = T // SEL_BLOCK
CMP_PER_T = T // CMP_STRIDE
CMP_NEAR = 2 * CMP_PER_T
N_NEAR = WINDOW // T + 1
N_BUF = 4
LONG_TRIP = 2 * N_BUF
CMP_CHUNK = 128
FFN_TM = 512
ROW_TM = 256
MEM_TM = 512
MERGE_TM = 512
HALO = 16
LANES = 128
BF16_SUBLANES = 16
V_ROWS = HEAD_DIM + BF16_SUBLANES
LOG2E = math.log2(math.e)
VMEM_LIMIT = 52 * 1024 * 1024

F32 = jnp.float32
BF16 = jnp.bfloat16
HI = lax.Precision.HIGHEST


def _dot(a, b):
    return jnp.dot(a, b, preferred_element_type=F32)


def _dot_nt(a, b):
    return lax.dot_general(a, b, (((1,), (1,)), ((), ())), preferred_element_type=F32)


def _rms_rows(xf, g):
    return xf * lax.rsqrt(jnp.mean(xf * xf, axis=-1, keepdims=True) + EPS) * g


def _sigmoid(x):
    return 1.0 / (1.0 + jnp.exp(-x))


def _resident(shape):
    zeros = (0,) * len(shape)
    return pl.BlockSpec(shape, lambda *_: zeros, pipeline_mode=pl.Buffered(1))


def _params(n_axes):
    return pltpu.CompilerParams(dimension_semantics=("arbitrary",) * n_axes,
                                vmem_limit_bytes=VMEM_LIMIT)


def _ffn_kernel(x_ref, g_ref, wa_ref, wb_ref, wo_ref, o_ref, *, ff_chunk):
    x = x_ref[...]
    h = _rms_rows(x, g_ref[...]).astype(BF16)
    d_ff = wa_ref.shape[1]
    acc = jnp.zeros(x.shape, F32)
    for lo in range(0, d_ff, ff_chunk):
        a = _dot(h, wa_ref[:, lo:lo + ff_chunk])
        b = _dot(h, wb_ref[:, lo:lo + ff_chunk])
        z = (a * _sigmoid(a) * b).astype(BF16)
        acc = acc + _dot(z, wo_ref[lo:lo + ff_chunk, :])
    o_ref[...] = x + 0.5 * acc


def _ffn(x2d, g, w_in, w_out):
    n, d = x2d.shape
    d_ff = w_out.shape[0]
    wa = w_in[:, :d_ff].astype(BF16)
    wb = w_in[:, d_ff:].astype(BF16)
    wo = w_out.astype(BF16)
    ff_chunk = d_ff // 2 if (d_ff // 2) % 128 == 0 else d_ff
    return pl.pallas_call(
        functools.partial(_ffn_kernel, ff_chunk=ff_chunk),
        grid=(n // FFN_TM,),
        in_specs=[pl.BlockSpec((FFN_TM, d), lambda i: (i, 0)),
                  _resident((1, d)), _resident((d, d_ff)), _resident((d, d_ff)), _resident((d_ff, d))],
        out_specs=pl.BlockSpec((FFN_TM, d), lambda i: (i, 0)),
        out_shape=jax.ShapeDtypeStruct((n, d), F32),
        compiler_params=_params(1),
        name="ffn",
    )(x2d, g.reshape(1, d), wa, wb, wo)


def _proj_rows_kernel(x_ref, g_ref, w_ref, bd_ref, kg_ref,
                      ks_o, kw_o, kc_o, vc_o, conv_o, qm_o, mg_o, *, widths):
    h = _rms_rows(x_ref[0], g_ref[...]).astype(BF16)

    def knorm(k):
        ms = jnp.dot(k * k, bd_ref[...], precision=HI, preferred_element_type=F32)
        return (k * lax.rsqrt(ms + EPS) * kg_ref[...]).astype(BF16)

    lo = 0
    outs = (ks_o, kw_o, kc_o, vc_o, conv_o, qm_o, mg_o)
    for idx, (o_ref, wd) in enumerate(zip(outs, widths)):
        y = _dot(h, w_ref[:, lo:lo + wd])
        o_ref[0] = knorm(y) if idx < 2 else y.astype(o_ref.dtype)
        lo += wd


def _proj_t_kernel(x_ref, g_ref, wt_ref, qg_ref, q_o, vs_o, vw_o, gt_o, *, d_q, d_kv):
    h = _rms_rows(x_ref[0], g_ref[...]).astype(BF16)
    qg = qg_ref[...] * (HEAD_DIM ** -0.5 * LOG2E)
    out_t = _dot_nt(wt_ref[...], h)
    for hd in range(d_q // HEAD_DIM):
        q = out_t[hd * HEAD_DIM:(hd + 1) * HEAD_DIM, :]
        qn = q * lax.rsqrt(jnp.mean(q * q, axis=0, keepdims=True) + EPS) * qg
        q_o[0, 0, hd * HEAD_DIM:(hd + 1) * HEAD_DIM, :] = qn.astype(BF16)
    lo = d_q
    for v_o in (vs_o, vw_o):
        y = out_t[lo:lo + d_kv, :].astype(BF16)
        for grp in range(d_kv // HEAD_DIM):
            v_o[0, 0, grp * V_ROWS:grp * V_ROWS + HEAD_DIM, :] = y[grp * HEAD_DIM:(grp + 1) * HEAD_DIM]
            v_o[0, 0, grp * V_ROWS + HEAD_DIM:(grp + 1) * V_ROWS, :] = jnp.ones((V_ROWS - HEAD_DIM, T), BF16)
        lo += d_kv
    gt_o[0, 0] = _sigmoid(out_t[lo:, :])


def _compress_kernel(c_ref, pe_ref, w1_ref, w2_ref, kg_ref, o_ref, *, is_key):
    c = c_ref[0, 0]
    n_chunks, half = c.shape
    a = _dot((c + pe_ref[:, :half]).astype(BF16), w1_ref[:half, :])
    b = _dot((c + pe_ref[:, half:]).astype(BF16), w1_ref[half:, :])
    hid = a + pltpu.roll(b, n_chunks - 1, 0)
    hid = (hid * _sigmoid(hid)).astype(BF16)
    if is_key:
        y = _dot(hid, w2_ref[...])
        y = _rms_rows(y, kg_ref[...])
        row = lax.broadcasted_iota(jnp.int32, y.shape, 0)
        o_ref[0, 0] = jnp.where(row < n_chunks - 1, y, 0.0).astype(BF16)
    else:
        y = _dot_nt(w2_ref[...], hid)
        col = lax.broadcasted_iota(jnp.int32, y.shape, 1)
        o_ref[0, 0] = jnp.where(col < n_chunks - 1, y, 0.0).astype(BF16)


def _compress(c4, pe, w1, w2, k_gain, is_key):
    b, g, n_chunks, width = c4.shape
    hidden = w1.shape[1]
    w2_arg = w2.astype(BF16) if is_key else w2.T.astype(BF16)
    out_block = (1, 1, n_chunks, HEAD_DIM) if is_key else (1, 1, HEAD_DIM, n_chunks)
    return pl.pallas_call(
        functools.partial(_compress_kernel, is_key=is_key),
        grid=(b, g),
        in_specs=[pl.BlockSpec((1, 1, n_chunks, width), lambda i, j: (i, j, 0, 0)),
                  _resident((1, 2 * width)), _resident((2 * width, hidden)),
                  _resident(w2_arg.shape), _resident((1, HEAD_DIM))],
        out_specs=pl.BlockSpec(out_block, lambda i, j: (i, j, 0, 0)),
        out_shape=jax.ShapeDtypeStruct((b, g) + out_block[2:], BF16),
        compiler_params=_params(2),
        name="compress_k" if is_key else "compress_v",
    )(c4, pe.reshape(1, 2 * width), w1.astype(BF16), w2_arg, k_gain.reshape(1, HEAD_DIM))


def _bias_kernel(rb_ref, bkt_near_ref, bkt_cmp_ref, near_o, cmp_o):
    h = pl.program_id(0)
    far = rb_ref[REL_BUCKETS - 1, h]

    def lookup(bkt):
        out = jnp.zeros(bkt.shape, F32)
        for k in range(REL_BUCKETS - 1):
            out = jnp.where(bkt == k, (rb_ref[k, h] - far) * LOG2E, out)
        return out

    key = lax.broadcasted_iota(jnp.int32, (T, T), 0)
    qry = lax.broadcasted_iota(jnp.int32, (T, T), 1)
    near_o[0, 0] = jnp.where(qry >= key, lookup(bkt_near_ref[0]), NEG)
    near_o[0, 1] = lookup(bkt_near_ref[1])
    near_o[0, 2] = jnp.where(key > qry, 0.0, NEG)
    cmp_o[0] = lookup(bkt_cmp_ref[...])


def _rel_bucket(dist):
    n = jnp.maximum(dist, 0)
    max_exact = REL_BUCKETS // 2
    nf = jnp.maximum(n, 1).astype(F32)
    large = max_exact + (jnp.log(nf / max_exact) / math.log(REL_MAX_DIST / max_exact)
                         * (REL_BUCKETS - max_exact)).astype(jnp.int32)
    large = jnp.minimum(large, REL_BUCKETS - 1)
    return jnp.where(n < max_exact, n, large)


def _bias_tiles(rel_bias):
    key = jnp.arange(T)[:, None]
    qry = jnp.arange(T)[None, :]
    bkt_near = jnp.stack([_rel_bucket(qry - key), _rel_bucket(qry - key + T)]).astype(jnp.int32)
    j = jnp.arange(CMP_NEAR)[:, None]
    bkt_cmp = _rel_bucket(qry - CMP_STRIDE * (j - CMP_PER_T) - (CMP_BLOCK - 1)).astype(jnp.int32)
    return pl.pallas_call(
        _bias_kernel,
        grid=(N_HEADS,),
        in_specs=[pl.BlockSpec(memory_space=pltpu.SMEM),
                  pl.BlockSpec((2, T, T), lambda h: (0, 0, 0)),
                  pl.BlockSpec((CMP_NEAR, T), lambda h: (0, 0))],
        out_specs=[pl.BlockSpec((1, N_NEAR, T, T), lambda h: (h // HEADS_PER_GROUP, 0, 0, h % HEADS_PER_GROUP)),
                   pl.BlockSpec((1, CMP_NEAR, T), lambda h: (h // HEADS_PER_GROUP, 0, h % HEADS_PER_GROUP))],
        out_shape=[jax.ShapeDtypeStruct((N_KV_GROUPS, N_NEAR, T, HEADS_PER_GROUP * T), F32),
                   jax.ShapeDtypeStruct((N_KV_GROUPS, CMP_NEAR, HEADS_PER_GROUP * T), F32)],
        compiler_params=_params(1),
        name="bias_tiles",
    )(rel_bias, bkt_near, bkt_cmp)


def _nsa_kernel(q_ref, ks_ref, vs_ref, kw_ref, vw_ref, kc_ref, vc_ref, gt_ref, nb_ref, cb_ref,
                o_ref,
                qcat_ref, sc_ref, psum_ref, selneg_ref, oc_ref,
                s0_ref, s1_ref, s2_ref, s3_ref, p0_ref, p1_ref, p2_ref, p3_ref, w0_ref, w1_ref, w2_ref,
                ms_ref, accs_ref, mw_ref, accw_ref, *, n_blk, n_cmp):
    g = pl.program_id(1)
    qi = pl.program_id(2)
    R = HEADS_PER_GROUP
    W = R * T
    qry = lax.broadcasted_iota(jnp.int32, (1, W), 1) & (T - 1)
    t = qi * T + qry

    def q_head(r):
        return q_ref[0, 0, r * HEAD_DIM:(r + 1) * HEAD_DIM, :]

    rowgrp = lax.shift_right_logical(lax.broadcasted_iota(jnp.int32, (GROUP_WIDTH, T), 0),
                                     int(math.log2(HEAD_DIM)))
    for r in range(R):
        q4 = jnp.concatenate([q_head(r).astype(F32)] * N_KV_GROUPS, axis=0)
        qcat_ref[:, r * T:(r + 1) * T] = jnp.where(rowgrp == g, q4, 0.0).astype(BF16)

    for m_ref, acc_ref in ((ms_ref, accs_ref), (mw_ref, accw_ref)):
        m_ref[...] = jnp.full(m_ref.shape, NEG, F32)
        acc_ref[...] = jnp.zeros(acc_ref.shape, F32)

    def softmax_tile(s_ref, p_ref, m_ref, rows):
        top = None
        for j in range(BLK_PER_T):
            blk_max = s_ref[j * SEL_BLOCK:(j + 1) * SEL_BLOCK, :].reshape(SEL_BLOCK // 8, 8, W).max(axis=0)
            if rows is not None:
                blk_max = blk_max + rows[j]
            top = blk_max if top is None else jnp.maximum(top, blk_max)
        m_old = m_ref[...]
        m_new = jnp.maximum(m_old, jnp.max(top, axis=0, keepdims=True))
        for j in range(BLK_PER_T):
            shift = m_new if rows is None else m_new - rows[j]
            sl = slice(j * SEL_BLOCK, (j + 1) * SEL_BLOCK)
            p_ref[sl, :] = jnp.exp2(s_ref[sl, :] - shift).astype(BF16)
        m_ref[...] = m_new
        return jnp.exp2(m_old - m_new)

    s_bufs = (s0_ref, s1_ref, s2_ref, s3_ref)
    p_bufs = (p0_ref, p1_ref, p2_ref, p3_ref)
    w_bufs = (w0_ref, w1_ref, w2_ref)
    neg_row = jnp.full((1, W), NEG, F32)

    def tile_of(y):
        return jnp.where(y == 0, qi, jnp.where(y == 1, jnp.maximum(qi - 1, 0), jnp.clip(y - 2, 0, qi)))

    win_tiles = [jnp.maximum(qi - back, 0) for back in range(N_NEAR)]
    for back in range(N_NEAR):
        w_bufs[back][...] = _dot(kw_ref[0, win_tiles[back]], qcat_ref[...]) + nb_ref[0, back]
    near0 = pl.multiple_of(qi * CMP_PER_T, CMP_PER_T)
    sc_ref[0:CMP_PER_T, :] = jnp.zeros((CMP_PER_T, W), F32)
    sc_ref[CMP_PER_T:, :] = _dot(kc_ref[0, 0], jnp.concatenate([q_head(r) for r in range(R)], axis=1))
    sc_ref[pl.ds(near0, CMP_NEAR), :] = sc_ref[pl.ds(near0, CMP_NEAR), :] + cb_ref[0]
    s0_ref[...] = _dot(ks_ref[0, tile_of(0)], qcat_ref[...]) + nb_ref[0, 0]
    s1_ref[...] = _dot(ks_ref[0, tile_of(1)], qcat_ref[...]) + nb_ref[0, 1]

    for back in range(N_NEAR):
        rows = None if back == 0 else [jnp.where(qi >= back, 0.0, neg_row)] * BLK_PER_T
        alpha = softmax_tile(w_bufs[back], p_bufs[back], mw_ref, rows)
        accw_ref[...] = alpha * accw_ref[...] + _dot(vw_ref[0, win_tiles[back], 0], p_bufs[back][...])

    def select_blocks(nc, nb):
        c_idx = lax.broadcasted_iota(jnp.int32, (nc, W), 0)
        valid_c = (c_idx * CMP_STRIDE + (CMP_BLOCK - 1) <= t) & (c_idx < n_cmp - 1)
        s = jnp.where(valid_c, sc_ref[CMP_PER_T:CMP_PER_T + nc, :], NEG)
        m = jnp.max(s, axis=0, keepdims=True)
        p = jnp.where(valid_c, jnp.exp2(s - m), 0.0)
        l = jnp.sum(p, axis=0, keepdims=True)
        p = p * jnp.where(l > 0.0, 1.0 / jnp.where(l > 0.0, l, 1.0), 0.0)
        oc_ref[...] = _dot(vc_ref[0, 0, :, 0:nc], p.astype(BF16))
        psum = p[:, 0:T]
        for r in range(1, R):
            psum = psum + p[:, r * T:(r + 1) * T]

        for ln in range(T // LANES):
            psum_ref[ln, 0:8, :] = jnp.zeros((8, LANES), F32)
            psum_ref[ln, 8:8 + nc, :] = psum[:, ln * LANES:(ln + 1) * LANES]

        def every_fourth(off):
            return jnp.concatenate([psum_ref[ln, pl.ds(8 + off, nb, stride=SEL_BLOCK // CMP_STRIDE), :]
                                    for ln in range(T // LANES)], axis=1)

        imp = every_fourth(0) + every_fourth(1) + every_fourth(2) + 0.5 * (every_fourth(3) + every_fourth(-1))
        blk = lax.broadcasted_iota(jnp.int32, (nb, T), 0)
        cur = lax.shift_right_logical(t[:, 0:T], int(math.log2(SEL_BLOCK)))
        forced = (blk == 0) | (blk == cur) | (blk == cur - 1)
        score = jnp.where(blk <= cur, imp + jnp.where(forced, FORCE, 0.0), -FORCE)
        for _ in range(min(N_SELECT, n_blk)):
            best = jnp.max(score, axis=0, keepdims=True)
            first = jnp.min(jnp.where(score == best, blk, nb), axis=0, keepdims=True)
            score = jnp.where(blk == first, -jnp.inf, score)
        selneg = jnp.where(score == -jnp.inf, 0.0, NEG)
        selneg_ref[0:nb, :] = jnp.concatenate([selneg] * R, axis=1)

    nq = n_cmp // CMP_PER_T
    n_var = max(1, n_cmp // CMP_CHUNK)
    tiles_per_var = nq // n_var
    for v in range(n_var):
        @pl.when((qi >= v * tiles_per_var) & (qi < (v + 1) * tiles_per_var))
        def _(v=v):
            n_tiles = (v + 1) * tiles_per_var
            select_blocks(n_tiles * CMP_PER_T, n_tiles * BLK_PER_T)

    def sel_rows(y):
        kj = tile_of(y)
        return [jnp.where(y <= qi, selneg_ref[pl.ds(kj * BLK_PER_T + j, 1), :], neg_row)
                for j in range(BLK_PER_T)]

    p_bufs[N_BUF - 1][...] = jnp.zeros(p0_ref.shape, BF16)

    def positions(first, count, final=False):
        for k in range(count):
            y = first + k
            pv = _dot(vs_ref[0, tile_of(y - 1), 0], p_bufs[(k - 1) % N_BUF][...])
            if not final or k + 2 < count:
                s_bufs[(k + 2) % N_BUF][...] = _dot(ks_ref[0, tile_of(y + 2)], qcat_ref[...])
            alpha = softmax_tile(s_bufs[k % N_BUF], p_bufs[k % N_BUF], ms_ref, sel_rows(y))
            accs_ref[...] = alpha * (accs_ref[...] + pv)

    n_long = lax.div(qi, LONG_TRIP)
    n_short = lax.div(qi - n_long * LONG_TRIP, N_BUF)

    def long_trip(i, carry):
        positions(LONG_TRIP * i, LONG_TRIP)
        return carry

    def short_trip(i, carry):
        positions(LONG_TRIP * n_long + N_BUF * i, N_BUF)
        return carry

    lax.fori_loop(0, n_long, long_trip, 0)
    lax.fori_loop(0, n_short, short_trip, 0)
    final_first = LONG_TRIP * n_long + N_BUF * n_short
    positions(final_first, N_BUF, final=True)
    accs_ref[...] = accs_ref[...] + _dot(vs_ref[0, tile_of(final_first + N_BUF - 1), 0], p_bufs[N_BUF - 1][...])

    o_s = accs_ref[0:HEAD_DIM, :] / accs_ref[HEAD_DIM:HEAD_DIM + 1, :]
    o_w = accw_ref[0:HEAD_DIM, :] / accw_ref[HEAD_DIM:HEAD_DIM + 1, :]
    outs = []
    for r in range(R):
        head = g * R + r
        sl = slice(r * T, (r + 1) * T)
        g_c = gt_ref[0, 0, pl.ds(head, 1), :]
        g_s = gt_ref[0, 0, pl.ds(N_HEADS + head, 1), :]
        g_w = gt_ref[0, 0, pl.ds(2 * N_HEADS + head, 1), :]
        outs.append(g_c * oc_ref[:, sl] + g_s * o_s[:, sl] + g_w * o_w[:, sl])
    o_ref[0] = jnp.concatenate(outs, axis=0).T.astype(o_ref.dtype)


def _nsa(q_t, ks, vs_t, kw, vw_t, kc, vc_t, gates_t, near_bias, cmp_bias, seq):
    b, nq = q_t.shape[0], q_t.shape[1]
    n_blk = seq // SEL_BLOCK
    n_cmp = seq // CMP_STRIDE
    W = HEADS_PER_GROUP * T
    kv_spec = pl.BlockSpec((1, nq, T, GROUP_WIDTH), lambda i, j, k: (i, 0, 0, 0))
    vt_spec = pl.BlockSpec((1, nq, 1, V_ROWS, T), lambda i, j, k: (i, 0, j, 0, 0))
    stat = pltpu.VMEM((1, W), F32)
    acc = pltpu.VMEM((V_ROWS, W), F32)
    scores = pltpu.VMEM((T, W), F32)
    probs = pltpu.VMEM((T, W), BF16)
    return pl.pallas_call(
        functools.partial(_nsa_kernel, n_blk=n_blk, n_cmp=n_cmp),
        grid=(b, N_KV_GROUPS, nq),
        in_specs=[pl.BlockSpec((1, 1, GROUP_WIDTH, T), lambda i, j, k: (i, k, j, 0)),
                  kv_spec, vt_spec, kv_spec, vt_spec,
                  pl.BlockSpec((1, 1, n_cmp, HEAD_DIM), lambda i, j, k: (i, j, 0, 0)),
                  pl.BlockSpec((1, 1, HEAD_DIM, n_cmp), lambda i, j, k: (i, j, 0, 0)),
                  pl.BlockSpec((1, 1, gates_t.shape[2], T), lambda i, j, k: (i, k, 0, 0)),
                  pl.BlockSpec((1, N_NEAR, T, W), lambda i, j, k: (j, 0, 0, 0)),
                  pl.BlockSpec((1, CMP_NEAR, W), lambda i, j, k: (j, 0, 0))],
        out_specs=pl.BlockSpec((1, T, GROUP_WIDTH), lambda i, j, k: (i, k, j)),
        out_shape=jax.ShapeDtypeStruct((b, seq, N_HEADS * HEAD_DIM), BF16),
        scratch_shapes=[pltpu.VMEM((GROUP_WIDTH, W), BF16),
                        pltpu.VMEM((n_cmp + CMP_PER_T, W), F32),
                        pltpu.VMEM((T // LANES, n_cmp + 8, LANES), F32),
                        pltpu.VMEM((n_blk, W), F32),
                        pltpu.VMEM((HEAD_DIM, W), F32), *([scores] * N_BUF), *([probs] * N_BUF), *([scores] * N_NEAR),
                        stat, acc, stat, acc],
        compiler_params=_params(3),
        name="nsa",
    )(q_t, ks, vs_t, kw, vw_t, kc, vc_t, gates_t, near_bias, cmp_bias)


def _mem_kv_kernel(mem_ref, g_ref, w_ref, kg_ref, k_o, v_o):
    h = _rms_rows(mem_ref[0], g_ref[...]).astype(BF16)
    width = k_o.shape[2]
    hd = width // MEM_HEADS
    k = _dot(h, w_ref[:, :width])
    for i in range(MEM_HEADS):
        k_o[0, :, i * hd:(i + 1) * hd] = _rms_rows(k[:, i * hd:(i + 1) * hd], kg_ref[...]).astype(BF16)
    v_o[0] = _dot(h, w_ref[:, width:]).astype(BF16)


def _mem_attn_kernel(q_ref, k_ref, v_ref, qg_ref, o_ref):
    width = q_ref.shape[2]
    hd = width // MEM_HEADS
    for i in range(MEM_HEADS):
        sl = slice(i * hd, (i + 1) * hd)
        q = (_rms_rows(q_ref[0, :, sl].astype(F32), qg_ref[...]) * (hd ** -0.5)).astype(BF16)
        s = _dot_nt(q, k_ref[0, :, sl])
        p = jnp.exp(s - jnp.max(s, axis=-1, keepdims=True))
        p = p / jnp.sum(p, axis=-1, keepdims=True)
        o_ref[0, :, sl] = _dot(p.astype(BF16), v_ref[0, :, sl]).astype(o_ref.dtype)


def _mem_attention(q_mem, mem, mem_norm_g, w_mem_kv, q_g, k_g):
    b, s, width = q_mem.shape
    n_mem, d = mem.shape[1], mem.shape[2]
    hd = width // MEM_HEADS
    kv_shape = jax.ShapeDtypeStruct((b, n_mem, width), BF16)
    kv_block = pl.BlockSpec((1, n_mem, width), lambda i: (i, 0, 0))
    km, vm = pl.pallas_call(
        _mem_kv_kernel,
        grid=(b,),
        in_specs=[pl.BlockSpec((1, n_mem, d), lambda i: (i, 0, 0)),
                  _resident((1, d)), _resident((d, 2 * width)), _resident((1, hd))],
        out_specs=[kv_block, kv_block],
        out_shape=[kv_shape, kv_shape],
        compiler_params=_params(1),
        name="mem_kv",
    )(mem, mem_norm_g.reshape(1, d), w_mem_kv.astype(BF16), k_g.reshape(1, hd))
    kv_block2 = pl.BlockSpec((1, n_mem, width), lambda i, j: (i, 0, 0))
    return pl.pallas_call(
        _mem_attn_kernel,
        grid=(b, s // MEM_TM),
        in_specs=[pl.BlockSpec((1, MEM_TM, width), lambda i, j: (i, j, 0)),
                  kv_block2, kv_block2, _resident((1, hd))],
        out_specs=pl.BlockSpec((1, MEM_TM, width), lambda i, j: (i, j, 0)),
        out_shape=jax.ShapeDtypeStruct((b, s, width), BF16),
        compiler_params=_params(2),
        name="mem_attn",
    )(q_mem, km, vm, q_g.reshape(1, hd))


def _merge_kernel(x_ref, nsa_ref, mem_ref, cb_ref, cc_ref, cx_ref, hc_ref, hx_ref,
                  g1_ref, g2_ref, g3_ref, cw_ref, bias_ref, wo_ref, o_ref):
    j = pl.program_id(1)

    def f32(ref):
        return ref[0].astype(F32)

    u = f32(cc_ref) * f32(cx_ref)
    halo = jnp.where(j > 0, f32(hc_ref) * f32(hx_ref), 0.0)
    prev1 = halo[HALO - 1:HALO, :]
    prev2 = halo[HALO - 2:HALO - 1, :]
    row = lax.broadcasted_iota(jnp.int32, u.shape, 0)
    u1 = jnp.where(row == 0, prev1, pltpu.roll(u, 1, 0))
    u2 = jnp.where(row == 0, prev2, jnp.where(row == 1, prev1, pltpu.roll(u, 2, 0)))
    y = cw_ref[0:1, :] * u2 + cw_ref[1:2, :] * u1 + cw_ref[2:3, :] * u
    o_conv = f32(cb_ref) * (y + bias_ref[...])
    merged = (_sigmoid(f32(g1_ref)) * f32(nsa_ref) + _sigmoid(f32(g2_ref)) * o_conv
              + _sigmoid(f32(g3_ref)) * f32(mem_ref))
    o_ref[0] = x_ref[0] + _dot(merged.astype(BF16), wo_ref[...])


def _merge(x, o_nsa, o_mem, conv_in, merge_g, conv_w, conv_b, w_out):
    b, s, d = x.shape
    tm = MERGE_TM

    def col(c):
        return pl.BlockSpec((1, tm, d), lambda i, j: (i, j, c))

    def halo(c):
        return pl.BlockSpec((1, HALO, d), lambda i, j: (i, jnp.maximum(j * (tm // HALO) - 1, 0), c))

    return pl.pallas_call(
        _merge_kernel,
        grid=(b, s // tm),
        in_specs=[col(0), col(0), col(0),
                  col(0), col(1), col(2), halo(1), halo(2),
                  col(0), col(1), col(2),
                  _resident((CONV_WIDTH, d)), _resident((1, d)), _resident((d, d))],
        out_specs=col(0),
        out_shape=jax.ShapeDtypeStruct((b, s, d), F32),
        compiler_params=_params(2),
        name="merge",
    )(x, o_nsa, o_mem, conv_in, conv_in, conv_in, conv_in, conv_in,
      merge_g, merge_g, merge_g, conv_w, conv_b.reshape(1, d), w_out.astype(BF16))


def _layer(x, mem, ffn1_norm_g, ffn1_w_in, ffn1_w_out, mix_norm_g, w_in, q_norm_g, k_norm_g,
           cmp_pe_k, cmp_w1_k, cmp_w2_k, cmp_pe_v, cmp_w1_v, cmp_w2_v, conv_w, conv_b,
           mem_norm_g, w_mem_kv, mem_q_norm_g, mem_k_norm_g, w_out,
           ffn2_norm_g, ffn2_w_in, ffn2_w_out, near_bias, cmp_bias):
    b, s, d = x.shape
    assert s % T == 0 and s % MERGE_TM == 0 and (b * s) % FFN_TM == 0
    assert WINDOW == 2 * T and REL_MAX_DIST <= T // 2
    assert SEL_BLOCK == 4 * CMP_STRIDE and CMP_BLOCK == 2 * CMP_STRIDE
    nq = s // T
    d_q = N_HEADS * HEAD_DIM
    d_kv = N_KV_GROUPS * HEAD_DIM
    d_conv = conv_w.shape[1]
    d_mem = w_mem_kv.shape[1] // 2

    x = _ffn(x.reshape(b * s, d), ffn1_norm_g, ffn1_w_in, ffn1_w_out).reshape(b, s, d)

    o = 0
    w_q = w_in[:, o:o + d_q]; o += d_q
    w_kc, w_vc, w_ks, w_vs, w_kw, w_vw = [w_in[:, o + i * d_kv:o + (i + 1) * d_kv] for i in range(6)]
    o += 6 * d_kv
    w_g = w_in[:, o:o + 3 * N_HEADS]; o += 3 * N_HEADS
    w_conv = w_in[:, o:o + 3 * d_conv]; o += 3 * d_conv
    w_qm = w_in[:, o:o + d_mem]; o += d_mem
    w_mg = w_in[:, o:]

    w_rows = jnp.concatenate([w_ks, w_kw, w_kc, w_vc, w_conv, w_qm, w_mg], axis=1).astype(BF16)
    widths = (d_kv, d_kv, d_kv, d_kv, 3 * d_conv, d_mem, w_mg.shape[1])
    group_of = jnp.arange(d_kv) // HEAD_DIM
    block_diag = (group_of[:, None] == group_of[None, :]).astype(F32) / HEAD_DIM
    k_gain_row = jnp.tile(k_norm_g, N_KV_GROUPS).reshape(1, d_kv)

    def rows_out(wd, dt):
        return (pl.BlockSpec((1, ROW_TM, wd), lambda i, j: (i, j, 0)), jax.ShapeDtypeStruct((b, s, wd), dt))

    specs = [rows_out(wd, BF16) for wd in widths]
    ks, kw, kc, vc, conv_in, q_mem, merge_g = pl.pallas_call(
        functools.partial(_proj_rows_kernel, widths=widths),
        grid=(b, s // ROW_TM),
        in_specs=[pl.BlockSpec((1, ROW_TM, d), lambda i, j: (i, j, 0)),
                  _resident((1, d)), _resident(w_rows.shape), _resident((d_kv, d_kv)), _resident((1, d_kv))],
        out_specs=[sp[0] for sp in specs],
        out_shape=[sp[1] for sp in specs],
        compiler_params=_params(2),
        name="proj_rows",
    )(x, mix_norm_g.reshape(1, d), w_rows, block_diag, k_gain_row)

    w_g_t = w_g.reshape(d, N_HEADS, 3).transpose(2, 1, 0).reshape(3 * N_HEADS, d)
    n_gate_rows = 128
    w_g_t = jnp.pad(w_g_t, ((0, n_gate_rows - 3 * N_HEADS), (0, 0)))
    w_t = jnp.concatenate([w_q.T, w_vs.T, w_vw.T, w_g_t], axis=0).astype(BF16)

    def t_out(rows, dt):
        return (pl.BlockSpec((1, 1, rows, T), lambda i, j: (i, j, 0, 0)),
                jax.ShapeDtypeStruct((b, nq, rows, T), dt))

    v_rows = N_KV_GROUPS * V_ROWS
    t_specs = [t_out(d_q, BF16), t_out(v_rows, BF16), t_out(v_rows, BF16), t_out(n_gate_rows, F32)]
    q_t, vs_t, vw_t, gates_t = pl.pallas_call(
        functools.partial(_proj_t_kernel, d_q=d_q, d_kv=d_kv),
        grid=(b, nq),
        in_specs=[pl.BlockSpec((1, T, d), lambda i, j: (i, j, 0)),
                  _resident((1, d)), _resident(w_t.shape), _resident((HEAD_DIM, 1))],
        out_specs=[sp[0] for sp in t_specs],
        out_shape=[sp[1] for sp in t_specs],
        compiler_params=_params(2),
        name="proj_t",
    )(x, mix_norm_g.reshape(1, d), w_t, q_norm_g.reshape(HEAD_DIM, 1))

    n_chunks = s // CMP_STRIDE

    def chunked(a):
        return a.reshape(b, n_chunks, CMP_STRIDE, N_KV_GROUPS, HEAD_DIM).transpose(0, 3, 1, 2, 4) \
                .reshape(b, N_KV_GROUPS, n_chunks, CMP_STRIDE * HEAD_DIM)

    k_cmp = _compress(chunked(kc), cmp_pe_k, cmp_w1_k, cmp_w2_k, k_norm_g, True)
    v_cmp_t = _compress(chunked(vc), cmp_pe_v, cmp_w1_v, cmp_w2_v, k_norm_g, False)

    o_nsa = _nsa(q_t, ks.reshape(b, nq, T, d_kv), vs_t.reshape(b, nq, N_KV_GROUPS, V_ROWS, T),
                 kw.reshape(b, nq, T, d_kv), vw_t.reshape(b, nq, N_KV_GROUPS, V_ROWS, T),
                 k_cmp, v_cmp_t, gates_t, near_bias, cmp_bias, s)

    o_mem = _mem_attention(q_mem, mem, mem_norm_g, w_mem_kv, mem_q_norm_g, mem_k_norm_g)
    x = _merge(x, o_nsa, o_mem, conv_in, merge_g, conv_w, conv_b, w_out)
    x = _ffn(x.reshape(b * s, d), ffn2_norm_g, ffn2_w_in, ffn2_w_out).reshape(b, s, d)
    return x


def kernel(x, mem, ffn1_norm_g, ffn1_w_in, ffn1_w_out, mix_norm_g, w_in, q_norm_g, k_norm_g, cmp_pe_k, cmp_w1_k, cmp_w2_k, cmp_pe_v, cmp_w1_v, cmp_w2_v, conv_w, conv_b, mem_norm_g, w_mem_kv, mem_q_norm_g, mem_k_norm_g, w_out, ffn2_norm_g, ffn2_w_in, ffn2_w_out, rel_bias):
    near_bias, cmp_bias = _bias_tiles(rel_bias)
    for l in range(ffn1_norm_g.shape[0]):
        x = _layer(x, mem, ffn1_norm_g[l], ffn1_w_in[l], ffn1_w_out[l], mix_norm_g[l], w_in[l],
                   q_norm_g[l], k_norm_g[l], cmp_pe_k[l], cmp_w1_k[l], cmp_w2_k[l],
                   cmp_pe_v[l], cmp_w1_v[l], cmp_w2_v[l], conv_w[l], conv_b[l],
                   mem_norm_g[l], w_mem_kv[l], mem_q_norm_g[l], mem_k_norm_g[l], w_out[l],
                   ffn2_norm_g[l], ffn2_w_in[l], ffn2_w_out[l], near_bias, cmp_bias)
    return x
```

```python
import functools
import math

import jax
import jax.numpy as jnp
from jax import lax
from jax.experimental import pallas as pl
from jax.experimental.pallas import tpu as pltpu

N_HEADS = 16
HEAD_DIM = 64
N_KV_GROUPS = 4
HEADS_PER_GROUP = N_HEADS // N_KV_GROUPS
GROUP_WIDTH = HEADS_PER_GROUP * HEAD_DIM
CMP_BLOCK = 32
CMP_STRIDE = 16
SEL_BLOCK = 64
N_SELECT = 16
WINDOW = 512
FORCE = 1e4
CONV_WIDTH = 3
MEM_HEADS = 4
REL_BUCKETS = 32
REL_MAX_DIST = 128
EPS = 1e-6
NEG = -1e30

T = 256
BLK_PER_T = T // SEL_BLOCK
CMP_PER_T = T // CMP_STRIDE
CMP_NEAR = 2 * CMP_PER_T
N_NEAR = WINDOW // T + 1
N_BUF = 4
LONG_TRIP = 2 * N_BUF
CMP_CHUNK = 128
FFN_TM = 512
ROW_TM = 256
MEM_TM = 512
MERGE_TM = 512
HALO = 16
LANES = 128
SUBLANES = 8
GROUPS_PER_PLANE = LANES // HEAD_DIM
BF16_SUBLANES = 16
V_ROWS = HEAD_DIM + BF16_SUBLANES
LOG2E = math.log2(math.e)
VMEM_LIMIT = 52 * 1024 * 1024

F32 = jnp.float32
BF16 = jnp.bfloat16
HI = lax.Precision.HIGHEST


def _dot(a, b):
    return jnp.dot(a, b, preferred_element_type=F32)


def _dot_nt(a, b):
    return lax.dot_general(a, b, (((1,), (1,)), ((), ())), preferred_element_type=F32)


def _rms_rows(xf, g):
    return xf * lax.rsqrt(jnp.mean(xf * xf, axis=-1, keepdims=True) + EPS) * g


def _sigmoid(x):
    return 1.0 / (1.0 + jnp.exp(-x))


def _resident(shape):
    zeros = (0,) * len(shape)
    return pl.BlockSpec(shape, lambda *_: zeros, pipeline_mode=pl.Buffered(1))


def _params(n_axes):
    return pltpu.CompilerParams(dimension_semantics=("arbitrary",) * n_axes,
                                vmem_limit_bytes=VMEM_LIMIT)


def _ffn_kernel(x_ref, g_ref, wa_ref, wb_ref, wo_ref, o_ref, *, ff_chunk):
    x = x_ref[...]
    h = _rms_rows(x, g_ref[...]).astype(BF16)
    d_ff = wa_ref.shape[1]
    acc = jnp.zeros(x.shape, F32)
    for lo in range(0, d_ff, ff_chunk):
        a = _dot(h, wa_ref[:, lo:lo + ff_chunk])
        b = _dot(h, wb_ref[:, lo:lo + ff_chunk])
        z = (a * _sigmoid(a) * b).astype(BF16)
        acc = acc + _dot(z, wo_ref[lo:lo + ff_chunk, :])
    o_ref[...] = x + 0.5 * acc


def _ffn(x2d, g, w_in, w_out):
    n, d = x2d.shape
    d_ff = w_out.shape[0]
    wa = w_in[:, :d_ff].astype(BF16)
    wb = w_in[:, d_ff:].astype(BF16)
    wo = w_out.astype(BF16)
    ff_chunk = d_ff // 2 if (d_ff // 2) % 128 == 0 else d_ff
    return pl.pallas_call(
        functools.partial(_ffn_kernel, ff_chunk=ff_chunk),
        grid=(n // FFN_TM,),
        in_specs=[pl.BlockSpec((FFN_TM, d), lambda i: (i, 0)),
                  _resident((1, d)), _resident((d, d_ff)), _resident((d, d_ff)), _resident((d_ff, d))],
        out_specs=pl.BlockSpec((FFN_TM, d), lambda i: (i, 0)),
        out_shape=jax.ShapeDtypeStruct((n, d), F32),
        compiler_params=_params(1),
        name="ffn",
    )(x2d, g.reshape(1, d), wa, wb, wo)


def _proj_rows_kernel(x_ref, g_ref, w_ref, bd_ref, kg_ref,
                      ks_o, kw_o, kc_o, vc_o, conv_o, qm_o, mg_o, *, widths):
    h = _rms_rows(x_ref[0], g_ref[...]).astype(BF16)

    def knorm(k):
        ms = jnp.dot(k * k, bd_ref[...], precision=HI, preferred_element_type=F32)
        return (k * lax.rsqrt(ms + EPS) * kg_ref[...]).astype(BF16)

    lo = 0
    outs = (ks_o, kw_o, kc_o, vc_o, conv_o, qm_o, mg_o)
    for idx, (o_ref, wd) in enumerate(zip(outs, widths)):
        y = _dot(h, w_ref[:, lo:lo + wd])
        if idx < 2:
            o_ref[0] = knorm(y)
        elif idx < 4:
            for plane in range(wd // LANES):
                o_ref[0, plane] = y[:, plane * LANES:(plane + 1) * LANES]
        else:
            o_ref[0] = y.astype(o_ref.dtype)
        lo += wd


def _proj_t_kernel(x_ref, g_ref, wt_ref, qg_ref, q_o, vs_o, vw_o, gt_o, *, d_q, d_kv):
    h = _rms_rows(x_ref[0], g_ref[...]).astype(BF16)
    qg = qg_ref[...] * (HEAD_DIM ** -0.5 * LOG2E)
    out_t = _dot_nt(wt_ref[...], h)
    for hd in range(d_q // HEAD_DIM):
        q = out_t[hd * HEAD_DIM:(hd + 1) * HEAD_DIM, :]
        qn = q * lax.rsqrt(jnp.mean(q * q, axis=0, keepdims=True) + EPS) * qg
        q_o[0, 0, hd * HEAD_DIM:(hd + 1) * HEAD_DIM, :] = qn.astype(BF16)
    lo = d_q
    for v_o in (vs_o, vw_o):
        y = out_t[lo:lo + d_kv, :].astype(BF16)
        for grp in range(d_kv // HEAD_DIM):
            v_o[0, 0, grp * V_ROWS:grp * V_ROWS + HEAD_DIM, :] = y[grp * HEAD_DIM:(grp + 1) * HEAD_DIM]
            v_o[0, 0, grp * V_ROWS + HEAD_DIM:(grp + 1) * V_ROWS, :] = jnp.ones((V_ROWS - HEAD_DIM, T), BF16)
        lo += d_kv
    gt_o[0, 0] = _sigmoid(out_t[lo:, :])


def _compress_kernel(c_ref, pe_ref, w1_ref, w2_ref, kg_ref, o_ref, *, is_key, hidden):
    n_chunks = c_ref.shape[2] // CMP_STRIDE
    a = jnp.zeros((n_chunks, GROUPS_PER_PLANE * hidden), F32)
    b = jnp.zeros((n_chunks, GROUPS_PER_PLANE * hidden), F32)
    for pos in range(CMP_STRIDE):
        x = c_ref[0, 0, pl.ds(pos, n_chunks, stride=CMP_STRIDE), :]
        a = a + _dot((x + pe_ref[pos:pos + 1, :]).astype(BF16), w1_ref[pos])
        b = b + _dot((x + pe_ref[CMP_STRIDE + pos:CMP_STRIDE + pos + 1, :]).astype(BF16), w1_ref[CMP_STRIDE + pos])
    hid = a + pltpu.roll(b, n_chunks - 1, 0)
    hid = (hid * _sigmoid(hid)).astype(BF16)
    for grp in range(GROUPS_PER_PLANE):
        hid_g = hid[:, grp * hidden:(grp + 1) * hidden]
        if is_key:
            y = _dot(hid_g, w2_ref[...])
            y = _rms_rows(y, kg_ref[...])
            row = lax.broadcasted_iota(jnp.int32, y.shape, 0)
            o_ref[0, grp] = jnp.where(row < n_chunks - 1, y, 0.0).astype(BF16)
        else:
            y = _dot_nt(w2_ref[...], hid_g)
            col = lax.broadcasted_iota(jnp.int32, y.shape, 1)
            o_ref[0, grp] = jnp.where(col < n_chunks - 1, y, 0.0).astype(BF16)


def _compress(c_planes, pe, w1, w2, k_gain, is_key):
    b, planes, s, _ = c_planes.shape
    n_chunks = s // CMP_STRIDE
    hidden = w1.shape[1]
    w1_pos = w1.reshape(CMP_BLOCK, HEAD_DIM, hidden)
    zeros = jnp.zeros_like(w1_pos)
    w1_bd = jnp.concatenate([jnp.concatenate([w1_pos, zeros], axis=2),
                             jnp.concatenate([zeros, w1_pos], axis=2)], axis=1).astype(BF16)
    pe_planes = jnp.tile(pe, (1, GROUPS_PER_PLANE))
    w2_arg = w2.astype(BF16) if is_key else w2.T.astype(BF16)
    out_tail = (n_chunks, HEAD_DIM) if is_key else (HEAD_DIM, n_chunks)
    return pl.pallas_call(
        functools.partial(_compress_kernel, is_key=is_key, hidden=hidden),
        grid=(b, planes),
        in_specs=[pl.BlockSpec((1, 1, s, LANES), lambda i, j: (i, j, 0, 0)),
                  _resident(pe_planes.shape), _resident(w1_bd.shape),
                  _resident(w2_arg.shape), _resident((1, HEAD_DIM))],
        out_specs=pl.BlockSpec((1, GROUPS_PER_PLANE) + out_tail, lambda i, j: (i, j, 0, 0)),
        out_shape=jax.ShapeDtypeStruct((b, planes * GROUPS_PER_PLANE) + out_tail, BF16),
        compiler_params=_params(2),
        name="compress_k" if is_key else "compress_v",
    )(c_planes, pe_planes, w1_bd, w2_arg, k_gain.reshape(1, HEAD_DIM))


def _bias_kernel(rb_ref, bkt_near_ref, bkt_cmp_ref, near_o, cmp_o):
    h = pl.program_id(0)
    far = rb_ref[REL_BUCKETS - 1, h]

    def lookup(bkt):
        out = jnp.zeros(bkt.shape, F32)
        for k in range(REL_BUCKETS - 1):
            out = jnp.where(bkt == k, (rb_ref[k, h] - far) * LOG2E, out)
        return out

    key = lax.broadcasted_iota(jnp.int32, (T, T), 0)
    qry = lax.broadcasted_iota(jnp.int32, (T, T), 1)
    near_o[0, 0] = jnp.where(qry >= key, lookup(bkt_near_ref[0]), NEG)
    near_o[0, 1] = lookup(bkt_near_ref[1])
    near_o[0, 2] = jnp.where(key > qry, 0.0, NEG)
    cmp_o[0] = lookup(bkt_cmp_ref[...])


def _rel_bucket(dist):
    n = jnp.maximum(dist, 0)
    max_exact = REL_BUCKETS // 2
    nf = jnp.maximum(n, 1).astype(F32)
    large = max_exact + (jnp.log(nf / max_exact) / math.log(REL_MAX_DIST / max_exact)
                         * (REL_BUCKETS - max_exact)).astype(jnp.int32)
    large = jnp.minimum(large, REL_BUCKETS - 1)
    return jnp.where(n < max_exact, n, large)


def _bias_tiles(rel_bias):
    key = jnp.arange(T)[:, None]
    qry = jnp.arange(T)[None, :]
    bkt_near = jnp.stack([_rel_bucket(qry - key), _rel_bucket(qry - key + T)]).astype(jnp.int32)
    j = jnp.arange(CMP_NEAR)[:, None]
    bkt_cmp = _rel_bucket(qry - CMP_STRIDE * (j - CMP_PER_T) - (CMP_BLOCK - 1)).astype(jnp.int32)
    return pl.pallas_call(
        _bias_kernel,
        grid=(N_HEADS,),
        in_specs=[pl.BlockSpec(memory_space=pltpu.SMEM),
                  pl.BlockSpec((2, T, T), lambda h: (0, 0, 0)),
                  pl.BlockSpec((CMP_NEAR, T), lambda h: (0, 0))],
        out_specs=[pl.BlockSpec((1, N_NEAR, T, T), lambda h: (h // HEADS_PER_GROUP, 0, 0, h % HEADS_PER_GROUP)),
                   pl.BlockSpec((1, CMP_NEAR, T), lambda h: (h // HEADS_PER_GROUP, 0, h % HEADS_PER_GROUP))],
        out_shape=[jax.ShapeDtypeStruct((N_KV_GROUPS, N_NEAR, T, HEADS_PER_GROUP * T), F32),
                   jax.ShapeDtypeStruct((N_KV_GROUPS, CMP_NEAR, HEADS_PER_GROUP * T), F32)],
        compiler_params=_params(1),
        name="bias_tiles",
    )(rel_bias, bkt_near, bkt_cmp)


def _nsa_kernel(q_ref, ks_ref, vs_ref, kw_ref, vw_ref, kc_ref, vc_ref, gt_ref, nb_ref, cb_ref,
                o_ref,
                qcat_ref, sc_ref, psum_ref, selneg_ref, oc_ref,
                s0_ref, s1_ref, s2_ref, s3_ref, p0_ref, p1_ref, p2_ref, p3_ref, w0_ref, w1_ref, w2_ref,
                ms_ref, accs_ref, mw_ref, accw_ref, *, n_blk, n_cmp):
    g = pl.program_id(1)
    qi = pl.program_id(2)
    R = HEADS_PER_GROUP
    W = R * T
    qry = lax.broadcasted_iota(jnp.int32, (1, W), 1) & (T - 1)
    t = qi * T + qry

    def q_head(r):
        return q_ref[0, 0, r * HEAD_DIM:(r + 1) * HEAD_DIM, :]

    rowgrp = lax.shift_right_logical(lax.broadcasted_iota(jnp.int32, (GROUP_WIDTH, T), 0),
                                     int(math.log2(HEAD_DIM)))
    for r in range(R):
        q4 = jnp.concatenate([q_head(r).astype(F32)] * N_KV_GROUPS, axis=0)
        qcat_ref[:, r * T:(r + 1) * T] = jnp.where(rowgrp == g, q4, 0.0).astype(BF16)

    for m_ref, acc_ref in ((ms_ref, accs_ref), (mw_ref, accw_ref)):
        m_ref[...] = jnp.full(m_ref.shape, NEG, F32)
        acc_ref[...] = jnp.zeros(acc_ref.shape, F32)

    def softmax_tile(s_ref, p_ref, m_ref, rows):
        top = None
        for j in range(BLK_PER_T):
            blk_max = s_ref[j * SEL_BLOCK:(j + 1) * SEL_BLOCK, :].reshape(SEL_BLOCK // SUBLANES, SUBLANES, W).max(axis=0)
            if rows is not None:
                blk_max = blk_max + rows[j]
            top = blk_max if top is None else jnp.maximum(top, blk_max)
        m_old = m_ref[...]
        m_new = jnp.maximum(m_old, jnp.max(top, axis=0, keepdims=True))
        for j in range(BLK_PER_T):
            shift = m_new if rows is None else m_new - rows[j]
            sl = slice(j * SEL_BLOCK, (j + 1) * SEL_BLOCK)
            p_ref[sl, :] = jnp.exp2(s_ref[sl, :] - shift).astype(BF16)
        m_ref[...] = m_new
        return jnp.exp2(m_old - m_new)

    s_bufs = (s0_ref, s1_ref, s2_ref, s3_ref)
    p_bufs = (p0_ref, p1_ref, p2_ref, p3_ref)
    w_bufs = (w0_ref, w1_ref, w2_ref)
    neg_row = jnp.full((1, W), NEG, F32)

    def tile_of(y):
        return jnp.where(y == 0, qi, jnp.where(y == 1, jnp.maximum(qi - 1, 0), jnp.clip(y - 2, 0, qi)))

    win_tiles = [jnp.maximum(qi - back, 0) for back in range(N_NEAR)]
    for back in range(N_NEAR):
        w_bufs[back][...] = _dot(kw_ref[0, win_tiles[back]], qcat_ref[...]) + nb_ref[0, back]
    near0 = pl.multiple_of(qi * CMP_PER_T, CMP_PER_T)
    sc_ref[0:CMP_PER_T, :] = jnp.zeros((CMP_PER_T, W), F32)
    sc_ref[CMP_PER_T:, :] = _dot(kc_ref[0, 0], jnp.concatenate([q_head(r) for r in range(R)], axis=1))
    sc_ref[pl.ds(near0, CMP_NEAR), :] = sc_ref[pl.ds(near0, CMP_NEAR), :] + cb_ref[0]
    s0_ref[...] = _dot(ks_ref[0, tile_of(0)], qcat_ref[...]) + nb_ref[0, 0]
    s1_ref[...] = _dot(ks_ref[0, tile_of(1)], qcat_ref[...]) + nb_ref[0, 1]

    for back in range(N_NEAR):
        rows = None if back == 0 else [jnp.where(qi >= back, 0.0, neg_row)] * BLK_PER_T
        alpha = softmax_tile(w_bufs[back], p_bufs[back], mw_ref, rows)
        accw_ref[...] = alpha * accw_ref[...] + _dot(vw_ref[0, win_tiles[back], 0], p_bufs[back][...])

    def select_blocks(nc, nb):
        c_idx = lax.broadcasted_iota(jnp.int32, (nc, W), 0)
        valid_c = (c_idx * CMP_STRIDE + (CMP_BLOCK - 1) <= t) & (c_idx < n_cmp - 1)
        s = jnp.where(valid_c, sc_ref[CMP_PER_T:CMP_PER_T + nc, :], NEG)
        m = jnp.max(s, axis=0, keepdims=True)
        p = jnp.where(valid_c, jnp.exp2(s - m), 0.0)
        l = jnp.sum(p, axis=0, keepdims=True)
        p = p * jnp.where(l > 0.0, 1.0 / jnp.where(l > 0.0, l, 1.0), 0.0)
        oc_ref[...] = _dot(vc_ref[0, 0, :, 0:nc], p.astype(BF16))
        psum = p[:, 0:T]
        for r in range(1, R):
            psum = psum + p[:, r * T:(r + 1) * T]

        for ln in range(T // LANES):
            psum_ref[ln, 0:8, :] = jnp.zeros((8, LANES), F32)
            psum_ref[ln, 8:8 + nc, :] = psum[:, ln * LANES:(ln + 1) * LANES]

        def every_fourth(off):
            return jnp.concatenate([psum_ref[ln, pl.ds(8 + off, nb, stride=SEL_BLOCK // CMP_STRIDE), :]
                                    for ln in range(T // LANES)], axis=1)

        imp = every_fourth(0) + every_fourth(1) + every_fourth(2) + 0.5 * (every_fourth(3) + every_fourth(-1))
        blk = lax.broadcasted_iota(jnp.int32, (nb, T), 0)
        cur = lax.shift_right_logical(t[:, 0:T], int(math.log2(SEL_BLOCK)))
        forced = (blk == 0) | (blk == cur) | (blk == cur - 1)
        score = jnp.where(blk <= cur, imp + jnp.where(forced, FORCE, 0.0), -FORCE)
        for _ in range(min(N_SELECT, n_blk)):
            best = jnp.max(score, axis=0, keepdims=True)
            first = jnp.min(jnp.where(score == best, blk, nb), axis=0, keepdims=True)
            score = jnp.where(blk == first, -jnp.inf, score)
        selneg = jnp.where(score == -jnp.inf, 0.0, NEG)
        selneg_ref[0:nb, :] = jnp.concatenate([selneg] * R, axis=1)

    nq = n_cmp // CMP_PER_T
    n_var = max(1, n_cmp // CMP_CHUNK)
    tiles_per_var = nq // n_var
    for v in range(n_var):
        @pl.when((qi >= v * tiles_per_var) & (qi < (v + 1) * tiles_per_var))
        def _(v=v):
            n_tiles = (v + 1) * tiles_per_var
            select_blocks(n_tiles * CMP_PER_T, n_tiles * BLK_PER_T)

    def sel_rows(y):
        kj = tile_of(y)
        return [selneg_ref[pl.ds(kj * BLK_PER_T + j, 1), :] for j in range(BLK_PER_T)]

    p_bufs[N_BUF - 1][...] = jnp.zeros(p0_ref.shape, BF16)

    def positions(first, count, final=False):
        for k in range(count):
            y = first + k
            pv = _dot(vs_ref[0, tile_of(y - 1), 0], p_bufs[(k - 1) % N_BUF][...])
            if not final or k + 2 < count:
                s_bufs[(k + 2) % N_BUF][...] = _dot(ks_ref[0, tile_of(y + 2)], qcat_ref[...])
            alpha = softmax_tile(s_bufs[k % N_BUF], p_bufs[k % N_BUF], ms_ref, sel_rows(y))
            accs_ref[...] = alpha * (accs_ref[...] + pv)

    n_long = lax.div(qi, LONG_TRIP)
    n_short = lax.div(qi - n_long * LONG_TRIP, N_BUF)

    def long_trip(i, carry):
        positions(LONG_TRIP * i, LONG_TRIP)
        return carry

    def short_trip(i, carry):
        positions(LONG_TRIP * n_long + N_BUF * i, N_BUF)
        return carry

    lax.fori_loop(0, n_long, long_trip, 0)
    lax.fori_loop(0, n_short, short_trip, 0)
    final_first = LONG_TRIP * n_long + N_BUF * n_short
    for count in range(1, N_BUF + 1):
        @pl.when(qi + 1 - final_first == count)
        def _(count=count):
            positions(final_first, count, final=True)
            accs_ref[...] = accs_ref[...] + _dot(vs_ref[0, tile_of(final_first + count - 1), 0],
                                                 p_bufs[count - 1][...])

    o_s = accs_ref[0:HEAD_DIM, :] / accs_ref[HEAD_DIM:HEAD_DIM + 1, :]
    o_w = accw_ref[0:HEAD_DIM, :] / accw_ref[HEAD_DIM:HEAD_DIM + 1, :]
    outs = []
    for r in range(R):
        head = g * R + r
        sl = slice(r * T, (r + 1) * T)
        g_c = gt_ref[0, 0, pl.ds(head, 1), :]
        g_s = gt_ref[0, 0, pl.ds(N_HEADS + head, 1), :]
        g_w = gt_ref[0, 0, pl.ds(2 * N_HEADS + head, 1), :]
        outs.append(g_c * oc_ref[:, sl] + g_s * o_s[:, sl] + g_w * o_w[:, sl])
    o_ref[0] = jnp.concatenate(outs, axis=0).T.astype(o_ref.dtype)


def _nsa(q_t, ks, vs_t, kw, vw_t, kc, vc_t, gates_t, near_bias, cmp_bias, seq):
    b, nq = q_t.shape[0], q_t.shape[1]
    n_blk = seq // SEL_BLOCK
    n_cmp = seq // CMP_STRIDE
    W = HEADS_PER_GROUP * T
    kv_spec = pl.BlockSpec((1, nq, T, GROUP_WIDTH), lambda i, j, k: (i, 0, 0, 0))
    vt_spec = pl.BlockSpec((1, nq, 1, V_ROWS, T), lambda i, j, k: (i, 0, j, 0, 0))
    stat = pltpu.VMEM((1, W), F32)
    acc = pltpu.VMEM((V_ROWS, W), F32)
    scores = pltpu.VMEM((T, W), F32)
    probs = pltpu.VMEM((T, W), BF16)
    return pl.pallas_call(
        functools.partial(_nsa_kernel, n_blk=n_blk, n_cmp=n_cmp),
        grid=(b, N_KV_GROUPS, nq),
        in_specs=[pl.BlockSpec((1, 1, GROUP_WIDTH, T), lambda i, j, k: (i, k, j, 0)),
                  kv_spec, vt_spec, kv_spec, vt_spec,
                  pl.BlockSpec((1, 1, n_cmp, HEAD_DIM), lambda i, j, k: (i, j, 0, 0)),
                  pl.BlockSpec((1, 1, HEAD_DIM, n_cmp), lambda i, j, k: (i, j, 0, 0)),
                  pl.BlockSpec((1, 1, gates_t.shape[2], T), lambda i, j, k: (i, k, 0, 0)),
                  pl.BlockSpec((1, N_NEAR, T, W), lambda i, j, k: (j, 0, 0, 0)),
                  pl.BlockSpec((1, CMP_NEAR, W), lambda i, j, k: (j, 0, 0))],
        out_specs=pl.BlockSpec((1, T, GROUP_WIDTH), lambda i, j, k: (i, k, j)),
        out_shape=jax.ShapeDtypeStruct((b, seq, N_HEADS * HEAD_DIM), BF16),
        scratch_shapes=[pltpu.VMEM((GROUP_WIDTH, W), BF16),
                        pltpu.VMEM((n_cmp + CMP_PER_T, W), F32),
                        pltpu.VMEM((T // LANES, n_cmp + 8, LANES), F32),
                        pltpu.VMEM((n_blk, W), F32),
                        pltpu.VMEM((HEAD_DIM, W), F32), *([scores] * N_BUF), *([probs] * N_BUF), *([scores] * N_NEAR),
                        stat, acc, stat, acc],
        compiler_params=_params(3),
        name="nsa",
    )(q_t, ks, vs_t, kw, vw_t, kc, vc_t, gates_t, near_bias, cmp_bias)


def _mem_kv_kernel(mem_ref, g_ref, w_ref, kg_ref, k_o, v_o):
    h = _rms_rows(mem_ref[0], g_ref[...]).astype(BF16)
    width = k_o.shape[2]
    hd = width // MEM_HEADS
    k = _dot(h, w_ref[:, :width])
    for i in range(MEM_HEADS):
        k_o[0, :, i * hd:(i + 1) * hd] = _rms_rows(k[:, i * hd:(i + 1) * hd], kg_ref[...]).astype(BF16)
    v_o[0] = _dot(h, w_ref[:, width:]).astype(BF16)


def _mem_attn_kernel(q_ref, k_ref, v_ref, qg_ref, o_ref):
    width = q_ref.shape[2]
    hd = width // MEM_HEADS
    for i in range(MEM_HEADS):
        sl = slice(i * hd, (i + 1) * hd)
        q = (_rms_rows(q_ref[0, :, sl].astype(F32), qg_ref[...]) * (hd ** -0.5)).astype(BF16)
        s = _dot_nt(q, k_ref[0, :, sl])
        p = jnp.exp(s - jnp.max(s, axis=-1, keepdims=True))
        p = p / jnp.sum(p, axis=-1, keepdims=True)
        o_ref[0, :, sl] = _dot(p.astype(BF16), v_ref[0, :, sl]).astype(o_ref.dtype)


def _mem_attention(q_mem, mem, mem_norm_g, w_mem_kv, q_g, k_g):
    b, s, width = q_mem.shape
    n_mem, d = mem.shape[1], mem.shape[2]
    hd = width // MEM_HEADS
    kv_shape = jax.ShapeDtypeStruct((b, n_mem, width), BF16)
    kv_block = pl.BlockSpec((1, n_mem, width), lambda i: (i, 0, 0))
    km, vm = pl.pallas_call(
        _mem_kv_kernel,
        grid=(b,),
        in_specs=[pl.BlockSpec((1, n_mem, d), lambda i: (i, 0, 0)),
                  _resident((1, d)), _resident((d, 2 * width)), _resident((1, hd))],
        out_specs=[kv_block, kv_block],
        out_shape=[kv_shape, kv_shape],
        compiler_params=_params(1),
        name="mem_kv",
    )(mem, mem_norm_g.reshape(1, d), w_mem_kv.astype(BF16), k_g.reshape(1, hd))
    kv_block2 = pl.BlockSpec((1, n_mem, width), lambda i, j: (i, 0, 0))
    return pl.pallas_call(
        _mem_attn_kernel,
        grid=(b, s // MEM_TM),
        in_specs=[pl.BlockSpec((1, MEM_TM, width), lambda i, j: (i, j, 0)),
                  kv_block2, kv_block2, _resident((1, hd))],
        out_specs=pl.BlockSpec((1, MEM_TM, width), lambda i, j: (i, j, 0)),
        out_shape=jax.ShapeDtypeStruct((b, s, width), BF16),
        compiler_params=_params(2),
        name="mem_attn",
    )(q_mem, km, vm, q_g.reshape(1, hd))


def _merge_kernel(x_ref, nsa_ref, mem_ref, cb_ref, cc_ref, cx_ref, hc_ref, hx_ref,
                  g1_ref, g2_ref, g3_ref, cw_ref, bias_ref, wo_ref, o_ref):
    j = pl.program_id(1)

    def f32(ref):
        return ref[0].astype(F32)

    u = f32(cc_ref) * f32(cx_ref)
    halo = jnp.where(j > 0, f32(hc_ref) * f32(hx_ref), 0.0)
    prev1 = halo[HALO - 1:HALO, :]
    prev2 = halo[HALO - 2:HALO - 1, :]
    row = lax.broadcasted_iota(jnp.int32, u.shape, 0)
    u1 = jnp.where(row == 0, prev1, pltpu.roll(u, 1, 0))
    u2 = jnp.where(row == 0, prev2, jnp.where(row == 1, prev1, pltpu.roll(u, 2, 0)))
    y = cw_ref[0:1, :] * u2 + cw_ref[1:2, :] * u1 + cw_ref[2:3, :] * u
    o_conv = f32(cb_ref) * (y + bias_ref[...])
    merged = (_sigmoid(f32(g1_ref)) * f32(nsa_ref) + _sigmoid(f32(g2_ref)) * o_conv
              + _sigmoid(f32(g3_ref)) * f32(mem_ref))
    o_ref[0] = x_ref[0] + _dot(merged.astype(BF16), wo_ref[...])


def _merge(x, o_nsa, o_mem, conv_in, merge_g, conv_w, conv_b, w_out):
    b, s, d = x.shape
    tm = MERGE_TM

    def col(c):
        return pl.BlockSpec((1, tm, d), lambda i, j: (i, j, c))

    def halo(c):
        return pl.BlockSpec((1, HALO, d), lambda i, j: (i, jnp.maximum(j * (tm // HALO) - 1, 0), c))

    return pl.pallas_call(
        _merge_kernel,
        grid=(b, s // tm),
        in_specs=[col(0), col(0), col(0),
                  col(0), col(1), col(2), halo(1), halo(2),
                  col(0), col(1), col(2),
                  _resident((CONV_WIDTH, d)), _resident((1, d)), _resident((d, d))],
        out_specs=col(0),
        out_shape=jax.ShapeDtypeStruct((b, s, d), F32),
        compiler_params=_params(2),
        name="merge",
    )(x, o_nsa, o_mem, conv_in, conv_in, conv_in, conv_in, conv_in,
      merge_g, merge_g, merge_g, conv_w, conv_b.reshape(1, d), w_out.astype(BF16))


def _layer(x, mem, ffn1_norm_g, ffn1_w_in, ffn1_w_out, mix_norm_g, w_in, q_norm_g, k_norm_g,
           cmp_pe_k, cmp_w1_k, cmp_w2_k, cmp_pe_v, cmp_w1_v, cmp_w2_v, conv_w, conv_b,
           mem_norm_g, w_mem_kv, mem_q_norm_g, mem_k_norm_g, w_out,
           ffn2_norm_g, ffn2_w_in, ffn2_w_out, near_bias, cmp_bias):
    b, s, d = x.shape
    assert s % T == 0 and s % MERGE_TM == 0 and (b * s) % FFN_TM == 0
    assert WINDOW == 2 * T and REL_MAX_DIST <= T // 2
    assert SEL_BLOCK == 4 * CMP_STRIDE and CMP_BLOCK == 2 * CMP_STRIDE
    nq = s // T
    d_q = N_HEADS * HEAD_DIM
    d_kv = N_KV_GROUPS * HEAD_DIM
    d_conv = conv_w.shape[1]
    d_mem = w_mem_kv.shape[1] // 2

    x = _ffn(x.reshape(b * s, d), ffn1_norm_g, ffn1_w_in, ffn1_w_out).reshape(b, s, d)

    o = 0
    w_q = w_in[:, o:o + d_q]; o += d_q
    w_kc, w_vc, w_ks, w_vs, w_kw, w_vw = [w_in[:, o + i * d_kv:o + (i + 1) * d_kv] for i in range(6)]
    o += 6 * d_kv
    w_g = w_in[:, o:o + 3 * N_HEADS]; o += 3 * N_HEADS
    w_conv = w_in[:, o:o + 3 * d_conv]; o += 3 * d_conv
    w_qm = w_in[:, o:o + d_mem]; o += d_mem
    w_mg = w_in[:, o:]

    w_rows = jnp.concatenate([w_ks, w_kw, w_kc, w_vc, w_conv, w_qm, w_mg], axis=1).astype(BF16)
    widths = (d_kv, d_kv, d_kv, d_kv, 3 * d_conv, d_mem, w_mg.shape[1])
    group_of = jnp.arange(d_kv) // HEAD_DIM
    block_diag = (group_of[:, None] == group_of[None, :]).astype(F32) / HEAD_DIM
    k_gain_row = jnp.tile(k_norm_g, N_KV_GROUPS).reshape(1, d_kv)

    def rows_out(wd, dt):
        return (pl.BlockSpec((1, ROW_TM, wd), lambda i, j: (i, j, 0)), jax.ShapeDtypeStruct((b, s, wd), dt))

    def planes_out(wd):
        n_planes = wd // LANES
        return (pl.BlockSpec((1, n_planes, ROW_TM, LANES), lambda i, j: (i, 0, j, 0)),
                jax.ShapeDtypeStruct((b, n_planes, s, LANES), F32))

    specs = [planes_out(wd) if i in (2, 3) else rows_out(wd, BF16) for i, wd in enumerate(widths)]
    ks, kw, kc, vc, conv_in, q_mem, merge_g = pl.pallas_call(
        functools.partial(_proj_rows_kernel, widths=widths),
        grid=(b, s // ROW_TM),
        in_specs=[pl.BlockSpec((1, ROW_TM, d), lambda i, j: (i, j, 0)),
                  _resident((1, d)), _resident(w_rows.shape), _resident((d_kv, d_kv)), _resident((1, d_kv))],
        out_specs=[sp[0] for sp in specs],
        out_shape=[sp[1] for sp in specs],
        compiler_params=_params(2),
        name="proj_rows",
    )(x, mix_norm_g.reshape(1, d), w_rows, block_diag, k_gain_row)

    w_g_t = w_g.reshape(d, N_HEADS, 3).transpose(2, 1, 0).reshape(3 * N_HEADS, d)
    n_gate_rows = 128
    w_g_t = jnp.pad(w_g_t, ((0, n_gate_rows - 3 * N_HEADS), (0, 0)))
    w_t = jnp.concatenate([w_q.T, w_vs.T, w_vw.T, w_g_t], axis=0).astype(BF16)

    def t_out(rows, dt):
        return (pl.BlockSpec((1, 1, rows, T), lambda i, j: (i, j, 0, 0)),
                jax.ShapeDtypeStruct((b, nq, rows, T), dt))

    v_rows = N_KV_GROUPS * V_ROWS
    t_specs = [t_out(d_q, BF16), t_out(v_rows, BF16), t_out(v_rows, BF16), t_out(n_gate_rows, F32)]
    q_t, vs_t, vw_t, gates_t = pl.pallas_call(
        functools.partial(_proj_t_kernel, d_q=d_q, d_kv=d_kv),
        grid=(b, nq),
        in_specs=[pl.BlockSpec((1, T, d), lambda i, j: (i, j, 0)),
                  _resident((1, d)), _resident(w_t.shape), _resident((HEAD_DIM, 1))],
        out_specs=[sp[0] for sp in t_specs],
        out_shape=[sp[1] for sp in t_specs],
        compiler_params=_params(2),
        name="proj_t",
    )(x, mix_norm_g.reshape(1, d), w_t, q_norm_g.reshape(HEAD_DIM, 1))

    k_cmp = _compress(kc, cmp_pe_k, cmp_w1_k, cmp_w2_k, k_norm_g, True)
    v_cmp_t = _compress(vc, cmp_pe_v, cmp_w1_v, cmp_w2_v, k_norm_g, False)

    o_nsa = _nsa(q_t, ks.reshape(b, nq, T, d_kv), vs_t.reshape(b, nq, N_KV_GROUPS, V_ROWS, T),
                 kw.reshape(b, nq, T, d_kv), vw_t.reshape(b, nq, N_KV_GROUPS, V_ROWS, T),
                 k_cmp, v_cmp_t, gates_t, near_bias, cmp_bias, s)

    o_mem = _mem_attention(q_mem, mem, mem_norm_g, w_mem_kv, mem_q_norm_g, mem_k_norm_g)
    x = _merge(x, o_nsa, o_mem, conv_in, merge_g, conv_w, conv_b, w_out)
    x = _ffn(x.reshape(b * s, d), ffn2_norm_g, ffn2_w_in, ffn2_w_out).reshape(b, s, d)
    return x


def kernel(x, mem, ffn1_norm_g, ffn1_w_in, ffn1_w_out, mix_norm_g, w_in, q_norm_g, k_norm_g, cmp_pe_k, cmp_w1_k, cmp_w2_k, cmp_pe_v, cmp_w1_v, cmp_w2_v, conv_w, conv_b, mem_norm_g, w_mem_kv, mem_q_norm_g, mem_k_norm_g, w_out, ffn2_norm_g, ffn2_w_in, ffn2_w_out, rel_bias):
    near_bias, cmp_bias = _bias_tiles(rel_bias)
    for l in range(ffn1_norm_g.shape[0]):
        x = _layer(x, mem, ffn1_norm_g[l], ffn1_w_in[l], ffn1_w_out[l], mix_norm_g[l], w_in[l],
                   q_norm_g[l], k_norm_g[l], cmp_pe_k[l], cmp_w1_k[l], cmp_w2_k[l],
                   cmp_pe_v[l], cmp_w1_v[l], cmp_w2_v[l], conv_w[l], conv_b[l],
                   mem_norm_g[l], w_mem_kv[l], mem_q_norm_g[l], mem_k_norm_g[l], w_out[l],
                   ffn2_norm_g[l], ffn2_w_in[l], ffn2_w_out[l], near_bias, cmp_bias)
    return x
```

```python
import functools
import math

import jax
import jax.numpy as jnp
from jax import lax
from jax.experimental import pallas as pl
from jax.experimental.pallas import tpu as pltpu

N_HEADS = 16
HEAD_DIM = 64
N_KV_GROUPS = 4
HEADS_PER_GROUP = N_HEADS // N_KV_GROUPS
GROUP_WIDTH = HEADS_PER_GROUP * HEAD_DIM
CMP_BLOCK = 32
CMP_STRIDE = 16
SEL_BLOCK = 64
N_SELECT = 16
WINDOW = 512
FORCE = 1e4
CONV_WIDTH = 3
MEM_HEADS = 4
REL_BUCKETS = 32
REL_MAX_DIST = 128
EPS = 1e-6
NEG = -1e30

T = 256
BLK_PER_T = T // SEL_BLOCK
CMP_PER_T = T // CMP_STRIDE
CMP_NEAR = 2 * CMP_PER_T
N_NEAR = WINDOW // T + 1
N_BUF = 6
PREFETCH = N_BUF // 2
LONG_TRIP = 2 * N_BUF
CMP_CHUNK = 128
FFN_TM = 512
ROW_TM = 256
MEM_TM = 512
MERGE_TM = 512
HALO = 16
LANES = 128
MXU_COLS = 256
SUBLANES = 8
GROUPS_PER_PLANE = LANES // HEAD_DIM
BF16_SUBLANES = 16
V_ROWS = HEAD_DIM + BF16_SUBLANES
LOG2E = math.log2(math.e)
VMEM_LIMIT = 52 * 1024 * 1024

F32 = jnp.float32
BF16 = jnp.bfloat16
HI = lax.Precision.HIGHEST


def _dot(a, b):
    return jnp.dot(a, b, preferred_element_type=F32)


def _dot_nt(a, b):
    return lax.dot_general(a, b, (((1,), (1,)), ((), ())), preferred_element_type=F32)


def _rms_rows(xf, g):
    return xf * lax.rsqrt(jnp.mean(xf * xf, axis=-1, keepdims=True) + EPS) * g


def _sigmoid(x):
    return 1.0 / (1.0 + jnp.exp(-x))


def _resident(shape):
    zeros = (0,) * len(shape)
    return pl.BlockSpec(shape, lambda *_: zeros, pipeline_mode=pl.Buffered(1))


def _params(n_axes):
    return pltpu.CompilerParams(dimension_semantics=("arbitrary",) * n_axes,
                                vmem_limit_bytes=VMEM_LIMIT)


def _ffn_kernel(x_ref, g_ref, wa_ref, wb_ref, wo_ref, o_ref, *, ff_bounds):
    x = x_ref[...]
    h = _rms_rows(x, g_ref[...]).astype(BF16)
    acc = jnp.zeros(x.shape, F32)
    for lo, hi in zip(ff_bounds[:-1], ff_bounds[1:]):
        a = _dot(h, wa_ref[:, lo:hi])
        b = _dot(h, wb_ref[:, lo:hi])
        z = (a * _sigmoid(a) * b).astype(BF16)
        acc = acc + _dot(z, wo_ref[lo:hi, :])
    o_ref[...] = x + 0.5 * acc


def _ffn(x2d, g, w_in, w_out):
    n, d = x2d.shape
    d_ff = w_out.shape[0]
    wa = w_in[:, :d_ff].astype(BF16)
    wb = w_in[:, d_ff:].astype(BF16)
    wo = w_out.astype(BF16)
    n_col_tiles = -(-d_ff // MXU_COLS)
    ff_bounds = (0, min(d_ff, (n_col_tiles + 1) // 2 * MXU_COLS), d_ff)
    return pl.pallas_call(
        functools.partial(_ffn_kernel, ff_bounds=ff_bounds),
        grid=(n // FFN_TM,),
        in_specs=[pl.BlockSpec((FFN_TM, d), lambda i: (i, 0)),
                  _resident((1, d)), _resident((d, d_ff)), _resident((d, d_ff)), _resident((d_ff, d))],
        out_specs=pl.BlockSpec((FFN_TM, d), lambda i: (i, 0)),
        out_shape=jax.ShapeDtypeStruct((n, d), F32),
        compiler_params=_params(1),
        name="ffn",
    )(x2d, g.reshape(1, d), wa, wb, wo)


def _proj_rows_kernel(x_ref, g_ref, w_ref, bd_ref, kg_ref,
                      ks_o, kw_o, kc_o, vc_o, conv_o, qm_o, mg_o, *, widths):
    h = _rms_rows(x_ref[0], g_ref[...]).astype(BF16)

    def knorm(k):
        ms = jnp.dot(k * k, bd_ref[...], precision=HI, preferred_element_type=F32)
        return (k * lax.rsqrt(ms + EPS) * kg_ref[...]).astype(BF16)

    lo = 0
    outs = (ks_o, kw_o, kc_o, vc_o, conv_o, qm_o, mg_o)
    for idx, (o_ref, wd) in enumerate(zip(outs, widths)):
        y = _dot(h, w_ref[:, lo:lo + wd])
        if idx < 2:
            o_ref[0] = knorm(y)
        elif idx < 4:
            for plane in range(wd // LANES):
                o_ref[0, plane] = y[:, plane * LANES:(plane + 1) * LANES]
        else:
            o_ref[0] = y.astype(o_ref.dtype)
        lo += wd


def _proj_t_kernel(x_ref, g_ref, wt_ref, qg_ref, q_o, vs_o, vw_o, gt_o, *, d_q, d_kv):
    h = _rms_rows(x_ref[0], g_ref[...]).astype(BF16)
    qg = qg_ref[...] * (HEAD_DIM ** -0.5 * LOG2E)
    out_t = _dot_nt(wt_ref[...], h)
    for hd in range(d_q // HEAD_DIM):
        q = out_t[hd * HEAD_DIM:(hd + 1) * HEAD_DIM, :]
        qn = q * lax.rsqrt(jnp.mean(q * q, axis=0, keepdims=True) + EPS) * qg
        q_o[0, 0, hd * HEAD_DIM:(hd + 1) * HEAD_DIM, :] = qn.astype(BF16)
    lo = d_q
    for v_o in (vs_o, vw_o):
        y = out_t[lo:lo + d_kv, :].astype(BF16)
        for grp in range(d_kv // HEAD_DIM):
            v_o[0, 0, grp * V_ROWS:grp * V_ROWS + HEAD_DIM, :] = y[grp * HEAD_DIM:(grp + 1) * HEAD_DIM]
            v_o[0, 0, grp * V_ROWS + HEAD_DIM:(grp + 1) * V_ROWS, :] = jnp.ones((V_ROWS - HEAD_DIM, T), BF16)
        lo += d_kv
    gt_o[0, 0] = _sigmoid(out_t[lo:, :])


def _compress_kernel(c_ref, pe_ref, w1_ref, w2_ref, kg_ref, o_ref, *, is_key, hidden):
    n_chunks = c_ref.shape[2] // CMP_STRIDE
    a = jnp.zeros((n_chunks, GROUPS_PER_PLANE * hidden), F32)
    b = jnp.zeros((n_chunks, GROUPS_PER_PLANE * hidden), F32)
    for pos in range(CMP_STRIDE):
        x = c_ref[0, 0, pl.ds(pos, n_chunks, stride=CMP_STRIDE), :]
        a = a + _dot((x + pe_ref[pos:pos + 1, :]).astype(BF16), w1_ref[pos])
        b = b + _dot((x + pe_ref[CMP_STRIDE + pos:CMP_STRIDE + pos + 1, :]).astype(BF16), w1_ref[CMP_STRIDE + pos])
    hid = a + pltpu.roll(b, n_chunks - 1, 0)
    hid = (hid * _sigmoid(hid)).astype(BF16)
    for grp in range(GROUPS_PER_PLANE):
        hid_g = hid[:, grp * hidden:(grp + 1) * hidden]
        if is_key:
            y = _dot(hid_g, w2_ref[...])
            y = _rms_rows(y, kg_ref[...])
            row = lax.broadcasted_iota(jnp.int32, y.shape, 0)
            o_ref[0, grp] = jnp.where(row < n_chunks - 1, y, 0.0).astype(BF16)
        else:
            y = _dot_nt(w2_ref[...], hid_g)
            col = lax.broadcasted_iota(jnp.int32, y.shape, 1)
            o_ref[0, grp] = jnp.where(col < n_chunks - 1, y, 0.0).astype(BF16)


def _compress(c_planes, pe, w1, w2, k_gain, is_key):
    b, planes, s, _ = c_planes.shape
    n_chunks = s // CMP_STRIDE
    hidden = w1.shape[1]
    w1_pos = w1.reshape(CMP_BLOCK, HEAD_DIM, hidden)
    zeros = jnp.zeros_like(w1_pos)
    w1_bd = jnp.concatenate([jnp.concatenate([w1_pos, zeros], axis=2),
                             jnp.concatenate([zeros, w1_pos], axis=2)], axis=1).astype(BF16)
    pe_planes = jnp.tile(pe, (1, GROUPS_PER_PLANE))
    w2_arg = w2.astype(BF16) if is_key else w2.T.astype(BF16)
    out_tail = (n_chunks, HEAD_DIM) if is_key else (HEAD_DIM, n_chunks)
    return pl.pallas_call(
        functools.partial(_compress_kernel, is_key=is_key, hidden=hidden),
        grid=(b, planes),
        in_specs=[pl.BlockSpec((1, 1, s, LANES), lambda i, j: (i, j, 0, 0)),
                  _resident(pe_planes.shape), _resident(w1_bd.shape),
                  _resident(w2_arg.shape), _resident((1, HEAD_DIM))],
        out_specs=pl.BlockSpec((1, GROUPS_PER_PLANE) + out_tail, lambda i, j: (i, j, 0, 0)),
        out_shape=jax.ShapeDtypeStruct((b, planes * GROUPS_PER_PLANE) + out_tail, BF16),
        compiler_params=_params(2),
        name="compress_k" if is_key else "compress_v",
    )(c_planes, pe_planes, w1_bd, w2_arg, k_gain.reshape(1, HEAD_DIM))


def _bias_kernel(rb_ref, bkt_near_ref, bkt_cmp_ref, near_o, cmp_o):
    h = pl.program_id(0)
    far = rb_ref[REL_BUCKETS - 1, h]

    def lookup(bkt):
        out = jnp.zeros(bkt.shape, F32)
        for k in range(REL_BUCKETS - 1):
            out = jnp.where(bkt == k, (rb_ref[k, h] - far) * LOG2E, out)
        return out

    key = lax.broadcasted_iota(jnp.int32, (T, T), 0)
    qry = lax.broadcasted_iota(jnp.int32, (T, T), 1)
    near_o[0, 0] = jnp.where(qry >= key, lookup(bkt_near_ref[0]), NEG)
    near_o[0, 1] = lookup(bkt_near_ref[1])
    near_o[0, 2] = jnp.where(key > qry, 0.0, NEG)
    cmp_o[0] = lookup(bkt_cmp_ref[...])


def _rel_bucket(dist):
    n = jnp.maximum(dist, 0)
    max_exact = REL_BUCKETS // 2
    nf = jnp.maximum(n, 1).astype(F32)
    large = max_exact + (jnp.log(nf / max_exact) / math.log(REL_MAX_DIST / max_exact)
                         * (REL_BUCKETS - max_exact)).astype(jnp.int32)
    large = jnp.minimum(large, REL_BUCKETS - 1)
    return jnp.where(n < max_exact, n, large)


def _bias_tiles(rel_bias):
    key = jnp.arange(T)[:, None]
    qry = jnp.arange(T)[None, :]
    bkt_near = jnp.stack([_rel_bucket(qry - key), _rel_bucket(qry - key + T)]).astype(jnp.int32)
    j = jnp.arange(CMP_NEAR)[:, None]
    bkt_cmp = _rel_bucket(qry - CMP_STRIDE * (j - CMP_PER_T) - (CMP_BLOCK - 1)).astype(jnp.int32)
    return pl.pallas_call(
        _bias_kernel,
        grid=(N_HEADS,),
        in_specs=[pl.BlockSpec(memory_space=pltpu.SMEM),
                  pl.BlockSpec((2, T, T), lambda h: (0, 0, 0)),
                  pl.BlockSpec((CMP_NEAR, T), lambda h: (0, 0))],
        out_specs=[pl.BlockSpec((1, N_NEAR, T, T), lambda h: (h // HEADS_PER_GROUP, 0, 0, h % HEADS_PER_GROUP)),
                   pl.BlockSpec((1, CMP_NEAR, T), lambda h: (h // HEADS_PER_GROUP, 0, h % HEADS_PER_GROUP))],
        out_shape=[jax.ShapeDtypeStruct((N_KV_GROUPS, N_NEAR, T, HEADS_PER_GROUP * T), F32),
                   jax.ShapeDtypeStruct((N_KV_GROUPS, CMP_NEAR, HEADS_PER_GROUP * T), F32)],
        compiler_params=_params(1),
        name="bias_tiles",
    )(rel_bias, bkt_near, bkt_cmp)


def _nsa_kernel(q_ref, ks_ref, vs_ref, kw_ref, vw_ref, kc_ref, vc_ref, gt_ref, nb_ref, cb_ref,
                o_ref,
                qcat_ref, sc_ref, psum_ref, selneg_ref, oc_ref, *rest, n_blk, n_cmp):
    s_bufs, rest = rest[:N_BUF], rest[N_BUF:]
    p_bufs, rest = rest[:N_BUF], rest[N_BUF:]
    w_bufs, rest = rest[:N_NEAR], rest[N_NEAR:]
    ms_ref, accs_ref, mw_ref, accw_ref = rest
    g = pl.program_id(1)
    qi = pl.program_id(2)
    R = HEADS_PER_GROUP
    W = R * T
    qry = lax.broadcasted_iota(jnp.int32, (1, W), 1) & (T - 1)
    t = qi * T + qry

    def q_head(r):
        return q_ref[0, 0, r * HEAD_DIM:(r + 1) * HEAD_DIM, :]

    rowgrp = lax.shift_right_logical(lax.broadcasted_iota(jnp.int32, (GROUP_WIDTH, T), 0),
                                     int(math.log2(HEAD_DIM)))
    for r in range(R):
        q4 = jnp.concatenate([q_head(r).astype(F32)] * N_KV_GROUPS, axis=0)
        qcat_ref[:, r * T:(r + 1) * T] = jnp.where(rowgrp == g, q4, 0.0).astype(BF16)

    for m_ref, acc_ref in ((ms_ref, accs_ref), (mw_ref, accw_ref)):
        m_ref[...] = jnp.full(m_ref.shape, NEG, F32)
        acc_ref[...] = jnp.zeros(acc_ref.shape, F32)

    def softmax_tile(s_ref, p_ref, m_ref, rows):
        top = None
        for j in range(BLK_PER_T):
            blk_max = s_ref[j * SEL_BLOCK:(j + 1) * SEL_BLOCK, :].reshape(SEL_BLOCK // SUBLANES, SUBLANES, W).max(axis=0)
            if rows is not None:
                blk_max = blk_max + rows[j]
            top = blk_max if top is None else jnp.maximum(top, blk_max)
        m_old = m_ref[...]
        m_new = jnp.maximum(m_old, jnp.max(top, axis=0, keepdims=True))
        for j in range(BLK_PER_T):
            shift = m_new if rows is None else m_new - rows[j]
            sl = slice(j * SEL_BLOCK, (j + 1) * SEL_BLOCK)
            p_ref[sl, :] = jnp.exp2(s_ref[sl, :] - shift).astype(BF16)
        m_ref[...] = m_new
        return jnp.exp2(m_old - m_new)

    neg_row = jnp.full((1, W), NEG, F32)

    def tile_of(y):
        return jnp.where(y == 0, qi, jnp.where(y == 1, jnp.maximum(qi - 1, 0), jnp.clip(y - 2, 0, qi)))

    win_tiles = [jnp.maximum(qi - back, 0) for back in range(N_NEAR)]
    for back in range(N_NEAR):
        w_bufs[back][...] = _dot(kw_ref[0, win_tiles[back]], qcat_ref[...]) + nb_ref[0, back]
    near0 = pl.multiple_of(qi * CMP_PER_T, CMP_PER_T)
    sc_ref[0:CMP_PER_T, :] = jnp.zeros((CMP_PER_T, W), F32)
    sc_ref[CMP_PER_T:, :] = _dot(kc_ref[0, 0], jnp.concatenate([q_head(r) for r in range(R)], axis=1))
    sc_ref[pl.ds(near0, CMP_NEAR), :] = sc_ref[pl.ds(near0, CMP_NEAR), :] + cb_ref[0]
    for y in range(PREFETCH):
        s = _dot(ks_ref[0, tile_of(y)], qcat_ref[...])
        s_bufs[y][...] = s + nb_ref[0, y] if y < 2 else s

    for back in range(N_NEAR):
        rows = None if back == 0 else [jnp.where(qi >= back, 0.0, neg_row)] * BLK_PER_T
        alpha = softmax_tile(w_bufs[back], p_bufs[back], mw_ref, rows)
        accw_ref[...] = alpha * accw_ref[...] + _dot(vw_ref[0, win_tiles[back], 0], p_bufs[back][...])

    def select_blocks(nc, nb):
        c_idx = lax.broadcasted_iota(jnp.int32, (nc, W), 0)
        valid_c = (c_idx * CMP_STRIDE + (CMP_BLOCK - 1) <= t) & (c_idx < n_cmp - 1)
        s = jnp.where(valid_c, sc_ref[CMP_PER_T:CMP_PER_T + nc, :], NEG)
        m = jnp.max(s, axis=0, keepdims=True)
        p = jnp.where(valid_c, jnp.exp2(s - m), 0.0)
        l = jnp.sum(p, axis=0, keepdims=True)
        p = p * jnp.where(l > 0.0, 1.0 / jnp.where(l > 0.0, l, 1.0), 0.0)
        oc_ref[...] = _dot(vc_ref[0, 0, :, 0:nc], p.astype(BF16))
        psum = p[:, 0:T]
        for r in range(1, R):
            psum = psum + p[:, r * T:(r + 1) * T]

        for ln in range(T // LANES):
            psum_ref[ln, 0:8, :] = jnp.zeros((8, LANES), F32)
            psum_ref[ln, 8:8 + nc, :] = psum[:, ln * LANES:(ln + 1) * LANES]

        def every_fourth(off):
            return jnp.concatenate([psum_ref[ln, pl.ds(8 + off, nb, stride=SEL_BLOCK // CMP_STRIDE), :]
                                    for ln in range(T // LANES)], axis=1)

        imp = every_fourth(0) + every_fourth(1) + every_fourth(2) + 0.5 * (every_fourth(3) + every_fourth(-1))
        blk = lax.broadcasted_iota(jnp.int32, (nb, T), 0)
        cur = lax.shift_right_logical(t[:, 0:T], int(math.log2(SEL_BLOCK)))
        forced = (blk == 0) | (blk == cur) | (blk == cur - 1)
        score = jnp.where(blk <= cur, imp + jnp.where(forced, FORCE, 0.0), -FORCE)
        for _ in range(min(N_SELECT, n_blk)):
            best = jnp.max(score, axis=0, keepdims=True)
            first = jnp.min(jnp.where(score == best, blk, nb), axis=0, keepdims=True)
            score = jnp.where(blk == first, -jnp.inf, score)
        selneg = jnp.where(score == -jnp.inf, 0.0, NEG)
        selneg_ref[0:nb, :] = jnp.concatenate([selneg] * R, axis=1)

    nq = n_cmp // CMP_PER_T
    n_var = max(1, n_cmp // CMP_CHUNK)
    tiles_per_var = nq // n_var
    for v in range(n_var):
        @pl.when((qi >= v * tiles_per_var) & (qi < (v + 1) * tiles_per_var))
        def _(v=v):
            n_tiles = (v + 1) * tiles_per_var
            select_blocks(n_tiles * CMP_PER_T, n_tiles * BLK_PER_T)

    def sel_rows(y):
        kj = tile_of(y)
        return [selneg_ref[pl.ds(kj * BLK_PER_T + j, 1), :] for j in range(BLK_PER_T)]

    p_bufs[N_BUF - 1][...] = jnp.zeros(p_bufs[0].shape, BF16)

    def positions(first, count, final=False):
        for k in range(count):
            y = first + k
            pv = _dot(vs_ref[0, tile_of(y - 1), 0], p_bufs[(k - 1) % N_BUF][...])
            if not final or k + PREFETCH < count:
                s_bufs[(k + PREFETCH) % N_BUF][...] = _dot(ks_ref[0, tile_of(y + PREFETCH)], qcat_ref[...])
            alpha = softmax_tile(s_bufs[k % N_BUF], p_bufs[k % N_BUF], ms_ref, sel_rows(y))
            accs_ref[...] = alpha * (accs_ref[...] + pv)

    n_long = lax.div(qi, LONG_TRIP)
    n_short = lax.div(qi - n_long * LONG_TRIP, N_BUF)

    def long_trip(i, carry):
        positions(LONG_TRIP * i, LONG_TRIP)
        return carry

    def short_trip(i, carry):
        positions(LONG_TRIP * n_long + N_BUF * i, N_BUF)
        return carry

    lax.fori_loop(0, n_long, long_trip, 0)
    lax.fori_loop(0, n_short, short_trip, 0)
    final_first = LONG_TRIP * n_long + N_BUF * n_short
    for count in range(1, N_BUF + 1):
        @pl.when(qi + 1 - final_first == count)
        def _(count=count):
            positions(final_first, count, final=True)
            accs_ref[...] = accs_ref[...] + _dot(vs_ref[0, tile_of(final_first + count - 1), 0],
                                                 p_bufs[count - 1][...])

    o_s = accs_ref[0:HEAD_DIM, :] / accs_ref[HEAD_DIM:HEAD_DIM + 1, :]
    o_w = accw_ref[0:HEAD_DIM, :] / accw_ref[HEAD_DIM:HEAD_DIM + 1, :]
    outs = []
    for r in range(R):
        head = g * R + r
        sl = slice(r * T, (r + 1) * T)
        g_c = gt_ref[0, 0, pl.ds(head, 1), :]
        g_s = gt_ref[0, 0, pl.ds(N_HEADS + head, 1), :]
        g_w = gt_ref[0, 0, pl.ds(2 * N_HEADS + head, 1), :]
        outs.append(g_c * oc_ref[:, sl] + g_s * o_s[:, sl] + g_w * o_w[:, sl])
    o_ref[0] = jnp.concatenate(outs, axis=0).T.astype(o_ref.dtype)


def _nsa(q_t, ks, vs_t, kw, vw_t, kc, vc_t, gates_t, near_bias, cmp_bias, seq):
    b, nq = q_t.shape[0], q_t.shape[1]
    n_blk = seq // SEL_BLOCK
    n_cmp = seq // CMP_STRIDE
    W = HEADS_PER_GROUP * T
    kv_spec = pl.BlockSpec((1, nq, T, GROUP_WIDTH), lambda i, j, k: (i, 0, 0, 0))
    vt_spec = pl.BlockSpec((1, nq, 1, V_ROWS, T), lambda i, j, k: (i, 0, j, 0, 0))
    stat = pltpu.VMEM((1, W), F32)
    acc = pltpu.VMEM((V_ROWS, W), F32)
    scores = pltpu.VMEM((T, W), F32)
    probs = pltpu.VMEM((T, W), BF16)
    return pl.pallas_call(
        functools.partial(_nsa_kernel, n_blk=n_blk, n_cmp=n_cmp),
        grid=(b, N_KV_GROUPS, nq),
        in_specs=[pl.BlockSpec((1, 1, GROUP_WIDTH, T), lambda i, j, k: (i, k, j, 0)),
                  kv_spec, vt_spec, kv_spec, vt_spec,
                  pl.BlockSpec((1, 1, n_cmp, HEAD_DIM), lambda i, j, k: (i, j, 0, 0)),
                  pl.BlockSpec((1, 1, HEAD_DIM, n_cmp), lambda i, j, k: (i, j, 0, 0)),
                  pl.BlockSpec((1, 1, gates_t.shape[2], T), lambda i, j, k: (i, k, 0, 0)),
                  pl.BlockSpec((1, N_NEAR, T, W), lambda i, j, k: (j, 0, 0, 0)),
                  pl.BlockSpec((1, CMP_NEAR, W), lambda i, j, k: (j, 0, 0))],
        out_specs=pl.BlockSpec((1, T, GROUP_WIDTH), lambda i, j, k: (i, k, j)),
        out_shape=jax.ShapeDtypeStruct((b, seq, N_HEADS * HEAD_DIM), BF16),
        scratch_shapes=[pltpu.VMEM((GROUP_WIDTH, W), BF16),
                        pltpu.VMEM((n_cmp + CMP_PER_T, W), F32),
                        pltpu.VMEM((T // LANES, n_cmp + 8, LANES), F32),
                        pltpu.VMEM((n_blk, W), F32),
                        pltpu.VMEM((HEAD_DIM, W), F32), *([scores] * N_BUF), *([probs] * N_BUF), *([scores] * N_NEAR),
                        stat, acc, stat, acc],
        compiler_params=_params(3),
        name="nsa",
    )(q_t, ks, vs_t, kw, vw_t, kc, vc_t, gates_t, near_bias, cmp_bias)


def _mem_kv_kernel(mem_ref, g_ref, w_ref, kg_ref, k_o, v_o):
    h = _rms_rows(mem_ref[0], g_ref[...]).astype(BF16)
    width = k_o.shape[2]
    hd = width // MEM_HEADS
    k = _dot(h, w_ref[:, :width])
    for i in range(MEM_HEADS):
        k_o[0, :, i * hd:(i + 1) * hd] = _rms_rows(k[:, i * hd:(i + 1) * hd], kg_ref[...]).astype(BF16)
    v_o[0] = _dot(h, w_ref[:, width:]).astype(BF16)


def _mem_attn_kernel(q_ref, k_ref, v_ref, qg_ref, o_ref):
    width = q_ref.shape[2]
    hd = width // MEM_HEADS
    for i in range(MEM_HEADS):
        sl = slice(i * hd, (i + 1) * hd)
        q = (_rms_rows(q_ref[0, :, sl].astype(F32), qg_ref[...]) * (hd ** -0.5)).astype(BF16)
        s = _dot_nt(q, k_ref[0, :, sl])
        p = jnp.exp(s - jnp.max(s, axis=-1, keepdims=True))
        p = p / jnp.sum(p, axis=-1, keepdims=True)
        o_ref[0, :, sl] = _dot(p.astype(BF16), v_ref[0, :, sl]).astype(o_ref.dtype)


def _mem_attention(q_mem, mem, mem_norm_g, w_mem_kv, q_g, k_g):
    b, s, width = q_mem.shape
    n_mem, d = mem.shape[1], mem.shape[2]
    hd = width // MEM_HEADS
    kv_shape = jax.ShapeDtypeStruct((b, n_mem, width), BF16)
    kv_block = pl.BlockSpec((1, n_mem, width), lambda i: (i, 0, 0))
    km, vm = pl.pallas_call(
        _mem_kv_kernel,
        grid=(b,),
        in_specs=[pl.BlockSpec((1, n_mem, d), lambda i: (i, 0, 0)),
                  _resident((1, d)), _resident((d, 2 * width)), _resident((1, hd))],
        out_specs=[kv_block, kv_block],
        out_shape=[kv_shape, kv_shape],
        compiler_params=_params(1),
        name="mem_kv",
    )(mem, mem_norm_g.reshape(1, d), w_mem_kv.astype(BF16), k_g.reshape(1, hd))
    kv_block2 = pl.BlockSpec((1, n_mem, width), lambda i, j: (i, 0, 0))
    return pl.pallas_call(
        _mem_attn_kernel,
        grid=(b, s // MEM_TM),
        in_specs=[pl.BlockSpec((1, MEM_TM, width), lambda i, j: (i, j, 0)),
                  kv_block2, kv_block2, _resident((1, hd))],
        out_specs=pl.BlockSpec((1, MEM_TM, width), lambda i, j: (i, j, 0)),
        out_shape=jax.ShapeDtypeStruct((b, s, width), BF16),
        compiler_params=_params(2),
        name="mem_attn",
    )(q_mem, km, vm, q_g.reshape(1, hd))


def _merge_kernel(x_ref, nsa_ref, mem_ref, cb_ref, cc_ref, cx_ref, hc_ref, hx_ref,
                  g1_ref, g2_ref, g3_ref, cw_ref, bias_ref, wo_ref, o_ref):
    j = pl.program_id(1)

    def f32(ref):
        return ref[0].astype(F32)

    u = f32(cc_ref) * f32(cx_ref)
    halo = jnp.where(j > 0, f32(hc_ref) * f32(hx_ref), 0.0)
    prev1 = halo[HALO - 1:HALO, :]
    prev2 = halo[HALO - 2:HALO - 1, :]
    row = lax.broadcasted_iota(jnp.int32, u.shape, 0)
    u1 = jnp.where(row == 0, prev1, pltpu.roll(u, 1, 0))
    u2 = jnp.where(row == 0, prev2, jnp.where(row == 1, prev1, pltpu.roll(u, 2, 0)))
    y = cw_ref[0:1, :] * u2 + cw_ref[1:2, :] * u1 + cw_ref[2:3, :] * u
    o_conv = f32(cb_ref) * (y + bias_ref[...])
    merged = (_sigmoid(f32(g1_ref)) * f32(nsa_ref) + _sigmoid(f32(g2_ref)) * o_conv
              + _sigmoid(f32(g3_ref)) * f32(mem_ref))
    o_ref[0] = x_ref[0] + _dot(merged.astype(BF16), wo_ref[...])


def _merge(x, o_nsa, o_mem, conv_in, merge_g, conv_w, conv_b, w_out):
    b, s, d = x.shape
    tm = MERGE_TM

    def col(c):
        return pl.BlockSpec((1, tm, d), lambda i, j: (i, j, c))

    def halo(c):
        return pl.BlockSpec((1, HALO, d), lambda i, j: (i, jnp.maximum(j * (tm // HALO) - 1, 0), c))

    return pl.pallas_call(
        _merge_kernel,
        grid=(b, s // tm),
        in_specs=[col(0), col(0), col(0),
                  col(0), col(1), col(2), halo(1), halo(2),
                  col(0), col(1), col(2),
                  _resident((CONV_WIDTH, d)), _resident((1, d)), _resident((d, d))],
        out_specs=col(0),
        out_shape=jax.ShapeDtypeStruct((b, s, d), F32),
        compiler_params=_params(2),
        name="merge",
    )(x, o_nsa, o_mem, conv_in, conv_in, conv_in, conv_in, conv_in,
      merge_g, merge_g, merge_g, conv_w, conv_b.reshape(1, d), w_out.astype(BF16))


def _layer(x, mem, ffn1_norm_g, ffn1_w_in, ffn1_w_out, mix_norm_g, w_in, q_norm_g, k_norm_g,
           cmp_pe_k, cmp_w1_k, cmp_w2_k, cmp_pe_v, cmp_w1_v, cmp_w2_v, conv_w, conv_b,
           mem_norm_g, w_mem_kv, mem_q_norm_g, mem_k_norm_g, w_out,
           ffn2_norm_g, ffn2_w_in, ffn2_w_out, near_bias, cmp_bias):
    b, s, d = x.shape
    assert s % T == 0 and s % MERGE_TM == 0 and (b * s) % FFN_TM == 0
    assert WINDOW == 2 * T and REL_MAX_DIST <= T // 2
    assert SEL_BLOCK == 4 * CMP_STRIDE and CMP_BLOCK == 2 * CMP_STRIDE
    nq = s // T
    d_q = N_HEADS * HEAD_DIM
    d_kv = N_KV_GROUPS * HEAD_DIM
    d_conv = conv_w.shape[1]
    d_mem = w_mem_kv.shape[1] // 2

    x = _ffn(x.reshape(b * s, d), ffn1_norm_g, ffn1_w_in, ffn1_w_out).reshape(b, s, d)

    o = 0
    w_q = w_in[:, o:o + d_q]; o += d_q
    w_kc, w_vc, w_ks, w_vs, w_kw, w_vw = [w_in[:, o + i * d_kv:o + (i + 1) * d_kv] for i in range(6)]
    o += 6 * d_kv
    w_g = w_in[:, o:o + 3 * N_HEADS]; o += 3 * N_HEADS
    w_conv = w_in[:, o:o + 3 * d_conv]; o += 3 * d_conv
    w_qm = w_in[:, o:o + d_mem]; o += d_mem
    w_mg = w_in[:, o:]

    w_rows = jnp.concatenate([w_ks, w_kw, w_kc, w_vc, w_conv, w_qm, w_mg], axis=1).astype(BF16)
    widths = (d_kv, d_kv, d_kv, d_kv, 3 * d_conv, d_mem, w_mg.shape[1])
    group_of = jnp.arange(d_kv) // HEAD_DIM
    block_diag = (group_of[:, None] == group_of[None, :]).astype(F32) / HEAD_DIM
    k_gain_row = jnp.tile(k_norm_g, N_KV_GROUPS).reshape(1, d_kv)

    def rows_out(wd, dt):
        return (pl.BlockSpec((1, ROW_TM, wd), lambda i, j: (i, j, 0)), jax.ShapeDtypeStruct((b, s, wd), dt))

    def planes_out(wd):
        n_planes = wd // LANES
        return (pl.BlockSpec((1, n_planes, ROW_TM, LANES), lambda i, j: (i, 0, j, 0)),
                jax.ShapeDtypeStruct((b, n_planes, s, LANES), F32))

    specs = [planes_out(wd) if i in (2, 3) else rows_out(wd, BF16) for i, wd in enumerate(widths)]
    ks, kw, kc, vc, conv_in, q_mem, merge_g = pl.pallas_call(
        functools.partial(_proj_rows_kernel, widths=widths),
        grid=(b, s // ROW_TM),
        in_specs=[pl.BlockSpec((1, ROW_TM, d), lambda i, j: (i, j, 0)),
                  _resident((1, d)), _resident(w_rows.shape), _resident((d_kv, d_kv)), _resident((1, d_kv))],
        out_specs=[sp[0] for sp in specs],
        out_shape=[sp[1] for sp in specs],
        compiler_params=_params(2),
        name="proj_rows",
    )(x, mix_norm_g.reshape(1, d), w_rows, block_diag, k_gain_row)

    w_g_t = w_g.reshape(d, N_HEADS, 3).transpose(2, 1, 0).reshape(3 * N_HEADS, d)
    n_gate_rows = 128
    w_g_t = jnp.pad(w_g_t, ((0, n_gate_rows - 3 * N_HEADS), (0, 0)))
    w_t = jnp.concatenate([w_q.T, w_vs.T, w_vw.T, w_g_t], axis=0).astype(BF16)

    def t_out(rows, dt):
        return (pl.BlockSpec((1, 1, rows, T), lambda i, j: (i, j, 0, 0)),
                jax.ShapeDtypeStruct((b, nq, rows, T), dt))

    v_rows = N_KV_GROUPS * V_ROWS
    t_specs = [t_out(d_q, BF16), t_out(v_rows, BF16), t_out(v_rows, BF16), t_out(n_gate_rows, F32)]
    q_t, vs_t, vw_t, gates_t = pl.pallas_call(
        functools.partial(_proj_t_kernel, d_q=d_q, d_kv=d_kv),
        grid=(b, nq),
        in_specs=[pl.BlockSpec((1, T, d), lambda i, j: (i, j, 0)),
                  _resident((1, d)), _resident(w_t.shape), _resident((HEAD_DIM, 1))],
        out_specs=[sp[0] for sp in t_specs],
        out_shape=[sp[1] for sp in t_specs],
        compiler_params=_params(2),
        name="proj_t",
    )(x, mix_norm_g.reshape(1, d), w_t, q_norm_g.reshape(HEAD_DIM, 1))

    k_cmp = _compress(kc, cmp_pe_k, cmp_w1_k, cmp_w2_k, k_norm_g, True)
    v_cmp_t = _compress(vc, cmp_pe_v, cmp_w1_v, cmp_w2_v, k_norm_g, False)

    o_nsa = _nsa(q_t, ks.reshape(b, nq, T, d_kv), vs_t.reshape(b, nq, N_KV_GROUPS, V_ROWS, T),
                 kw.reshape(b, nq, T, d_kv), vw_t.reshape(b, nq, N_KV_GROUPS, V_ROWS, T),
                 k_cmp, v_cmp_t, gates_t, near_bias, cmp_bias, s)

    o_mem = _mem_attention(q_mem, mem, mem_norm_g, w_mem_kv, mem_q_norm_g, mem_k_norm_g)
    x = _merge(x, o_nsa, o_mem, conv_in, merge_g, conv_w, conv_b, w_out)
    x = _ffn(x.reshape(b * s, d), ffn2_norm_g, ffn2_w_in, ffn2_w_out).reshape(b, s, d)
    return x


def kernel(x, mem, ffn1_norm_g, ffn1_w_in, ffn1_w_out, mix_norm_g, w_in, q_norm_g, k_norm_g, cmp_pe_k, cmp_w1_k, cmp_w2_k, cmp_pe_v, cmp_w1_v, cmp_w2_v, conv_w, conv_b, mem_norm_g, w_mem_kv, mem_q_norm_g, mem_k_norm_g, w_out, ffn2_norm_g, ffn2_w_in, ffn2_w_out, rel_bias):
    near_bias, cmp_bias = _bias_tiles(rel_bias)
    for l in range(ffn1_norm_g.shape[0]):
        x = _layer(x, mem, ffn1_norm_g[l], ffn1_w_in[l], ffn1_w_out[l], mix_norm_g[l], w_in[l],
                   q_norm_g[l], k_norm_g[l], cmp_pe_k[l], cmp_w1_k[l], cmp_w2_k[l],
                   cmp_pe_v[l], cmp_w1_v[l], cmp_w2_v[l], conv_w[l], conv_b[l],
                   mem_norm_g[l], w_mem_kv[l], mem_q_norm_g[l], mem_k_norm_g[l], w_out[l],
                   ffn2_norm_g[l], ffn2_w_in[l], ffn2_w_out[l], near_bias, cmp_bias)
    return x
```

```python
import functools
import math

import jax
import jax.numpy as jnp
from jax import lax
from jax.experimental import pallas as pl
from jax.experimental.pallas import tpu as pltpu

N_HEADS = 16
HEAD_DIM = 64
N_KV_GROUPS = 4
HEADS_PER_GROUP = N_HEADS // N_KV_GROUPS
GROUP_WIDTH = HEADS_PER_GROUP * HEAD_DIM
CMP_BLOCK = 32
CMP_STRIDE = 16
SEL_BLOCK = 64
N_SELECT = 16
WINDOW = 512
FORCE = 1e4
CONV_WIDTH = 3
MEM_HEADS = 4
REL_BUCKETS = 32
REL_MAX_DIST = 128
EPS = 1e-6
NEG = -1e30

T = 256
BLK_PER_T = T // SEL_BLOCK
CMP_PER_T = T // CMP_STRIDE
CMP_NEAR = 2 * CMP_PER_T
N_NEAR = WINDOW // T + 1
N_FORCED = 3
N_BUF = 4
PREFETCH = N_BUF // 2
LONG_TRIP = 2 * N_BUF
CMP_CHUNK = 128
FFN_TM = 512
ROW_TM = 256
MEM_TM = 512
MERGE_TM = 512
HALO = 16
LANES = 128
MXU_COLS = 256
SUBLANES = 8
GROUPS_PER_PLANE = LANES // HEAD_DIM
BF16_SUBLANES = 16
V_ROWS = HEAD_DIM + BF16_SUBLANES
LOG2E = math.log2(math.e)
VMEM_LIMIT = 52 * 1024 * 1024

F32 = jnp.float32
BF16 = jnp.bfloat16
HI = lax.Precision.HIGHEST


def _dot(a, b):
    return jnp.dot(a, b, preferred_element_type=F32)


def _dot_nt(a, b):
    return lax.dot_general(a, b, (((1,), (1,)), ((), ())), preferred_element_type=F32)


def _rms_rows(xf, g):
    return xf * lax.rsqrt(jnp.mean(xf * xf, axis=-1, keepdims=True) + EPS) * g


def _sigmoid(x):
    return 1.0 / (1.0 + jnp.exp(-x))


def _resident(shape):
    zeros = (0,) * len(shape)
    return pl.BlockSpec(shape, lambda *_: zeros, pipeline_mode=pl.Buffered(1))


def _params(n_axes):
    return pltpu.CompilerParams(dimension_semantics=("arbitrary",) * n_axes,
                                vmem_limit_bytes=VMEM_LIMIT)


def _ffn_kernel(x_ref, g_ref, wa_ref, wb_ref, wo_ref, o_ref, *, ff_bounds):
    x = x_ref[...]
    h = _rms_rows(x, g_ref[...]).astype(BF16)
    acc = jnp.zeros(x.shape, F32)
    for lo, hi in zip(ff_bounds[:-1], ff_bounds[1:]):
        a = _dot(h, wa_ref[:, lo:hi])
        b = _dot(h, wb_ref[:, lo:hi])
        z = (a * _sigmoid(a) * b).astype(BF16)
        acc = acc + _dot(z, wo_ref[lo:hi, :])
    o_ref[...] = x + 0.5 * acc


def _ffn(x2d, g, w_in, w_out):
    n, d = x2d.shape
    d_ff = w_out.shape[0]
    wa = w_in[:, :d_ff].astype(BF16)
    wb = w_in[:, d_ff:].astype(BF16)
    wo = w_out.astype(BF16)
    n_col_tiles = -(-d_ff // MXU_COLS)
    ff_bounds = (0, min(d_ff, (n_col_tiles + 1) // 2 * MXU_COLS), d_ff)
    return pl.pallas_call(
        functools.partial(_ffn_kernel, ff_bounds=ff_bounds),
        grid=(n // FFN_TM,),
        in_specs=[pl.BlockSpec((FFN_TM, d), lambda i: (i, 0)),
                  _resident((1, d)), _resident((d, d_ff)), _resident((d, d_ff)), _resident((d_ff, d))],
        out_specs=pl.BlockSpec((FFN_TM, d), lambda i: (i, 0)),
        out_shape=jax.ShapeDtypeStruct((n, d), F32),
        compiler_params=_params(1),
        name="ffn",
    )(x2d, g.reshape(1, d), wa, wb, wo)


def _proj_rows_kernel(x_ref, g_ref, w_ref, bd_ref, kg_ref,
                      ks_o, kw_o, kc_o, vc_o, conv_o, qm_o, mg_o, *, widths):
    h = _rms_rows(x_ref[0], g_ref[...]).astype(BF16)

    def knorm(k):
        ms = jnp.dot(k * k, bd_ref[...], precision=HI, preferred_element_type=F32)
        return (k * lax.rsqrt(ms + EPS) * kg_ref[...]).astype(BF16)

    lo = 0
    outs = (ks_o, kw_o, kc_o, vc_o, conv_o, qm_o, mg_o)
    for idx, (o_ref, wd) in enumerate(zip(outs, widths)):
        y = _dot(h, w_ref[:, lo:lo + wd])
        if idx < 2:
            o_ref[0] = knorm(y)
        elif idx < 4:
            for plane in range(wd // LANES):
                o_ref[0, plane] = y[:, plane * LANES:(plane + 1) * LANES]
        else:
            o_ref[0] = y.astype(o_ref.dtype)
        lo += wd


def _proj_t_kernel(x_ref, g_ref, wt_ref, qg_ref, q_o, vs_o, vw_o, gt_o, *, d_q, d_kv):
    h = _rms_rows(x_ref[0], g_ref[...]).astype(BF16)
    qg = qg_ref[...] * (HEAD_DIM ** -0.5 * LOG2E)
    out_t = _dot_nt(wt_ref[...], h)
    for hd in range(d_q // HEAD_DIM):
        q = out_t[hd * HEAD_DIM:(hd + 1) * HEAD_DIM, :]
        qn = q * lax.rsqrt(jnp.mean(q * q, axis=0, keepdims=True) + EPS) * qg
        q_o[0, 0, hd * HEAD_DIM:(hd + 1) * HEAD_DIM, :] = qn.astype(BF16)
    lo = d_q
    for v_o in (vs_o, vw_o):
        y = out_t[lo:lo + d_kv, :].astype(BF16)
        for grp in range(d_kv // HEAD_DIM):
            v_o[0, 0, grp * V_ROWS:grp * V_ROWS + HEAD_DIM, :] = y[grp * HEAD_DIM:(grp + 1) * HEAD_DIM]
            v_o[0, 0, grp * V_ROWS + HEAD_DIM:(grp + 1) * V_ROWS, :] = jnp.ones((V_ROWS - HEAD_DIM, T), BF16)
        lo += d_kv
    gt_o[0, 0] = _sigmoid(out_t[lo:, :])


def _compress_kernel(c_ref, pe_ref, w1_ref, w2_ref, kg_ref, o_ref, *, is_key, hidden):
    n_chunks = c_ref.shape[2] // CMP_STRIDE
    a = jnp.zeros((n_chunks, GROUPS_PER_PLANE * hidden), F32)
    b = jnp.zeros((n_chunks, GROUPS_PER_PLANE * hidden), F32)
    for pos in range(CMP_STRIDE):
        x = c_ref[0, 0, pl.ds(pos, n_chunks, stride=CMP_STRIDE), :]
        a = a + _dot((x + pe_ref[pos:pos + 1, :]).astype(BF16), w1_ref[pos])
        b = b + _dot((x + pe_ref[CMP_STRIDE + pos:CMP_STRIDE + pos + 1, :]).astype(BF16), w1_ref[CMP_STRIDE + pos])
    hid = a + pltpu.roll(b, n_chunks - 1, 0)
    hid = (hid * _sigmoid(hid)).astype(BF16)
    for grp in range(GROUPS_PER_PLANE):
        hid_g = hid[:, grp * hidden:(grp + 1) * hidden]
        if is_key:
            y = _dot(hid_g, w2_ref[...])
            y = _rms_rows(y, kg_ref[...])
            row = lax.broadcasted_iota(jnp.int32, y.shape, 0)
            o_ref[0, grp] = jnp.where(row < n_chunks - 1, y, 0.0).astype(BF16)
        else:
            y = _dot_nt(w2_ref[...], hid_g)
            col = lax.broadcasted_iota(jnp.int32, y.shape, 1)
            o_ref[0, grp] = jnp.where(col < n_chunks - 1, y, 0.0).astype(BF16)


def _compress(c_planes, pe, w1, w2, k_gain, is_key):
    b, planes, s, _ = c_planes.shape
    n_chunks = s // CMP_STRIDE
    hidden = w1.shape[1]
    w1_pos = w1.reshape(CMP_BLOCK, HEAD_DIM, hidden)
    zeros = jnp.zeros_like(w1_pos)
    w1_bd = jnp.concatenate([jnp.concatenate([w1_pos, zeros], axis=2),
                             jnp.concatenate([zeros, w1_pos], axis=2)], axis=1).astype(BF16)
    pe_planes = jnp.tile(pe, (1, GROUPS_PER_PLANE))
    w2_arg = w2.astype(BF16) if is_key else w2.T.astype(BF16)
    out_tail = (n_chunks, HEAD_DIM) if is_key else (HEAD_DIM, n_chunks)
    return pl.pallas_call(
        functools.partial(_compress_kernel, is_key=is_key, hidden=hidden),
        grid=(b, planes),
        in_specs=[pl.BlockSpec((1, 1, s, LANES), lambda i, j: (i, j, 0, 0)),
                  _resident(pe_planes.shape), _resident(w1_bd.shape),
                  _resident(w2_arg.shape), _resident((1, HEAD_DIM))],
        out_specs=pl.BlockSpec((1, GROUPS_PER_PLANE) + out_tail, lambda i, j: (i, j, 0, 0)),
        out_shape=jax.ShapeDtypeStruct((b, planes * GROUPS_PER_PLANE) + out_tail, BF16),
        compiler_params=_params(2),
        name="compress_k" if is_key else "compress_v",
    )(c_planes, pe_planes, w1_bd, w2_arg, k_gain.reshape(1, HEAD_DIM))


def _bias_kernel(rb_ref, bkt_near_ref, bkt_cmp_ref, near_o, cmp_o):
    h = pl.program_id(0)
    far = rb_ref[REL_BUCKETS - 1, h]

    def lookup(bkt):
        out = jnp.zeros(bkt.shape, F32)
        for k in range(REL_BUCKETS - 1):
            out = jnp.where(bkt == k, (rb_ref[k, h] - far) * LOG2E, out)
        return out

    key = lax.broadcasted_iota(jnp.int32, (T, T), 0)
    qry = lax.broadcasted_iota(jnp.int32, (T, T), 1)
    near_o[0, 0] = jnp.where(qry >= key, lookup(bkt_near_ref[0]), NEG)
    near_o[0, 1] = lookup(bkt_near_ref[1])
    near_o[0, 2] = jnp.where(key > qry, 0.0, NEG)
    cmp_o[0] = lookup(bkt_cmp_ref[...])


def _rel_bucket(dist):
    n = jnp.maximum(dist, 0)
    max_exact = REL_BUCKETS // 2
    nf = jnp.maximum(n, 1).astype(F32)
    large = max_exact + (jnp.log(nf / max_exact) / math.log(REL_MAX_DIST / max_exact)
                         * (REL_BUCKETS - max_exact)).astype(jnp.int32)
    large = jnp.minimum(large, REL_BUCKETS - 1)
    return jnp.where(n < max_exact, n, large)


def _bias_tiles(rel_bias):
    key = jnp.arange(T)[:, None]
    qry = jnp.arange(T)[None, :]
    bkt_near = jnp.stack([_rel_bucket(qry - key), _rel_bucket(qry - key + T)]).astype(jnp.int32)
    j = jnp.arange(CMP_NEAR)[:, None]
    bkt_cmp = _rel_bucket(qry - CMP_STRIDE * (j - CMP_PER_T) - (CMP_BLOCK - 1)).astype(jnp.int32)
    return pl.pallas_call(
        _bias_kernel,
        grid=(N_HEADS,),
        in_specs=[pl.BlockSpec(memory_space=pltpu.SMEM),
                  pl.BlockSpec((2, T, T), lambda h: (0, 0, 0)),
                  pl.BlockSpec((CMP_NEAR, T), lambda h: (0, 0))],
        out_specs=[pl.BlockSpec((1, N_NEAR, T, T), lambda h: (h // HEADS_PER_GROUP, 0, 0, h % HEADS_PER_GROUP)),
                   pl.BlockSpec((1, CMP_NEAR, T), lambda h: (h // HEADS_PER_GROUP, 0, h % HEADS_PER_GROUP))],
        out_shape=[jax.ShapeDtypeStruct((N_KV_GROUPS, N_NEAR, T, HEADS_PER_GROUP * T), F32),
                   jax.ShapeDtypeStruct((N_KV_GROUPS, CMP_NEAR, HEADS_PER_GROUP * T), F32)],
        compiler_params=_params(1),
        name="bias_tiles",
    )(rel_bias, bkt_near, bkt_cmp)


def _nsa_kernel(q_ref, ks_ref, vs_ref, kw_ref, vw_ref, kc_ref, vc_ref, gt_ref, nb_ref, cb_ref,
                o_ref,
                qcat_ref, sc_ref, psum_ref, selneg_ref, oc_ref, *rest, n_blk, n_cmp):
    s_bufs, rest = rest[:N_BUF], rest[N_BUF:]
    p_bufs, rest = rest[:N_BUF], rest[N_BUF:]
    w_bufs, rest = rest[:N_NEAR], rest[N_NEAR:]
    ms_ref, accs_ref, mw_ref, accw_ref = rest
    g = pl.program_id(1)
    qi = pl.program_id(2)
    R = HEADS_PER_GROUP
    W = R * T
    qry = lax.broadcasted_iota(jnp.int32, (1, W), 1) & (T - 1)
    t = qi * T + qry

    def q_head(r):
        return q_ref[0, 0, r * HEAD_DIM:(r + 1) * HEAD_DIM, :]

    rowgrp = lax.shift_right_logical(lax.broadcasted_iota(jnp.int32, (GROUP_WIDTH, T), 0),
                                     int(math.log2(HEAD_DIM)))
    for r in range(R):
        q4 = jnp.concatenate([q_head(r).astype(F32)] * N_KV_GROUPS, axis=0)
        qcat_ref[:, r * T:(r + 1) * T] = jnp.where(rowgrp == g, q4, 0.0).astype(BF16)

    for m_ref, acc_ref in ((ms_ref, accs_ref), (mw_ref, accw_ref)):
        m_ref[...] = jnp.full(m_ref.shape, NEG, F32)
        acc_ref[...] = jnp.zeros(acc_ref.shape, F32)

    def softmax_tile(s_ref, p_ref, m_ref, rows):
        top = None
        for j in range(BLK_PER_T):
            blk_max = s_ref[j * SEL_BLOCK:(j + 1) * SEL_BLOCK, :].reshape(SEL_BLOCK // SUBLANES, SUBLANES, W).max(axis=0)
            if rows is not None:
                blk_max = blk_max + rows[j]
            top = blk_max if top is None else jnp.maximum(top, blk_max)
        m_old = m_ref[...]
        m_new = jnp.maximum(m_old, jnp.max(top, axis=0, keepdims=True))
        for j in range(BLK_PER_T):
            shift = m_new if rows is None else m_new - rows[j]
            sl = slice(j * SEL_BLOCK, (j + 1) * SEL_BLOCK)
            p_ref[sl, :] = jnp.exp2(s_ref[sl, :] - shift).astype(BF16)
        m_ref[...] = m_new
        return jnp.exp2(m_old - m_new)

    neg_row = jnp.full((1, W), NEG, F32)

    def tile_of(y):
        return jnp.where(y == 0, qi, jnp.where(y == 1, jnp.maximum(qi - 1, 0), jnp.clip(y - 2, 0, qi)))

    win_tiles = [jnp.maximum(qi - back, 0) for back in range(N_NEAR)]
    for back in range(N_NEAR):
        w_bufs[back][...] = _dot(kw_ref[0, win_tiles[back]], qcat_ref[...]) + nb_ref[0, back]
    near0 = pl.multiple_of(qi * CMP_PER_T, CMP_PER_T)
    sc_ref[0:CMP_PER_T, :] = jnp.zeros((CMP_PER_T, W), F32)
    sc_ref[CMP_PER_T:, :] = _dot(kc_ref[0, 0], jnp.concatenate([q_head(r) for r in range(R)], axis=1))
    sc_ref[pl.ds(near0, CMP_NEAR), :] = sc_ref[pl.ds(near0, CMP_NEAR), :] + cb_ref[0]
    for y in range(PREFETCH):
        s = _dot(ks_ref[0, tile_of(y)], qcat_ref[...])
        s_bufs[y][...] = s + nb_ref[0, y] if y < 2 else s

    for back in range(N_NEAR):
        rows = None if back == 0 else [jnp.where(qi >= back, 0.0, neg_row)] * BLK_PER_T
        alpha = softmax_tile(w_bufs[back], p_bufs[back], mw_ref, rows)
        accw_ref[...] = alpha * accw_ref[...] + _dot(vw_ref[0, win_tiles[back], 0], p_bufs[back][...])

    def select_blocks(nc, nb):
        c_idx = lax.broadcasted_iota(jnp.int32, (nc, W), 0)
        valid_c = (c_idx * CMP_STRIDE + (CMP_BLOCK - 1) <= t) & (c_idx < n_cmp - 1)
        s = jnp.where(valid_c, sc_ref[CMP_PER_T:CMP_PER_T + nc, :], NEG)
        m = jnp.max(s, axis=0, keepdims=True)
        p = jnp.where(valid_c, jnp.exp2(s - m), 0.0)
        l = jnp.sum(p, axis=0, keepdims=True)
        p = p * jnp.where(l > 0.0, 1.0 / jnp.where(l > 0.0, l, 1.0), 0.0)
        oc_ref[...] = _dot(vc_ref[0, 0, :, 0:nc], p.astype(BF16))
        psum = p[:, 0:T]
        for r in range(1, R):
            psum = psum + p[:, r * T:(r + 1) * T]

        for ln in range(T // LANES):
            psum_ref[ln, 0:8, :] = jnp.zeros((8, LANES), F32)
            psum_ref[ln, 8:8 + nc, :] = psum[:, ln * LANES:(ln + 1) * LANES]

        def every_fourth(off):
            return jnp.concatenate([psum_ref[ln, pl.ds(8 + off, nb, stride=SEL_BLOCK // CMP_STRIDE), :]
                                    for ln in range(T // LANES)], axis=1)

        imp = every_fourth(0) + every_fourth(1) + every_fourth(2) + 0.5 * (every_fourth(3) + every_fourth(-1))
        blk = lax.broadcasted_iota(jnp.int32, (nb, T), 0)
        cur = lax.shift_right_logical(t[:, 0:T], int(math.log2(SEL_BLOCK)))
        forced = (blk == 0) | (blk == cur) | (blk == cur - 1)
        score = jnp.where(forced, -jnp.inf, jnp.where(blk <= cur, imp, -FORCE))
        for _ in range(min(N_SELECT, n_blk) - N_FORCED):
            best = jnp.max(score, axis=0, keepdims=True)
            first = jnp.min(jnp.where(score == best, blk, nb), axis=0, keepdims=True)
            score = jnp.where(blk == first, -jnp.inf, score)
        selneg = jnp.where(score == -jnp.inf, 0.0, NEG)
        selneg_ref[0:nb, :] = jnp.concatenate([selneg] * R, axis=1)

    nq = n_cmp // CMP_PER_T
    n_var = max(1, n_cmp // CMP_CHUNK)
    tiles_per_var = nq // n_var
    for v in range(n_var):
        @pl.when((qi >= v * tiles_per_var) & (qi < (v + 1) * tiles_per_var))
        def _(v=v):
            n_tiles = (v + 1) * tiles_per_var
            select_blocks(n_tiles * CMP_PER_T, n_tiles * BLK_PER_T)

    def sel_rows(y):
        kj = tile_of(y)
        return [selneg_ref[pl.ds(kj * BLK_PER_T + j, 1), :] for j in range(BLK_PER_T)]

    p_bufs[N_BUF - 1][...] = jnp.zeros(p_bufs[0].shape, BF16)

    def positions(first, count, final=False):
        for k in range(count):
            y = first + k
            pv = _dot(vs_ref[0, tile_of(y - 1), 0], p_bufs[(k - 1) % N_BUF][...])
            if not final or k + PREFETCH < count:
                s_bufs[(k + PREFETCH) % N_BUF][...] = _dot(ks_ref[0, tile_of(y + PREFETCH)], qcat_ref[...])
            alpha = softmax_tile(s_bufs[k % N_BUF], p_bufs[k % N_BUF], ms_ref, sel_rows(y))
            accs_ref[...] = alpha * (accs_ref[...] + pv)

    n_long = lax.div(qi, LONG_TRIP)
    n_short = lax.div(qi - n_long * LONG_TRIP, N_BUF)

    def long_trip(i, carry):
        positions(LONG_TRIP * i, LONG_TRIP)
        return carry

    def short_trip(i, carry):
        positions(LONG_TRIP * n_long + N_BUF * i, N_BUF)
        return carry

    lax.fori_loop(0, n_long, long_trip, 0)
    lax.fori_loop(0, n_short, short_trip, 0)
    final_first = LONG_TRIP * n_long + N_BUF * n_short
    for count in range(1, N_BUF + 1):
        @pl.when(qi + 1 - final_first == count)
        def _(count=count):
            positions(final_first, count, final=True)
            accs_ref[...] = accs_ref[...] + _dot(vs_ref[0, tile_of(final_first + count - 1), 0],
                                                 p_bufs[count - 1][...])

    o_s = accs_ref[0:HEAD_DIM, :] / accs_ref[HEAD_DIM:HEAD_DIM + 1, :]
    o_w = accw_ref[0:HEAD_DIM, :] / accw_ref[HEAD_DIM:HEAD_DIM + 1, :]
    outs = []
    for r in range(R):
        head = g * R + r
        sl = slice(r * T, (r + 1) * T)
        g_c = gt_ref[0, 0, pl.ds(head, 1), :]
        g_s = gt_ref[0, 0, pl.ds(N_HEADS + head, 1), :]
        g_w = gt_ref[0, 0, pl.ds(2 * N_HEADS + head, 1), :]
        outs.append(g_c * oc_ref[:, sl] + g_s * o_s[:, sl] + g_w * o_w[:, sl])
    o_ref[0] = jnp.concatenate(outs, axis=0).T.astype(o_ref.dtype)


def _nsa(q_t, ks, vs_t, kw, vw_t, kc, vc_t, gates_t, near_bias, cmp_bias, seq):
    b, nq = q_t.shape[0], q_t.shape[1]
    n_blk = seq // SEL_BLOCK
    n_cmp = seq // CMP_STRIDE
    W = HEADS_PER_GROUP * T
    kv_spec = pl.BlockSpec((1, nq, T, GROUP_WIDTH), lambda i, j, k: (i, 0, 0, 0))
    vt_spec = pl.BlockSpec((1, nq, 1, V_ROWS, T), lambda i, j, k: (i, 0, j, 0, 0))
    stat = pltpu.VMEM((1, W), F32)
    acc = pltpu.VMEM((V_ROWS, W), F32)
    scores = pltpu.VMEM((T, W), F32)
    probs = pltpu.VMEM((T, W), BF16)
    return pl.pallas_call(
        functools.partial(_nsa_kernel, n_blk=n_blk, n_cmp=n_cmp),
        grid=(b, N_KV_GROUPS, nq),
        in_specs=[pl.BlockSpec((1, 1, GROUP_WIDTH, T), lambda i, j, k: (i, k, j, 0)),
                  kv_spec, vt_spec, kv_spec, vt_spec,
                  pl.BlockSpec((1, 1, n_cmp, HEAD_DIM), lambda i, j, k: (i, j, 0, 0)),
                  pl.BlockSpec((1, 1, HEAD_DIM, n_cmp), lambda i, j, k: (i, j, 0, 0)),
                  pl.BlockSpec((1, 1, gates_t.shape[2], T), lambda i, j, k: (i, k, 0, 0)),
                  pl.BlockSpec((1, N_NEAR, T, W), lambda i, j, k: (j, 0, 0, 0)),
                  pl.BlockSpec((1, CMP_NEAR, W), lambda i, j, k: (j, 0, 0))],
        out_specs=pl.BlockSpec((1, T, GROUP_WIDTH), lambda i, j, k: (i, k, j)),
        out_shape=jax.ShapeDtypeStruct((b, seq, N_HEADS * HEAD_DIM), BF16),
        scratch_shapes=[pltpu.VMEM((GROUP_WIDTH, W), BF16),
                        pltpu.VMEM((n_cmp + CMP_PER_T, W), F32),
                        pltpu.VMEM((T // LANES, n_cmp + 8, LANES), F32),
                        pltpu.VMEM((n_blk, W), F32),
                        pltpu.VMEM((HEAD_DIM, W), F32), *([scores] * N_BUF), *([probs] * N_BUF), *([scores] * N_NEAR),
                        stat, acc, stat, acc],
        compiler_params=_params(3),
        name="nsa",
    )(q_t, ks, vs_t, kw, vw_t, kc, vc_t, gates_t, near_bias, cmp_bias)


def _mem_kv_kernel(mem_ref, g_ref, w_ref, kg_ref, k_o, v_o):
    h = _rms_rows(mem_ref[0], g_ref[...]).astype(BF16)
    width = k_o.shape[2]
    hd = width // MEM_HEADS
    k = _dot(h, w_ref[:, :width])
    for i in range(MEM_HEADS):
        k_o[0, :, i * hd:(i + 1) * hd] = _rms_rows(k[:, i * hd:(i + 1) * hd], kg_ref[...]).astype(BF16)
    v_o[0] = _dot(h, w_ref[:, width:]).astype(BF16)


def _mem_attn_kernel(q_ref, k_ref, v_ref, qg_ref, o_ref):
    width = q_ref.shape[2]
    hd = width // MEM_HEADS
    for i in range(MEM_HEADS):
        sl = slice(i * hd, (i + 1) * hd)
        q = (_rms_rows(q_ref[0, :, sl].astype(F32), qg_ref[...]) * (hd ** -0.5)).astype(BF16)
        s = _dot_nt(q, k_ref[0, :, sl])
        p = jnp.exp(s - jnp.max(s, axis=-1, keepdims=True))
        p = p / jnp.sum(p, axis=-1, keepdims=True)
        o_ref[0, :, sl] = _dot(p.astype(BF16), v_ref[0, :, sl]).astype(o_ref.dtype)


def _mem_attention(q_mem, mem, mem_norm_g, w_mem_kv, q_g, k_g):
    b, s, width = q_mem.shape
    n_mem, d = mem.shape[1], mem.shape[2]
    hd = width // MEM_HEADS
    kv_shape = jax.ShapeDtypeStruct((b, n_mem, width), BF16)
    kv_block = pl.BlockSpec((1, n_mem, width), lambda i: (i, 0, 0))
    km, vm = pl.pallas_call(
        _mem_kv_kernel,
        grid=(b,),
        in_specs=[pl.BlockSpec((1, n_mem, d), lambda i: (i, 0, 0)),
                  _resident((1, d)), _resident((d, 2 * width)), _resident((1, hd))],
        out_specs=[kv_block, kv_block],
        out_shape=[kv_shape, kv_shape],
        compiler_params=_params(1),
        name="mem_kv",
    )(mem, mem_norm_g.reshape(1, d), w_mem_kv.astype(BF16), k_g.reshape(1, hd))
    kv_block2 = pl.BlockSpec((1, n_mem, width), lambda i, j: (i, 0, 0))
    return pl.pallas_call(
        _mem_attn_kernel,
        grid=(b, s // MEM_TM),
        in_specs=[pl.BlockSpec((1, MEM_TM, width), lambda i, j: (i, j, 0)),
                  kv_block2, kv_block2, _resident((1, hd))],
        out_specs=pl.BlockSpec((1, MEM_TM, width), lambda i, j: (i, j, 0)),
        out_shape=jax.ShapeDtypeStruct((b, s, width), BF16),
        compiler_params=_params(2),
        name="mem_attn",
    )(q_mem, km, vm, q_g.reshape(1, hd))


def _merge_kernel(x_ref, nsa_ref, mem_ref, cb_ref, cc_ref, cx_ref, hc_ref, hx_ref,
                  g1_ref, g2_ref, g3_ref, cw_ref, bias_ref, wo_ref, o_ref):
    j = pl.program_id(1)

    def f32(ref):
        return ref[0].astype(F32)

    u = f32(cc_ref) * f32(cx_ref)
    halo = jnp.where(j > 0, f32(hc_ref) * f32(hx_ref), 0.0)
    prev1 = halo[HALO - 1:HALO, :]
    prev2 = halo[HALO - 2:HALO - 1, :]
    row = lax.broadcasted_iota(jnp.int32, u.shape, 0)
    u1 = jnp.where(row == 0, prev1, pltpu.roll(u, 1, 0))
    u2 = jnp.where(row == 0, prev2, jnp.where(row == 1, prev1, pltpu.roll(u, 2, 0)))
    y = cw_ref[0:1, :] * u2 + cw_ref[1:2, :] * u1 + cw_ref[2:3, :] * u
    o_conv = f32(cb_ref) * (y + bias_ref[...])
    merged = (_sigmoid(f32(g1_ref)) * f32(nsa_ref) + _sigmoid(f32(g2_ref)) * o_conv
              + _sigmoid(f32(g3_ref)) * f32(mem_ref))
    o_ref[0] = x_ref[0] + _dot(merged.astype(BF16), wo_ref[...])


def _merge(x, o_nsa, o_mem, conv_in, merge_g, conv_w, conv_b, w_out):
    b, s, d = x.shape
    tm = MERGE_TM

    def col(c):
        return pl.BlockSpec((1, tm, d), lambda i, j: (i, j, c))

    def halo(c):
        return pl.BlockSpec((1, HALO, d), lambda i, j: (i, jnp.maximum(j * (tm // HALO) - 1, 0), c))

    return pl.pallas_call(
        _merge_kernel,
        grid=(b, s // tm),
        in_specs=[col(0), col(0), col(0),
                  col(0), col(1), col(2), halo(1), halo(2),
                  col(0), col(1), col(2),
                  _resident((CONV_WIDTH, d)), _resident((1, d)), _resident((d, d))],
        out_specs=col(0),
        out_shape=jax.ShapeDtypeStruct((b, s, d), F32),
        compiler_params=_params(2),
        name="merge",
    )(x, o_nsa, o_mem, conv_in, conv_in, conv_in, conv_in, conv_in,
      merge_g, merge_g, merge_g, conv_w, conv_b.reshape(1, d), w_out.astype(BF16))


def _layer(x, mem, ffn1_norm_g, ffn1_w_in, ffn1_w_out, mix_norm_g, w_in, q_norm_g, k_norm_g,
           cmp_pe_k, cmp_w1_k, cmp_w2_k, cmp_pe_v, cmp_w1_v, cmp_w2_v, conv_w, conv_b,
           mem_norm_g, w_mem_kv, mem_q_norm_g, mem_k_norm_g, w_out,
           ffn2_norm_g, ffn2_w_in, ffn2_w_out, near_bias, cmp_bias):
    b, s, d = x.shape
    assert s % T == 0 and s % MERGE_TM == 0 and (b * s) % FFN_TM == 0
    assert WINDOW == 2 * T and REL_MAX_DIST <= T // 2
    assert SEL_BLOCK == 4 * CMP_STRIDE and CMP_BLOCK == 2 * CMP_STRIDE
    nq = s // T
    d_q = N_HEADS * HEAD_DIM
    d_kv = N_KV_GROUPS * HEAD_DIM
    d_conv = conv_w.shape[1]
    d_mem = w_mem_kv.shape[1] // 2

    x = _ffn(x.reshape(b * s, d), ffn1_norm_g, ffn1_w_in, ffn1_w_out).reshape(b, s, d)

    o = 0
    w_q = w_in[:, o:o + d_q]; o += d_q
    w_kc, w_vc, w_ks, w_vs, w_kw, w_vw = [w_in[:, o + i * d_kv:o + (i + 1) * d_kv] for i in range(6)]
    o += 6 * d_kv
    w_g = w_in[:, o:o + 3 * N_HEADS]; o += 3 * N_HEADS
    w_conv = w_in[:, o:o + 3 * d_conv]; o += 3 * d_conv
    w_qm = w_in[:, o:o + d_mem]; o += d_mem
    w_mg = w_in[:, o:]

    w_rows = jnp.concatenate([w_ks, w_kw, w_kc, w_vc, w_conv, w_qm, w_mg], axis=1).astype(BF16)
    widths = (d_kv, d_kv, d_kv, d_kv, 3 * d_conv, d_mem, w_mg.shape[1])
    group_of = jnp.arange(d_kv) // HEAD_DIM
    block_diag = (group_of[:, None] == group_of[None, :]).astype(F32) / HEAD_DIM
    k_gain_row = jnp.tile(k_norm_g, N_KV_GROUPS).reshape(1, d_kv)

    def rows_out(wd, dt):
        return (pl.BlockSpec((1, ROW_TM, wd), lambda i, j: (i, j, 0)), jax.ShapeDtypeStruct((b, s, wd), dt))

    def planes_out(wd):
        n_planes = wd // LANES
        return (pl.BlockSpec((1, n_planes, ROW_TM, LANES), lambda i, j: (i, 0, j, 0)),
                jax.ShapeDtypeStruct((b, n_planes, s, LANES), F32))

    specs = [planes_out(wd) if i in (2, 3) else rows_out(wd, BF16) for i, wd in enumerate(widths)]
    ks, kw, kc, vc, conv_in, q_mem, merge_g = pl.pallas_call(
        functools.partial(_proj_rows_kernel, widths=widths),
        grid=(b, s // ROW_TM),
        in_specs=[pl.BlockSpec((1, ROW_TM, d), lambda i, j: (i, j, 0)),
                  _resident((1, d)), _resident(w_rows.shape), _resident((d_kv, d_kv)), _resident((1, d_kv))],
        out_specs=[sp[0] for sp in specs],
        out_shape=[sp[1] for sp in specs],
        compiler_params=_params(2),
        name="proj_rows",
    )(x, mix_norm_g.reshape(1, d), w_rows, block_diag, k_gain_row)

    w_g_t = w_g.reshape(d, N_HEADS, 3).transpose(2, 1, 0).reshape(3 * N_HEADS, d)
    n_gate_rows = 128
    w_g_t = jnp.pad(w_g_t, ((0, n_gate_rows - 3 * N_HEADS), (0, 0)))
    w_t = jnp.concatenate([w_q.T, w_vs.T, w_vw.T, w_g_t], axis=0).astype(BF16)

    def t_out(rows, dt):
        return (pl.BlockSpec((1, 1, rows, T), lambda i, j: (i, j, 0, 0)),
                jax.ShapeDtypeStruct((b, nq, rows, T), dt))

    v_rows = N_KV_GROUPS * V_ROWS
    t_specs = [t_out(d_q, BF16), t_out(v_rows, BF16), t_out(v_rows, BF16), t_out(n_gate_rows, F32)]
    q_t, vs_t, vw_t, gates_t = pl.pallas_call(
        functools.partial(_proj_t_kernel, d_q=d_q, d_kv=d_kv),
        grid=(b, nq),
        in_specs=[pl.BlockSpec((1, T, d), lambda i, j: (i, j, 0)),
                  _resident((1, d)), _resident(w_t.shape), _resident((HEAD_DIM, 1))],
        out_specs=[sp[0] for sp in t_specs],
        out_shape=[sp[1] for sp in t_specs],
        compiler_params=_params(2),
        name="proj_t",
    )(x, mix_norm_g.reshape(1, d), w_t, q_norm_g.reshape(HEAD_DIM, 1))

    k_cmp = _compress(kc, cmp_pe_k, cmp_w1_k, cmp_w2_k, k_norm_g, True)
    v_cmp_t = _compress(vc, cmp_pe_v, cmp_w1_v, cmp_w2_v, k_norm_g, False)

    o_nsa = _nsa(q_t, ks.reshape(b, nq, T, d_kv), vs_t.reshape(b, nq, N_KV_GROUPS, V_ROWS, T),
                 kw.reshape(b, nq, T, d_kv), vw_t.reshape(b, nq, N_KV_GROUPS, V_ROWS, T),
                 k_cmp, v_cmp_t, gates_t, near_bias, cmp_bias, s)

    o_mem = _mem_attention(q_mem, mem, mem_norm_g, w_mem_kv, mem_q_norm_g, mem_k_norm_g)
    x = _merge(x, o_nsa, o_mem, conv_in, merge_g, conv_w, conv_b, w_out)
    x = _ffn(x.reshape(b * s, d), ffn2_norm_g, ffn2_w_in, ffn2_w_out).reshape(b, s, d)
    return x


def kernel(x, mem, ffn1_norm_g, ffn1_w_in, ffn1_w_out, mix_norm_g, w_in, q_norm_g, k_norm_g, cmp_pe_k, cmp_w1_k, cmp_w2_k, cmp_pe_v, cmp_w1_v, cmp_w2_v, conv_w, conv_b, mem_norm_g, w_mem_kv, mem_q_norm_g, mem_k_norm_g, w_out, ffn2_norm_g, ffn2_w_in, ffn2_w_out, rel_bias):
    near_bias, cmp_bias = _bias_tiles(rel_bias)
    for l in range(ffn1_norm_g.shape[0]):
        x = _layer(x, mem, ffn1_norm_g[l], ffn1_w_in[l], ffn1_w_out[l], mix_norm_g[l], w_in[l],
                   q_norm_g[l], k_norm_g[l], cmp_pe_k[l], cmp_w1_k[l], cmp_w2_k[l],
                   cmp_pe_v[l], cmp_w1_v[l], cmp_w2_v[l], conv_w[l], conv_b[l],
                   mem_norm_g[l], w_mem_kv[l], mem_q_norm_g[l], mem_k_norm_g[l], w_out[l],
                   ffn2_norm_g[l], ffn2_w_in[l], ffn2_w_out[l], near_bias, cmp_bias)
    return x
```

```python
import functools
import math

import jax
import jax.numpy as jnp
from jax import lax
from jax.experimental import pallas as pl
from jax.experimental.pallas import tpu as pltpu

N_HEADS = 16
HEAD_DIM = 64
N_KV_GROUPS = 4
HEADS_PER_GROUP = N_HEADS // N_KV_GROUPS
GROUP_WIDTH = HEADS_PER_GROUP * HEAD_DIM
CMP_BLOCK = 32
CMP_STRIDE = 16
SEL_BLOCK = 64
N_SELECT = 16
WINDOW = 512
FORCE = 1e4
CONV_WIDTH = 3
MEM_HEADS = 4
REL_BUCKETS = 32
REL_MAX_DIST = 128
EPS = 1e-6
NEG = -1e30

T = 256
BLK_PER_T = T // SEL_BLOCK
CMP_PER_T = T // CMP_STRIDE
CMP_NEAR = 2 * CMP_PER_T
N_NEAR = WINDOW // T + 1
N_FORCED = 3
N_BUF = 4
PREFETCH = N_BUF // 2
LONG_TRIP = 2 * N_BUF
CMP_CHUNK = 128
FFN_TM = 512
ROW_TM = 256
MEM_TM = 512
MERGE_TM = 512
HALO = 16
LANES = 128
MXU_COLS = 256
SUBLANES = 8
GROUPS_PER_PLANE = LANES // HEAD_DIM
BF16_SUBLANES = 16
V_ROWS = HEAD_DIM + BF16_SUBLANES
LOG2E = math.log2(math.e)
VMEM_LIMIT = 52 * 1024 * 1024

F32 = jnp.float32
BF16 = jnp.bfloat16
HI = lax.Precision.HIGHEST


def _dot(a, b):
    return jnp.dot(a, b, preferred_element_type=F32)


def _dot_nt(a, b):
    return lax.dot_general(a, b, (((1,), (1,)), ((), ())), preferred_element_type=F32)


def _rms_rows(xf, g):
    return xf * lax.rsqrt(jnp.mean(xf * xf, axis=-1, keepdims=True) + EPS) * g


def _sigmoid(x):
    return 1.0 / (1.0 + jnp.exp(-x))


def _resident(shape):
    zeros = (0,) * len(shape)
    return pl.BlockSpec(shape, lambda *_: zeros, pipeline_mode=pl.Buffered(1))


def _params(n_axes):
    return pltpu.CompilerParams(dimension_semantics=("arbitrary",) * n_axes,
                                vmem_limit_bytes=VMEM_LIMIT)


def _ffn_kernel(x_ref, g_ref, wa_ref, wb_ref, wo_ref, o_ref, *, ff_bounds):
    x = x_ref[...]
    h = _rms_rows(x, g_ref[...]).astype(BF16)
    acc = jnp.zeros(x.shape, F32)
    for lo, hi in zip(ff_bounds[:-1], ff_bounds[1:]):
        a = _dot(h, wa_ref[:, lo:hi])
        b = _dot(h, wb_ref[:, lo:hi])
        z = (a * _sigmoid(a) * b).astype(BF16)
        acc = acc + _dot(z, wo_ref[lo:hi, :])
    o_ref[...] = x + 0.5 * acc


def _ffn(x2d, g, w_in, w_out):
    n, d = x2d.shape
    d_ff = w_out.shape[0]
    wa = w_in[:, :d_ff].astype(BF16)
    wb = w_in[:, d_ff:].astype(BF16)
    wo = w_out.astype(BF16)
    n_col_tiles = -(-d_ff // MXU_COLS)
    ff_bounds = (0, min(d_ff, (n_col_tiles + 1) // 2 * MXU_COLS), d_ff)
    return pl.pallas_call(
        functools.partial(_ffn_kernel, ff_bounds=ff_bounds),
        grid=(n // FFN_TM,),
        in_specs=[pl.BlockSpec((FFN_TM, d), lambda i: (i, 0)),
                  _resident((1, d)), _resident((d, d_ff)), _resident((d, d_ff)), _resident((d_ff, d))],
        out_specs=pl.BlockSpec((FFN_TM, d), lambda i: (i, 0)),
        out_shape=jax.ShapeDtypeStruct((n, d), F32),
        compiler_params=_params(1),
        name="ffn",
    )(x2d, g.reshape(1, d), wa, wb, wo)


def _proj_rows_kernel(x_ref, g_ref, w_ref, bd_ref, kg_ref,
                      ks_o, kw_o, kc_o, vc_o, conv_o, qm_o, mg_o, *, widths):
    h = _rms_rows(x_ref[0], g_ref[...]).astype(BF16)

    def knorm(k):
        ms = jnp.dot(k * k, bd_ref[...], precision=HI, preferred_element_type=F32)
        return (k * lax.rsqrt(ms + EPS) * kg_ref[...]).astype(BF16)

    lo = 0
    outs = (ks_o, kw_o, kc_o, vc_o, conv_o, qm_o, mg_o)
    for idx, (o_ref, wd) in enumerate(zip(outs, widths)):
        y = _dot(h, w_ref[:, lo:lo + wd])
        if idx < 2:
            o_ref[0] = knorm(y)
        elif idx < 4:
            for plane in range(wd // LANES):
                o_ref[0, plane] = y[:, plane * LANES:(plane + 1) * LANES]
        else:
            o_ref[0] = y.astype(o_ref.dtype)
        lo += wd


def _proj_t_kernel(x_ref, g_ref, wt_ref, qg_ref, q_o, vs_o, vw_o, gt_o, *, d_q, d_kv):
    h = _rms_rows(x_ref[0], g_ref[...]).astype(BF16)
    qg = qg_ref[...] * (HEAD_DIM ** -0.5 * LOG2E)
    out_t = _dot_nt(wt_ref[...], h)
    for hd in range(d_q // HEAD_DIM):
        q = out_t[hd * HEAD_DIM:(hd + 1) * HEAD_DIM, :]
        qn = q * lax.rsqrt(jnp.mean(q * q, axis=0, keepdims=True) + EPS) * qg
        q_o[0, 0, hd * HEAD_DIM:(hd + 1) * HEAD_DIM, :] = qn.astype(BF16)
    lo = d_q
    for v_o in (vs_o, vw_o):
        y = out_t[lo:lo + d_kv, :].astype(BF16)
        for grp in range(d_kv // HEAD_DIM):
            v_o[0, 0, grp * V_ROWS:grp * V_ROWS + HEAD_DIM, :] = y[grp * HEAD_DIM:(grp + 1) * HEAD_DIM]
            v_o[0, 0, grp * V_ROWS + HEAD_DIM:(grp + 1) * V_ROWS, :] = jnp.ones((V_ROWS - HEAD_DIM, T), BF16)
        lo += d_kv
    gt_o[0, 0] = _sigmoid(out_t[lo:, :])


def _compress_kernel(c_ref, pe_ref, w1_ref, w2_ref, kg_ref, o_ref, *, is_key, hidden):
    n_chunks = c_ref.shape[2] // CMP_STRIDE
    a = jnp.zeros((n_chunks, GROUPS_PER_PLANE * hidden), F32)
    b = jnp.zeros((n_chunks, GROUPS_PER_PLANE * hidden), F32)
    for pos in range(CMP_STRIDE):
        x = c_ref[0, 0, pl.ds(pos, n_chunks, stride=CMP_STRIDE), :]
        a = a + _dot((x + pe_ref[pos:pos + 1, :]).astype(BF16), w1_ref[pos])
        b = b + _dot((x + pe_ref[CMP_STRIDE + pos:CMP_STRIDE + pos + 1, :]).astype(BF16), w1_ref[CMP_STRIDE + pos])
    hid = a + pltpu.roll(b, n_chunks - 1, 0)
    hid = (hid * _sigmoid(hid)).astype(BF16)
    for grp in range(GROUPS_PER_PLANE):
        hid_g = hid[:, grp * hidden:(grp + 1) * hidden]
        if is_key:
            y = _dot(hid_g, w2_ref[...])
            y = _rms_rows(y, kg_ref[...])
            row = lax.broadcasted_iota(jnp.int32, y.shape, 0)
            o_ref[0, grp] = jnp.where(row < n_chunks - 1, y, 0.0).astype(BF16)
        else:
            y = _dot_nt(w2_ref[...], hid_g)
            col = lax.broadcasted_iota(jnp.int32, y.shape, 1)
            o_ref[0, grp] = jnp.where(col < n_chunks - 1, y, 0.0).astype(BF16)


def _compress(c_planes, pe, w1, w2, k_gain, is_key):
    b, planes, s, _ = c_planes.shape
    n_chunks = s // CMP_STRIDE
    hidden = w1.shape[1]
    w1_pos = w1.reshape(CMP_BLOCK, HEAD_DIM, hidden)
    zeros = jnp.zeros_like(w1_pos)
    w1_bd = jnp.concatenate([jnp.concatenate([w1_pos, zeros], axis=2),
                             jnp.concatenate([zeros, w1_pos], axis=2)], axis=1).astype(BF16)
    pe_planes = jnp.tile(pe, (1, GROUPS_PER_PLANE))
    w2_arg = w2.astype(BF16) if is_key else w2.T.astype(BF16)
    out_tail = (n_chunks, HEAD_DIM) if is_key else (HEAD_DIM, n_chunks)
    return pl.pallas_call(
        functools.partial(_compress_kernel, is_key=is_key, hidden=hidden),
        grid=(b, planes),
        in_specs=[pl.BlockSpec((1, 1, s, LANES), lambda i, j: (i, j, 0, 0)),
                  _resident(pe_planes.shape), _resident(w1_bd.shape),
                  _resident(w2_arg.shape), _resident((1, HEAD_DIM))],
        out_specs=pl.BlockSpec((1, GROUPS_PER_PLANE) + out_tail, lambda i, j: (i, j, 0, 0)),
        out_shape=jax.ShapeDtypeStruct((b, planes * GROUPS_PER_PLANE) + out_tail, BF16),
        compiler_params=_params(2),
        name="compress_k" if is_key else "compress_v",
    )(c_planes, pe_planes, w1_bd, w2_arg, k_gain.reshape(1, HEAD_DIM))


def _bias_kernel(rb_ref, bkt_near_ref, bkt_cmp_ref, near_o, cmp_o):
    h = pl.program_id(0)
    far = rb_ref[REL_BUCKETS - 1, h]

    def lookup(bkt):
        out = jnp.zeros(bkt.shape, F32)
        for k in range(REL_BUCKETS - 1):
            out = jnp.where(bkt == k, (rb_ref[k, h] - far) * LOG2E, out)
        return out

    key = lax.broadcasted_iota(jnp.int32, (T, T), 0)
    qry = lax.broadcasted_iota(jnp.int32, (T, T), 1)
    near_o[0, 0] = jnp.where(qry >= key, lookup(bkt_near_ref[0]), NEG)
    near_o[0, 1] = lookup(bkt_near_ref[1])
    near_o[0, 2] = jnp.where(key > qry, 0.0, NEG)
    cmp_o[0] = lookup(bkt_cmp_ref[...])


def _rel_bucket(dist):
    n = jnp.maximum(dist, 0)
    max_exact = REL_BUCKETS // 2
    nf = jnp.maximum(n, 1).astype(F32)
    large = max_exact + (jnp.log(nf / max_exact) / math.log(REL_MAX_DIST / max_exact)
                         * (REL_BUCKETS - max_exact)).astype(jnp.int32)
    large = jnp.minimum(large, REL_BUCKETS - 1)
    return jnp.where(n < max_exact, n, large)


def _bias_tiles(rel_bias):
    key = jnp.arange(T)[:, None]
    qry = jnp.arange(T)[None, :]
    bkt_near = jnp.stack([_rel_bucket(qry - key), _rel_bucket(qry - key + T)]).astype(jnp.int32)
    j = jnp.arange(CMP_NEAR)[:, None]
    bkt_cmp = _rel_bucket(qry - CMP_STRIDE * (j - CMP_PER_T) - (CMP_BLOCK - 1)).astype(jnp.int32)
    return pl.pallas_call(
        _bias_kernel,
        grid=(N_HEADS,),
        in_specs=[pl.BlockSpec(memory_space=pltpu.SMEM),
                  pl.BlockSpec((2, T, T), lambda h: (0, 0, 0)),
                  pl.BlockSpec((CMP_NEAR, T), lambda h: (0, 0))],
        out_specs=[pl.BlockSpec((1, N_NEAR, T, T), lambda h: (h // HEADS_PER_GROUP, 0, 0, h % HEADS_PER_GROUP)),
                   pl.BlockSpec((1, CMP_NEAR, T), lambda h: (h // HEADS_PER_GROUP, 0, h % HEADS_PER_GROUP))],
        out_shape=[jax.ShapeDtypeStruct((N_KV_GROUPS, N_NEAR, T, HEADS_PER_GROUP * T), F32),
                   jax.ShapeDtypeStruct((N_KV_GROUPS, CMP_NEAR, HEADS_PER_GROUP * T), F32)],
        compiler_params=_params(1),
        name="bias_tiles",
    )(rel_bias, bkt_near, bkt_cmp)


def _nsa_kernel(q_ref, ks_ref, vs_ref, kw_ref, vw_ref, kc_ref, vc_ref, gt_ref, nb_ref, cb_ref,
                o_ref,
                qcat_ref, sc_ref, psum_ref, selneg_ref, oc_ref, *rest, n_blk, n_cmp):
    s_bufs, rest = rest[:N_BUF], rest[N_BUF:]
    p_bufs, rest = rest[:N_BUF], rest[N_BUF:]
    w_bufs, rest = rest[:N_NEAR], rest[N_NEAR:]
    ms_ref, accs_ref, mw_ref, accw_ref = rest
    g = pl.program_id(1)
    qi = pl.program_id(2)
    R = HEADS_PER_GROUP
    W = R * T
    qry = lax.broadcasted_iota(jnp.int32, (1, W), 1) & (T - 1)
    t = qi * T + qry

    def q_head(r):
        return q_ref[0, 0, r * HEAD_DIM:(r + 1) * HEAD_DIM, :]

    rowgrp = lax.shift_right_logical(lax.broadcasted_iota(jnp.int32, (GROUP_WIDTH, T), 0),
                                     int(math.log2(HEAD_DIM)))
    for r in range(R):
        q4 = jnp.concatenate([q_head(r).astype(F32)] * N_KV_GROUPS, axis=0)
        qcat_ref[:, r * T:(r + 1) * T] = jnp.where(rowgrp == g, q4, 0.0).astype(BF16)

    for m_ref, acc_ref in ((ms_ref, accs_ref), (mw_ref, accw_ref)):
        m_ref[...] = jnp.full(m_ref.shape, NEG, F32)
        acc_ref[...] = jnp.zeros(acc_ref.shape, F32)

    def softmax_tile(s_ref, p_ref, m_ref, rows):
        top = None
        for j in range(BLK_PER_T):
            blk_max = s_ref[j * SEL_BLOCK:(j + 1) * SEL_BLOCK, :].reshape(SEL_BLOCK // SUBLANES, SUBLANES, W).max(axis=0)
            if rows is not None:
                blk_max = blk_max + rows[j]
            top = blk_max if top is None else jnp.maximum(top, blk_max)
        m_old = m_ref[...]
        m_new = jnp.maximum(m_old, jnp.max(top, axis=0, keepdims=True))
        for j in range(BLK_PER_T):
            shift = m_new if rows is None else m_new - rows[j]
            sl = slice(j * SEL_BLOCK, (j + 1) * SEL_BLOCK)
            p_ref[sl, :] = jnp.exp2(s_ref[sl, :] - shift).astype(BF16)
        m_ref[...] = m_new
        return jnp.exp2(m_old - m_new)

    neg_row = jnp.full((1, W), NEG, F32)

    def tile_of(y):
        return jnp.where(y == 0, qi, jnp.where(y == 1, jnp.maximum(qi - 1, 0), jnp.clip(y - 2, 0, qi)))

    win_tiles = [jnp.maximum(qi - back, 0) for back in range(N_NEAR)]
    for back in range(N_NEAR):
        w_bufs[back][...] = _dot(kw_ref[0, win_tiles[back]], qcat_ref[...]) + nb_ref[0, back]
    near0 = pl.multiple_of(qi * CMP_PER_T, CMP_PER_T)
    sc_ref[0:CMP_PER_T, :] = jnp.zeros((CMP_PER_T, W), F32)
    sc_ref[CMP_PER_T:, :] = _dot(kc_ref[0, 0], jnp.concatenate([q_head(r) for r in range(R)], axis=1))
    sc_ref[pl.ds(near0, CMP_NEAR), :] = sc_ref[pl.ds(near0, CMP_NEAR), :] + cb_ref[0]
    for y in range(PREFETCH):
        s = _dot(ks_ref[0, tile_of(y)], qcat_ref[...])
        s_bufs[y][...] = s + nb_ref[0, y] if y < 2 else s

    for back in range(N_NEAR):
        rows = None if back == 0 else [jnp.where(qi >= back, 0.0, neg_row)] * BLK_PER_T
        alpha = softmax_tile(w_bufs[back], p_bufs[back], mw_ref, rows)
        accw_ref[...] = alpha * accw_ref[...] + _dot(vw_ref[0, win_tiles[back], 0], p_bufs[back][...])

    def select_blocks(nc, nb):
        last_c = jnp.minimum(lax.shift_right_arithmetic(t - (CMP_BLOCK - 1), int(math.log2(CMP_STRIDE))),
                             n_cmp - 2)
        c_idx = lax.broadcasted_iota(jnp.int32, (nc, W), 0)
        s = jnp.where(c_idx <= last_c, sc_ref[CMP_PER_T:CMP_PER_T + nc, :], NEG)
        m = jnp.max(s, axis=0, keepdims=True)
        p = jnp.exp2(s - m)
        l = jnp.sum(p, axis=0, keepdims=True)
        p = p * jnp.where(last_c >= 0, 1.0 / l, 0.0)
        oc_ref[...] = _dot(vc_ref[0, 0, :, 0:nc], p.astype(BF16))
        psum = p[:, 0:T]
        for r in range(1, R):
            psum = psum + p[:, r * T:(r + 1) * T]

        for ln in range(T // LANES):
            psum_ref[ln, 0:8, :] = jnp.zeros((8, LANES), F32)
            psum_ref[ln, 8:8 + nc, :] = psum[:, ln * LANES:(ln + 1) * LANES]

        def every_fourth(off):
            return jnp.concatenate([psum_ref[ln, pl.ds(8 + off, nb, stride=SEL_BLOCK // CMP_STRIDE), :]
                                    for ln in range(T // LANES)], axis=1)

        imp = every_fourth(0) + every_fourth(1) + every_fourth(2) + 0.5 * (every_fourth(3) + every_fourth(-1))
        blk = lax.broadcasted_iota(jnp.int32, (nb, T), 0)
        cur = lax.shift_right_logical(t[:, 0:T], int(math.log2(SEL_BLOCK)))
        forced = (blk == 0) | (blk == cur) | (blk == cur - 1)
        score = jnp.where(forced, -jnp.inf, jnp.where(blk <= cur, imp, -FORCE))
        for _ in range(min(N_SELECT, n_blk) - N_FORCED):
            best = jnp.max(score, axis=0, keepdims=True)
            first = jnp.min(jnp.where(score == best, blk, nb), axis=0, keepdims=True)
            score = jnp.where(blk == first, -jnp.inf, score)
        selneg = jnp.where(score == -jnp.inf, 0.0, NEG)
        selneg_ref[0:nb, :] = jnp.concatenate([selneg] * R, axis=1)

    nq = n_cmp // CMP_PER_T
    n_var = max(1, n_cmp // CMP_CHUNK)
    tiles_per_var = nq // n_var
    for v in range(n_var):
        @pl.when((qi >= v * tiles_per_var) & (qi < (v + 1) * tiles_per_var))
        def _(v=v):
            n_tiles = (v + 1) * tiles_per_var
            select_blocks(n_tiles * CMP_PER_T, n_tiles * BLK_PER_T)

    def sel_rows(y):
        kj = tile_of(y)
        return [selneg_ref[pl.ds(kj * BLK_PER_T + j, 1), :] for j in range(BLK_PER_T)]

    p_bufs[N_BUF - 1][...] = jnp.zeros(p_bufs[0].shape, BF16)

    def positions(first, count, final=False):
        for k in range(count):
            y = first + k
            pv = _dot(vs_ref[0, tile_of(y - 1), 0], p_bufs[(k - 1) % N_BUF][...])
            if not final or k + PREFETCH < count:
                s_bufs[(k + PREFETCH) % N_BUF][...] = _dot(ks_ref[0, tile_of(y + PREFETCH)], qcat_ref[...])
            alpha = softmax_tile(s_bufs[k % N_BUF], p_bufs[k % N_BUF], ms_ref, sel_rows(y))
            accs_ref[...] = alpha * (accs_ref[...] + pv)

    n_long = lax.div(qi, LONG_TRIP)
    n_short = lax.div(qi - n_long * LONG_TRIP, N_BUF)

    def long_trip(i, carry):
        positions(LONG_TRIP * i, LONG_TRIP)
        return carry

    def short_trip(i, carry):
        positions(LONG_TRIP * n_long + N_BUF * i, N_BUF)
        return carry

    lax.fori_loop(0, n_long, long_trip, 0)
    lax.fori_loop(0, n_short, short_trip, 0)
    final_first = LONG_TRIP * n_long + N_BUF * n_short
    for count in range(1, N_BUF + 1):
        @pl.when(qi + 1 - final_first == count)
        def _(count=count):
            positions(final_first, count, final=True)
            accs_ref[...] = accs_ref[...] + _dot(vs_ref[0, tile_of(final_first + count - 1), 0],
                                                 p_bufs[count - 1][...])

    o_s = accs_ref[0:HEAD_DIM, :] / accs_ref[HEAD_DIM:HEAD_DIM + 1, :]
    o_w = accw_ref[0:HEAD_DIM, :] / accw_ref[HEAD_DIM:HEAD_DIM + 1, :]
    outs = []
    for r in range(R):
        head = g * R + r
        sl = slice(r * T, (r + 1) * T)
        g_c = gt_ref[0, 0, pl.ds(head, 1), :]
        g_s = gt_ref[0, 0, pl.ds(N_HEADS + head, 1), :]
        g_w = gt_ref[0, 0, pl.ds(2 * N_HEADS + head, 1), :]
        outs.append(g_c * oc_ref[:, sl] + g_s * o_s[:, sl] + g_w * o_w[:, sl])
    o_ref[0] = jnp.concatenate(outs, axis=0).T.astype(o_ref.dtype)


def _nsa(q_t, ks, vs_t, kw, vw_t, kc, vc_t, gates_t, near_bias, cmp_bias, seq):
    b, nq = q_t.shape[0], q_t.shape[1]
    n_blk = seq // SEL_BLOCK
    n_cmp = seq // CMP_STRIDE
    W = HEADS_PER_GROUP * T
    kv_spec = pl.BlockSpec((1, nq, T, GROUP_WIDTH), lambda i, j, k: (i, 0, 0, 0))
    vt_spec = pl.BlockSpec((1, nq, 1, V_ROWS, T), lambda i, j, k: (i, 0, j, 0, 0))
    stat = pltpu.VMEM((1, W), F32)
    acc = pltpu.VMEM((V_ROWS, W), F32)
    scores = pltpu.VMEM((T, W), F32)
    probs = pltpu.VMEM((T, W), BF16)
    return pl.pallas_call(
        functools.partial(_nsa_kernel, n_blk=n_blk, n_cmp=n_cmp),
        grid=(b, N_KV_GROUPS, nq),
        in_specs=[pl.BlockSpec((1, 1, GROUP_WIDTH, T), lambda i, j, k: (i, k, j, 0)),
                  kv_spec, vt_spec, kv_spec, vt_spec,
                  pl.BlockSpec((1, 1, n_cmp, HEAD_DIM), lambda i, j, k: (i, j, 0, 0)),
                  pl.BlockSpec((1, 1, HEAD_DIM, n_cmp), lambda i, j, k: (i, j, 0, 0)),
                  pl.BlockSpec((1, 1, gates_t.shape[2], T), lambda i, j, k: (i, k, 0, 0)),
                  pl.BlockSpec((1, N_NEAR, T, W), lambda i, j, k: (j, 0, 0, 0)),
                  pl.BlockSpec((1, CMP_NEAR, W), lambda i, j, k: (j, 0, 0))],
        out_specs=pl.BlockSpec((1, T, GROUP_WIDTH), lambda i, j, k: (i, k, j)),
        out_shape=jax.ShapeDtypeStruct((b, seq, N_HEADS * HEAD_DIM), BF16),
        scratch_shapes=[pltpu.VMEM((GROUP_WIDTH, W), BF16),
                        pltpu.VMEM((n_cmp + CMP_PER_T, W), F32),
                        pltpu.VMEM((T // LANES, n_cmp + 8, LANES), F32),
                        pltpu.VMEM((n_blk, W), F32),
                        pltpu.VMEM((HEAD_DIM, W), F32), *([scores] * N_BUF), *([probs] * N_BUF), *([scores] * N_NEAR),
                        stat, acc, stat, acc],
        compiler_params=_params(3),
        name="nsa",
    )(q_t, ks, vs_t, kw, vw_t, kc, vc_t, gates_t, near_bias, cmp_bias)


def _mem_kv_kernel(mem_ref, g_ref, w_ref, kg_ref, k_o, v_o):
    h = _rms_rows(mem_ref[0], g_ref[...]).astype(BF16)
    width = k_o.shape[2]
    hd = width // MEM_HEADS
    k = _dot(h, w_ref[:, :width])
    for i in range(MEM_HEADS):
        k_o[0, :, i * hd:(i + 1) * hd] = _rms_rows(k[:, i * hd:(i + 1) * hd], kg_ref[...]).astype(BF16)
    v_o[0] = _dot(h, w_ref[:, width:]).astype(BF16)


def _mem_attn_kernel(q_ref, k_ref, v_ref, qg_ref, o_ref):
    width = q_ref.shape[2]
    hd = width // MEM_HEADS
    for i in range(MEM_HEADS):
        sl = slice(i * hd, (i + 1) * hd)
        q = (_rms_rows(q_ref[0, :, sl].astype(F32), qg_ref[...]) * (hd ** -0.5)).astype(BF16)
        s = _dot_nt(q, k_ref[0, :, sl])
        p = jnp.exp(s - jnp.max(s, axis=-1, keepdims=True))
        p = p / jnp.sum(p, axis=-1, keepdims=True)
        o_ref[0, :, sl] = _dot(p.astype(BF16), v_ref[0, :, sl]).astype(o_ref.dtype)


def _mem_attention(q_mem, mem, mem_norm_g, w_mem_kv, q_g, k_g):
    b, s, width = q_mem.shape
    n_mem, d = mem.shape[1], mem.shape[2]
    hd = width // MEM_HEADS
    kv_shape = jax.ShapeDtypeStruct((b, n_mem, width), BF16)
    kv_block = pl.BlockSpec((1, n_mem, width), lambda i: (i, 0, 0))
    km, vm = pl.pallas_call(
        _mem_kv_kernel,
        grid=(b,),
        in_specs=[pl.BlockSpec((1, n_mem, d), lambda i: (i, 0, 0)),
                  _resident((1, d)), _resident((d, 2 * width)), _resident((1, hd))],
        out_specs=[kv_block, kv_block],
        out_shape=[kv_shape, kv_shape],
        compiler_params=_params(1),
        name="mem_kv",
    )(mem, mem_norm_g.reshape(1, d), w_mem_kv.astype(BF16), k_g.reshape(1, hd))
    kv_block2 = pl.BlockSpec((1, n_mem, width), lambda i, j: (i, 0, 0))
    return pl.pallas_call(
        _mem_attn_kernel,
        grid=(b, s // MEM_TM),
        in_specs=[pl.BlockSpec((1, MEM_TM, width), lambda i, j: (i, j, 0)),
                  kv_block2, kv_block2, _resident((1, hd))],
        out_specs=pl.BlockSpec((1, MEM_TM, width), lambda i, j: (i, j, 0)),
        out_shape=jax.ShapeDtypeStruct((b, s, width), BF16),
        compiler_params=_params(2),
        name="mem_attn",
    )(q_mem, km, vm, q_g.reshape(1, hd))


def _merge_kernel(x_ref, nsa_ref, mem_ref, cb_ref, cc_ref, cx_ref, hc_ref, hx_ref,
                  g1_ref, g2_ref, g3_ref, cw_ref, bias_ref, wo_ref, o_ref):
    j = pl.program_id(1)

    def f32(ref):
        return ref[0].astype(F32)

    u = f32(cc_ref) * f32(cx_ref)
    halo = jnp.where(j > 0, f32(hc_ref) * f32(hx_ref), 0.0)
    prev1 = halo[HALO - 1:HALO, :]
    prev2 = halo[HALO - 2:HALO - 1, :]
    row = lax.broadcasted_iota(jnp.int32, u.shape, 0)
    u1 = jnp.where(row == 0, prev1, pltpu.roll(u, 1, 0))
    u2 = jnp.where(row == 0, prev2, jnp.where(row == 1, prev1, pltpu.roll(u, 2, 0)))
    y = cw_ref[0:1, :] * u2 + cw_ref[1:2, :] * u1 + cw_ref[2:3, :] * u
    o_conv = f32(cb_ref) * (y + bias_ref[...])
    merged = (_sigmoid(f32(g1_ref)) * f32(nsa_ref) + _sigmoid(f32(g2_ref)) * o_conv
              + _sigmoid(f32(g3_ref)) * f32(mem_ref))
    o_ref[0] = x_ref[0] + _dot(merged.astype(BF16), wo_ref[...])


def _merge(x, o_nsa, o_mem, conv_in, merge_g, conv_w, conv_b, w_out):
    b, s, d = x.shape
    tm = MERGE_TM

    def col(c):
        return pl.BlockSpec((1, tm, d), lambda i, j: (i, j, c))

    def halo(c):
        return pl.BlockSpec((1, HALO, d), lambda i, j: (i, jnp.maximum(j * (tm // HALO) - 1, 0), c))

    return pl.pallas_call(
        _merge_kernel,
        grid=(b, s // tm),
        in_specs=[col(0), col(0), col(0),
                  col(0), col(1), col(2), halo(1), halo(2),
                  col(0), col(1), col(2),
                  _resident((CONV_WIDTH, d)), _resident((1, d)), _resident((d, d))],
        out_specs=col(0),
        out_shape=jax.ShapeDtypeStruct((b, s, d), F32),
        compiler_params=_params(2),
        name="merge",
    )(x, o_nsa, o_mem, conv_in, conv_in, conv_in, conv_in, conv_in,
      merge_g, merge_g, merge_g, conv_w, conv_b.reshape(1, d), w_out.astype(BF16))


def _layer(x, mem, ffn1_norm_g, ffn1_w_in, ffn1_w_out, mix_norm_g, w_in, q_norm_g, k_norm_g,
           cmp_pe_k, cmp_w1_k, cmp_w2_k, cmp_pe_v, cmp_w1_v, cmp_w2_v, conv_w, conv_b,
           mem_norm_g, w_mem_kv, mem_q_norm_g, mem_k_norm_g, w_out,
           ffn2_norm_g, ffn2_w_in, ffn2_w_out, near_bias, cmp_bias):
    b, s, d = x.shape
    assert s % T == 0 and s % MERGE_TM == 0 and (b * s) % FFN_TM == 0
    assert WINDOW == 2 * T and REL_MAX_DIST <= T // 2
    assert SEL_BLOCK == 4 * CMP_STRIDE and CMP_BLOCK == 2 * CMP_STRIDE
    nq = s // T
    d_q = N_HEADS * HEAD_DIM
    d_kv = N_KV_GROUPS * HEAD_DIM
    d_conv = conv_w.shape[1]
    d_mem = w_mem_kv.shape[1] // 2

    x = _ffn(x.reshape(b * s, d), ffn1_norm_g, ffn1_w_in, ffn1_w_out).reshape(b, s, d)

    o = 0
    w_q = w_in[:, o:o + d_q]; o += d_q
    w_kc, w_vc, w_ks, w_vs, w_kw, w_vw = [w_in[:, o + i * d_kv:o + (i + 1) * d_kv] for i in range(6)]
    o += 6 * d_kv
    w_g = w_in[:, o:o + 3 * N_HEADS]; o += 3 * N_HEADS
    w_conv = w_in[:, o:o + 3 * d_conv]; o += 3 * d_conv
    w_qm = w_in[:, o:o + d_mem]; o += d_mem
    w_mg = w_in[:, o:]

    w_rows = jnp.concatenate([w_ks, w_kw, w_kc, w_vc, w_conv, w_qm, w_mg], axis=1).astype(BF16)
    widths = (d_kv, d_kv, d_kv, d_kv, 3 * d_conv, d_mem, w_mg.shape[1])
    group_of = jnp.arange(d_kv) // HEAD_DIM
    block_diag = (group_of[:, None] == group_of[None, :]).astype(F32) / HEAD_DIM
    k_gain_row = jnp.tile(k_norm_g, N_KV_GROUPS).reshape(1, d_kv)

    def rows_out(wd, dt):
        return (pl.BlockSpec((1, ROW_TM, wd), lambda i, j: (i, j, 0)), jax.ShapeDtypeStruct((b, s, wd), dt))

    def planes_out(wd):
        n_planes = wd // LANES
        return (pl.BlockSpec((1, n_planes, ROW_TM, LANES), lambda i, j: (i, 0, j, 0)),
                jax.ShapeDtypeStruct((b, n_planes, s, LANES), F32))

    specs = [planes_out(wd) if i in (2, 3) else rows_out(wd, BF16) for i, wd in enumerate(widths)]
    ks, kw, kc, vc, conv_in, q_mem, merge_g = pl.pallas_call(
        functools.partial(_proj_rows_kernel, widths=widths),
        grid=(b, s // ROW_TM),
        in_specs=[pl.BlockSpec((1, ROW_TM, d), lambda i, j: (i, j, 0)),
                  _resident((1, d)), _resident(w_rows.shape), _resident((d_kv, d_kv)), _resident((1, d_kv))],
        out_specs=[sp[0] for sp in specs],
        out_shape=[sp[1] for sp in specs],
        compiler_params=_params(2),
        name="proj_rows",
    )(x, mix_norm_g.reshape(1, d), w_rows, block_diag, k_gain_row)

    w_g_t = w_g.reshape(d, N_HEADS, 3).transpose(2, 1, 0).reshape(3 * N_HEADS, d)
    n_gate_rows = 128
    w_g_t = jnp.pad(w_g_t, ((0, n_gate_rows - 3 * N_HEADS), (0, 0)))
    w_t = jnp.concatenate([w_q.T, w_vs.T, w_vw.T, w_g_t], axis=0).astype(BF16)

    def t_out(rows, dt):
        return (pl.BlockSpec((1, 1, rows, T), lambda i, j: (i, j, 0, 0)),
                jax.ShapeDtypeStruct((b, nq, rows, T), dt))

    v_rows = N_KV_GROUPS * V_ROWS
    t_specs = [t_out(d_q, BF16), t_out(v_rows, BF16), t_out(v_rows, BF16), t_out(n_gate_rows, F32)]
    q_t, vs_t, vw_t, gates_t = pl.pallas_call(
        functools.partial(_proj_t_kernel, d_q=d_q, d_kv=d_kv),
        grid=(b, nq),
        in_specs=[pl.BlockSpec((1, T, d), lambda i, j: (i, j, 0)),
                  _resident((1, d)), _resident(w_t.shape), _resident((HEAD_DIM, 1))],
        out_specs=[sp[0] for sp in t_specs],
        out_shape=[sp[1] for sp in t_specs],
        compiler_params=_params(2),
        name="proj_t",
    )(x, mix_norm_g.reshape(1, d), w_t, q_norm_g.reshape(HEAD_DIM, 1))

    k_cmp = _compress(kc, cmp_pe_k, cmp_w1_k, cmp_w2_k, k_norm_g, True)
    v_cmp_t = _compress(vc, cmp_pe_v, cmp_w1_v, cmp_w2_v, k_norm_g, False)

    o_nsa = _nsa(q_t, ks.reshape(b, nq, T, d_kv), vs_t.reshape(b, nq, N_KV_GROUPS, V_ROWS, T),
                 kw.reshape(b, nq, T, d_kv), vw_t.reshape(b, nq, N_KV_GROUPS, V_ROWS, T),
                 k_cmp, v_cmp_t, gates_t, near_bias, cmp_bias, s)

    o_mem = _mem_attention(q_mem, mem, mem_norm_g, w_mem_kv, mem_q_norm_g, mem_k_norm_g)
    x = _merge(x, o_nsa, o_mem, conv_in, merge_g, conv_w, conv_b, w_out)
    x = _ffn(x.reshape(b * s, d), ffn2_norm_g, ffn2_w_in, ffn2_w_out).reshape(b, s, d)
    return x


def kernel(x, mem, ffn1_norm_g, ffn1_w_in, ffn1_w_out, mix_norm_g, w_in, q_norm_g, k_norm_g, cmp_pe_k, cmp_w1_k, cmp_w2_k, cmp_pe_v, cmp_w1_v, cmp_w2_v, conv_w, conv_b, mem_norm_g, w_mem_kv, mem_q_norm_g, mem_k_norm_g, w_out, ffn2_norm_g, ffn2_w_in, ffn2_w_out, rel_bias):
    near_bias, cmp_bias = _bias_tiles(rel_bias)
    for l in range(ffn1_norm_g.shape[0]):
        x = _layer(x, mem, ffn1_norm_g[l], ffn1_w_in[l], ffn1_w_out[l], mix_norm_g[l], w_in[l],
                   q_norm_g[l], k_norm_g[l], cmp_pe_k[l], cmp_w1_k[l], cmp_w2_k[l],
                   cmp_pe_v[l], cmp_w1_v[l], cmp_w2_v[l], conv_w[l], conv_b[l],
                   mem_norm_g[l], w_mem_kv[l], mem_q_norm_g[l], mem_k_norm_g[l], w_out[l],
                   ffn2_norm_g[l], ffn2_w_in[l], ffn2_w_out[l], near_bias, cmp_bias)
    return x
```

```python
import functools
import math

import jax
import jax.numpy as jnp
from jax import lax
from jax.experimental import pallas as pl
from jax.experimental.pallas import tpu as pltpu

N_HEADS = 16
HEAD_DIM = 64
N_KV_GROUPS = 4
HEADS_PER_GROUP = N_HEADS // N_KV_GROUPS
GROUP_WIDTH = HEADS_PER_GROUP * HEAD_DIM
CMP_BLOCK = 32
CMP_STRIDE = 16
SEL_BLOCK = 64
N_SELECT = 16
WINDOW = 512
FORCE = 1e4
CONV_WIDTH = 3
MEM_HEADS = 4
REL_BUCKETS = 32
REL_MAX_DIST = 128
EPS = 1e-6
NEG = -1e30

LANES = 128
SUBLANES = 8
BF16_SUBLANES = 16
MXU_COLS = 256
VMEM_BYTES = 64 * 1024 * 1024

T = 256
BLK_PER_T = T // SEL_BLOCK
CMP_PER_T = T // CMP_STRIDE
CMP_NEAR = 2 * CMP_PER_T
N_NEAR = WINDOW // T + 1
N_FORCED = 3
N_BUF = 4
PREFETCH = N_BUF // 2
LONG_TRIP = 2 * N_BUF
CMP_CHUNK = LANES
FFN_TM = 512
ROW_TM = 256
MEM_TM = 512
MERGE_TM = 512
HALO = BF16_SUBLANES
GROUPS_PER_PLANE = LANES // HEAD_DIM
V_ROWS = HEAD_DIM + BF16_SUBLANES
LOG2E = math.log2(math.e)
VMEM_LIMIT = VMEM_BYTES - 12 * 1024 * 1024

F32 = jnp.float32
BF16 = jnp.bfloat16
HI = lax.Precision.HIGHEST


def _dot(a, b):
    return jnp.dot(a, b, preferred_element_type=F32)


def _dot_nt(a, b):
    return lax.dot_general(a, b, (((1,), (1,)), ((), ())), preferred_element_type=F32)


def _rms_rows(xf, g):
    return xf * lax.rsqrt(jnp.mean(xf * xf, axis=-1, keepdims=True) + EPS) * g


def _sigmoid(x):
    return 1.0 / (1.0 + jnp.exp(-x))


def _resident(shape):
    zeros = (0,) * len(shape)
    return pl.BlockSpec(shape, lambda *_: zeros, pipeline_mode=pl.Buffered(1))


def _params(n_axes):
    return pltpu.CompilerParams(dimension_semantics=("arbitrary",) * n_axes,
                                vmem_limit_bytes=VMEM_LIMIT)


def _ffn_kernel(x_ref, g_ref, wa_ref, wb_ref, wo_ref, o_ref, *, ff_bounds):
    x = x_ref[...]
    h = _rms_rows(x, g_ref[...]).astype(BF16)
    acc = jnp.zeros(x.shape, F32)
    for lo, hi in zip(ff_bounds[:-1], ff_bounds[1:]):
        a = _dot(h, wa_ref[:, lo:hi])
        b = _dot(h, wb_ref[:, lo:hi])
        z = (a * _sigmoid(a) * b).astype(BF16)
        acc = acc + _dot(z, wo_ref[lo:hi, :])
    o_ref[...] = x + 0.5 * acc


def _ffn(x2d, g, w_in, w_out):
    n, d = x2d.shape
    d_ff = w_out.shape[0]
    wa = w_in[:, :d_ff].astype(BF16)
    wb = w_in[:, d_ff:].astype(BF16)
    wo = w_out.astype(BF16)
    n_col_tiles = -(-d_ff // MXU_COLS)
    ff_bounds = (0, min(d_ff, (n_col_tiles + 1) // 2 * MXU_COLS), d_ff)
    return pl.pallas_call(
        functools.partial(_ffn_kernel, ff_bounds=ff_bounds),
        grid=(n // FFN_TM,),
        in_specs=[pl.BlockSpec((FFN_TM, d), lambda i: (i, 0)),
                  _resident((1, d)), _resident((d, d_ff)), _resident((d, d_ff)), _resident((d_ff, d))],
        out_specs=pl.BlockSpec((FFN_TM, d), lambda i: (i, 0)),
        out_shape=jax.ShapeDtypeStruct((n, d), F32),
        compiler_params=_params(1),
        name="ffn",
    )(x2d, g.reshape(1, d), wa, wb, wo)


def _proj_rows_kernel(x_ref, g_ref, w_ref, bd_ref, kg_ref,
                      ks_o, kw_o, kc_o, vc_o, conv_o, qm_o, mg_o, *, widths):
    h = _rms_rows(x_ref[0], g_ref[...]).astype(BF16)

    def knorm(k):
        ms = jnp.dot(k * k, bd_ref[...], precision=HI, preferred_element_type=F32)
        return (k * lax.rsqrt(ms + EPS) * kg_ref[...]).astype(BF16)

    lo = 0
    outs = (ks_o, kw_o, kc_o, vc_o, conv_o, qm_o, mg_o)
    for idx, (o_ref, wd) in enumerate(zip(outs, widths)):
        y = _dot(h, w_ref[:, lo:lo + wd])
        if idx < 2:
            o_ref[0] = knorm(y)
        elif idx < 4:
            for plane in range(wd // LANES):
                o_ref[0, plane] = y[:, plane * LANES:(plane + 1) * LANES]
        else:
            o_ref[0] = y.astype(o_ref.dtype)
        lo += wd


def _proj_t_kernel(x_ref, g_ref, wt_ref, qg_ref, q_o, vs_o, vw_o, gt_o, *, d_q, d_kv):
    h = _rms_rows(x_ref[0], g_ref[...]).astype(BF16)
    qg = qg_ref[...] * (HEAD_DIM ** -0.5 * LOG2E)
    out_t = _dot_nt(wt_ref[...], h)
    for hd in range(d_q // HEAD_DIM):
        q = out_t[hd * HEAD_DIM:(hd + 1) * HEAD_DIM, :]
        qn = q * lax.rsqrt(jnp.mean(q * q, axis=0, keepdims=True) + EPS) * qg
        q_o[0, 0, hd * HEAD_DIM:(hd + 1) * HEAD_DIM, :] = qn.astype(BF16)
    lo = d_q
    for v_o in (vs_o, vw_o):
        y = out_t[lo:lo + d_kv, :].astype(BF16)
        for grp in range(d_kv // HEAD_DIM):
            v_o[0, 0, grp * V_ROWS:grp * V_ROWS + HEAD_DIM, :] = y[grp * HEAD_DIM:(grp + 1) * HEAD_DIM]
            v_o[0, 0, grp * V_ROWS + HEAD_DIM:(grp + 1) * V_ROWS, :] = jnp.ones((V_ROWS - HEAD_DIM, T), BF16)
        lo += d_kv
    gt_o[0, 0] = _sigmoid(out_t[lo:, :])


def _compress_kernel(c_ref, pe_ref, w1_ref, w2_ref, kg_ref, o_ref, *, is_key, hidden):
    n_chunks = c_ref.shape[2] // CMP_STRIDE
    a = jnp.zeros((n_chunks, GROUPS_PER_PLANE * hidden), F32)
    b = jnp.zeros((n_chunks, GROUPS_PER_PLANE * hidden), F32)
    for pos in range(CMP_STRIDE):
        x = c_ref[0, 0, pl.ds(pos, n_chunks, stride=CMP_STRIDE), :]
        a = a + _dot((x + pe_ref[pos:pos + 1, :]).astype(BF16), w1_ref[pos])
        b = b + _dot((x + pe_ref[CMP_STRIDE + pos:CMP_STRIDE + pos + 1, :]).astype(BF16), w1_ref[CMP_STRIDE + pos])
    hid = a + pltpu.roll(b, n_chunks - 1, 0)
    hid = (hid * _sigmoid(hid)).astype(BF16)
    for grp in range(GROUPS_PER_PLANE):
        hid_g = hid[:, grp * hidden:(grp + 1) * hidden]
        if is_key:
            y = _dot(hid_g, w2_ref[...])
            y = _rms_rows(y, kg_ref[...])
            row = lax.broadcasted_iota(jnp.int32, y.shape, 0)
            o_ref[0, grp] = jnp.where(row < n_chunks - 1, y, 0.0).astype(BF16)
        else:
            y = _dot_nt(w2_ref[...], hid_g)
            col = lax.broadcasted_iota(jnp.int32, y.shape, 1)
            o_ref[0, grp] = jnp.where(col < n_chunks - 1, y, 0.0).astype(BF16)


def _compress(c_planes, pe, w1, w2, k_gain, is_key):
    b, planes, s, _ = c_planes.shape
    n_chunks = s // CMP_STRIDE
    hidden = w1.shape[1]
    w1_pos = w1.reshape(CMP_BLOCK, HEAD_DIM, hidden)
    zeros = jnp.zeros_like(w1_pos)
    w1_bd = jnp.concatenate([jnp.concatenate([w1_pos, zeros], axis=2),
                             jnp.concatenate([zeros, w1_pos], axis=2)], axis=1).astype(BF16)
    pe_planes = jnp.tile(pe, (1, GROUPS_PER_PLANE))
    w2_arg = w2.astype(BF16) if is_key else w2.T.astype(BF16)
    out_tail = (n_chunks, HEAD_DIM) if is_key else (HEAD_DIM, n_chunks)
    return pl.pallas_call(
        functools.partial(_compress_kernel, is_key=is_key, hidden=hidden),
        grid=(b, planes),
        in_specs=[pl.BlockSpec((1, 1, s, LANES), lambda i, j: (i, j, 0, 0)),
                  _resident(pe_planes.shape), _resident(w1_bd.shape),
                  _resident(w2_arg.shape), _resident((1, HEAD_DIM))],
        out_specs=pl.BlockSpec((1, GROUPS_PER_PLANE) + out_tail, lambda i, j: (i, j, 0, 0)),
        out_shape=jax.ShapeDtypeStruct((b, planes * GROUPS_PER_PLANE) + out_tail, BF16),
        compiler_params=_params(2),
        name="compress_k" if is_key else "compress_v",
    )(c_planes, pe_planes, w1_bd, w2_arg, k_gain.reshape(1, HEAD_DIM))


def _bias_kernel(rb_ref, bkt_near_ref, bkt_cmp_ref, near_o, cmp_o):
    h = pl.program_id(0)
    far = rb_ref[REL_BUCKETS - 1, h]

    def lookup(bkt):
        out = jnp.zeros(bkt.shape, F32)
        for k in range(REL_BUCKETS - 1):
            out = jnp.where(bkt == k, (rb_ref[k, h] - far) * LOG2E, out)
        return out

    key = lax.broadcasted_iota(jnp.int32, (T, T), 0)
    qry = lax.broadcasted_iota(jnp.int32, (T, T), 1)
    near_o[0, 0] = jnp.where(qry >= key, lookup(bkt_near_ref[0]), NEG)
    near_o[0, 1] = lookup(bkt_near_ref[1])
    near_o[0, 2] = jnp.where(key > qry, 0.0, NEG)
    cmp_o[0] = lookup(bkt_cmp_ref[...])


def _rel_bucket(dist):
    n = jnp.maximum(dist, 0)
    max_exact = REL_BUCKETS // 2
    nf = jnp.maximum(n, 1).astype(F32)
    large = max_exact + (jnp.log(nf / max_exact) / math.log(REL_MAX_DIST / max_exact)
                         * (REL_BUCKETS - max_exact)).astype(jnp.int32)
    large = jnp.minimum(large, REL_BUCKETS - 1)
    return jnp.where(n < max_exact, n, large)


def _bias_tiles(rel_bias):
    key = jnp.arange(T)[:, None]
    qry = jnp.arange(T)[None, :]
    bkt_near = jnp.stack([_rel_bucket(qry - key), _rel_bucket(qry - key + T)]).astype(jnp.int32)
    j = jnp.arange(CMP_NEAR)[:, None]
    bkt_cmp = _rel_bucket(qry - CMP_STRIDE * (j - CMP_PER_T) - (CMP_BLOCK - 1)).astype(jnp.int32)
    return pl.pallas_call(
        _bias_kernel,
        grid=(N_HEADS,),
        in_specs=[pl.BlockSpec(memory_space=pltpu.SMEM),
                  pl.BlockSpec((2, T, T), lambda h: (0, 0, 0)),
                  pl.BlockSpec((CMP_NEAR, T), lambda h: (0, 0))],
        out_specs=[pl.BlockSpec((1, N_NEAR, T, T), lambda h: (h // HEADS_PER_GROUP, 0, 0, h % HEADS_PER_GROUP)),
                   pl.BlockSpec((1, CMP_NEAR, T), lambda h: (h // HEADS_PER_GROUP, 0, h % HEADS_PER_GROUP))],
        out_shape=[jax.ShapeDtypeStruct((N_KV_GROUPS, N_NEAR, T, HEADS_PER_GROUP * T), F32),
                   jax.ShapeDtypeStruct((N_KV_GROUPS, CMP_NEAR, HEADS_PER_GROUP * T), F32)],
        compiler_params=_params(1),
        name="bias_tiles",
    )(rel_bias, bkt_near, bkt_cmp)


def _nsa_kernel(q_ref, ks_ref, vs_ref, kw_ref, vw_ref, kc_ref, vc_ref, gt_ref, nb_ref, cb_ref,
                o_ref,
                qcat_ref, sc_ref, psum_ref, selneg_ref, oc_ref, *rest, n_blk, n_cmp):
    s_bufs, rest = rest[:N_BUF], rest[N_BUF:]
    p_bufs, rest = rest[:N_BUF], rest[N_BUF:]
    w_bufs, rest = rest[:N_NEAR], rest[N_NEAR:]
    ms_ref, accs_ref, mw_ref, accw_ref = rest
    g = pl.program_id(1)
    qi = pl.program_id(2)
    R = HEADS_PER_GROUP
    W = R * T
    qry = lax.broadcasted_iota(jnp.int32, (1, W), 1) & (T - 1)
    t = qi * T + qry

    def q_head(r):
        return q_ref[0, 0, r * HEAD_DIM:(r + 1) * HEAD_DIM, :]

    rowgrp = lax.shift_right_logical(lax.broadcasted_iota(jnp.int32, (GROUP_WIDTH, T), 0),
                                     int(math.log2(HEAD_DIM)))
    for r in range(R):
        q4 = jnp.concatenate([q_head(r).astype(F32)] * N_KV_GROUPS, axis=0)
        qcat_ref[:, r * T:(r + 1) * T] = jnp.where(rowgrp == g, q4, 0.0).astype(BF16)

    for m_ref, acc_ref in ((ms_ref, accs_ref), (mw_ref, accw_ref)):
        m_ref[...] = jnp.full(m_ref.shape, NEG, F32)
        acc_ref[...] = jnp.zeros(acc_ref.shape, F32)

    def softmax_tile(s_ref, p_ref, m_ref, rows):
        top = None
        for j in range(BLK_PER_T):
            blk_max = s_ref[j * SEL_BLOCK:(j + 1) * SEL_BLOCK, :].reshape(SEL_BLOCK // SUBLANES, SUBLANES, W).max(axis=0)
            if rows is not None:
                blk_max = blk_max + rows[j]
            top = blk_max if top is None else jnp.maximum(top, blk_max)
        m_old = m_ref[...]
        m_new = jnp.maximum(m_old, jnp.max(top, axis=0, keepdims=True))
        for j in range(BLK_PER_T):
            shift = m_new if rows is None else m_new - rows[j]
            sl = slice(j * SEL_BLOCK, (j + 1) * SEL_BLOCK)
            p_ref[sl, :] = jnp.exp2(s_ref[sl, :] - shift).astype(BF16)
        m_ref[...] = m_new
        return jnp.exp2(m_old - m_new)

    neg_row = jnp.full((1, W), NEG, F32)

    def tile_of(y):
        return jnp.where(y == 0, qi, jnp.where(y == 1, jnp.maximum(qi - 1, 0), jnp.clip(y - 2, 0, qi)))

    win_tiles = [jnp.maximum(qi - back, 0) for back in range(N_NEAR)]
    for back in range(N_NEAR):
        w_bufs[back][...] = _dot(kw_ref[0, win_tiles[back]], qcat_ref[...]) + nb_ref[0, back]
    near0 = pl.multiple_of(qi * CMP_PER_T, CMP_PER_T)
    sc_ref[0:CMP_PER_T, :] = jnp.zeros((CMP_PER_T, W), F32)
    sc_ref[CMP_PER_T:, :] = _dot(kc_ref[0, 0], jnp.concatenate([q_head(r) for r in range(R)], axis=1))
    sc_ref[pl.ds(near0, CMP_NEAR), :] = sc_ref[pl.ds(near0, CMP_NEAR), :] + cb_ref[0]
    for y in range(PREFETCH):
        s = _dot(ks_ref[0, tile_of(y)], qcat_ref[...])
        s_bufs[y][...] = s + nb_ref[0, y] if y < 2 else s

    for back in range(N_NEAR):
        rows = None if back == 0 else [jnp.where(qi >= back, 0.0, neg_row)] * BLK_PER_T
        alpha = softmax_tile(w_bufs[back], p_bufs[back], mw_ref, rows)
        accw_ref[...] = alpha * accw_ref[...] + _dot(vw_ref[0, win_tiles[back], 0], p_bufs[back][...])

    def select_blocks(nc, nb):
        last_c = jnp.minimum(lax.shift_right_arithmetic(t - (CMP_BLOCK - 1), int(math.log2(CMP_STRIDE))),
                             n_cmp - 2)
        c_idx = lax.broadcasted_iota(jnp.int32, (nc, W), 0)
        s = jnp.where(c_idx <= last_c, sc_ref[CMP_PER_T:CMP_PER_T + nc, :], NEG)
        m = jnp.max(s, axis=0, keepdims=True)
        p = jnp.exp2(s - m)
        l = jnp.sum(p, axis=0, keepdims=True)
        p = p * jnp.where(last_c >= 0, 1.0 / l, 0.0)
        oc_ref[...] = _dot(vc_ref[0, 0, :, 0:nc], p.astype(BF16))
        psum = p[:, 0:T]
        for r in range(1, R):
            psum = psum + p[:, r * T:(r + 1) * T]

        for ln in range(T // LANES):
            psum_ref[ln, 0:8, :] = jnp.zeros((8, LANES), F32)
            psum_ref[ln, 8:8 + nc, :] = psum[:, ln * LANES:(ln + 1) * LANES]

        def every_fourth(off):
            return jnp.concatenate([psum_ref[ln, pl.ds(8 + off, nb, stride=SEL_BLOCK // CMP_STRIDE), :]
                                    for ln in range(T // LANES)], axis=1)

        imp = every_fourth(0) + every_fourth(1) + every_fourth(2) + 0.5 * (every_fourth(3) + every_fourth(-1))
        blk = lax.broadcasted_iota(jnp.int32, (nb, T), 0)
        cur = lax.shift_right_logical(t[:, 0:T], int(math.log2(SEL_BLOCK)))
        forced = (blk == 0) | (blk == cur) | (blk == cur - 1)
        score = jnp.where(forced, -jnp.inf, jnp.where(blk <= cur, imp, -FORCE))
        for _ in range(min(N_SELECT, n_blk) - N_FORCED):
            best = jnp.max(score, axis=0, keepdims=True)
            first = jnp.min(jnp.where(score == best, blk, nb), axis=0, keepdims=True)
            score = jnp.where(blk == first, -jnp.inf, score)
        selneg = jnp.where(score == -jnp.inf, 0.0, NEG)
        selneg_ref[0:nb, :] = jnp.concatenate([selneg] * R, axis=1)

    nq = n_cmp // CMP_PER_T
    n_var = max(1, n_cmp // CMP_CHUNK)
    tiles_per_var = nq // n_var
    for v in range(n_var):
        @pl.when((qi >= v * tiles_per_var) & (qi < (v + 1) * tiles_per_var))
        def _(v=v):
            n_tiles = (v + 1) * tiles_per_var
            select_blocks(n_tiles * CMP_PER_T, n_tiles * BLK_PER_T)

    def sel_rows(y):
        kj = tile_of(y)
        return [selneg_ref[pl.ds(kj * BLK_PER_T + j, 1), :] for j in range(BLK_PER_T)]

    p_bufs[N_BUF - 1][...] = jnp.zeros(p_bufs[0].shape, BF16)

    def positions(first, count, final=False):
        for k in range(count):
            y = first + k
            pv = _dot(vs_ref[0, tile_of(y - 1), 0], p_bufs[(k - 1) % N_BUF][...])
            if not final or k + PREFETCH < count:
                s_bufs[(k + PREFETCH) % N_BUF][...] = _dot(ks_ref[0, tile_of(y + PREFETCH)], qcat_ref[...])
            alpha = softmax_tile(s_bufs[k % N_BUF], p_bufs[k % N_BUF], ms_ref, sel_rows(y))
            accs_ref[...] = alpha * (accs_ref[...] + pv)

    n_long = lax.div(qi, LONG_TRIP)
    n_short = lax.div(qi - n_long * LONG_TRIP, N_BUF)

    def long_trip(i, carry):
        positions(LONG_TRIP * i, LONG_TRIP)
        return carry

    def short_trip(i, carry):
        positions(LONG_TRIP * n_long + N_BUF * i, N_BUF)
        return carry

    lax.fori_loop(0, n_long, long_trip, 0)
    lax.fori_loop(0, n_short, short_trip, 0)
    final_first = LONG_TRIP * n_long + N_BUF * n_short
    for count in range(1, N_BUF + 1):
        @pl.when(qi + 1 - final_first == count)
        def _(count=count):
            positions(final_first, count, final=True)
            accs_ref[...] = accs_ref[...] + _dot(vs_ref[0, tile_of(final_first + count - 1), 0],
                                                 p_bufs[count - 1][...])

    o_s = accs_ref[0:HEAD_DIM, :] / accs_ref[HEAD_DIM:HEAD_DIM + 1, :]
    o_w = accw_ref[0:HEAD_DIM, :] / accw_ref[HEAD_DIM:HEAD_DIM + 1, :]
    outs = []
    for r in range(R):
        head = g * R + r
        sl = slice(r * T, (r + 1) * T)
        g_c = gt_ref[0, 0, pl.ds(head, 1), :]
        g_s = gt_ref[0, 0, pl.ds(N_HEADS + head, 1), :]
        g_w = gt_ref[0, 0, pl.ds(2 * N_HEADS + head, 1), :]
        outs.append(g_c * oc_ref[:, sl] + g_s * o_s[:, sl] + g_w * o_w[:, sl])
    o_ref[0] = jnp.concatenate(outs, axis=0).T.astype(o_ref.dtype)


def _nsa(q_t, ks, vs_t, kw, vw_t, kc, vc_t, gates_t, near_bias, cmp_bias, seq):
    b, nq = q_t.shape[0], q_t.shape[1]
    n_blk = seq // SEL_BLOCK
    n_cmp = seq // CMP_STRIDE
    W = HEADS_PER_GROUP * T
    kv_spec = pl.BlockSpec((1, nq, T, GROUP_WIDTH), lambda i, j, k: (i, 0, 0, 0))
    vt_spec = pl.BlockSpec((1, nq, 1, V_ROWS, T), lambda i, j, k: (i, 0, j, 0, 0))
    stat = pltpu.VMEM((1, W), F32)
    acc = pltpu.VMEM((V_ROWS, W), F32)
    scores = pltpu.VMEM((T, W), F32)
    probs = pltpu.VMEM((T, W), BF16)
    return pl.pallas_call(
        functools.partial(_nsa_kernel, n_blk=n_blk, n_cmp=n_cmp),
        grid=(b, N_KV_GROUPS, nq),
        in_specs=[pl.BlockSpec((1, 1, GROUP_WIDTH, T), lambda i, j, k: (i, k, j, 0)),
                  kv_spec, vt_spec, kv_spec, vt_spec,
                  pl.BlockSpec((1, 1, n_cmp, HEAD_DIM), lambda i, j, k: (i, j, 0, 0)),
                  pl.BlockSpec((1, 1, HEAD_DIM, n_cmp), lambda i, j, k: (i, j, 0, 0)),
                  pl.BlockSpec((1, 1, gates_t.shape[2], T), lambda i, j, k: (i, k, 0, 0)),
                  pl.BlockSpec((1, N_NEAR, T, W), lambda i, j, k: (j, 0, 0, 0)),
                  pl.BlockSpec((1, CMP_NEAR, W), lambda i, j, k: (j, 0, 0))],
        out_specs=pl.BlockSpec((1, T, GROUP_WIDTH), lambda i, j, k: (i, k, j)),
        out_shape=jax.ShapeDtypeStruct((b, seq, N_HEADS * HEAD_DIM), BF16),
        scratch_shapes=[pltpu.VMEM((GROUP_WIDTH, W), BF16),
                        pltpu.VMEM((n_cmp + CMP_PER_T, W), F32),
                        pltpu.VMEM((T // LANES, n_cmp + 8, LANES), F32),
                        pltpu.VMEM((n_blk, W), F32),
                        pltpu.VMEM((HEAD_DIM, W), F32), *([scores] * N_BUF), *([probs] * N_BUF), *([scores] * N_NEAR),
                        stat, acc, stat, acc],
        compiler_params=_params(3),
        name="nsa",
    )(q_t, ks, vs_t, kw, vw_t, kc, vc_t, gates_t, near_bias, cmp_bias)


def _mem_kv_kernel(mem_ref, g_ref, w_ref, kg_ref, k_o, v_o):
    h = _rms_rows(mem_ref[0], g_ref[...]).astype(BF16)
    width = k_o.shape[2]
    hd = width // MEM_HEADS
    k = _dot(h, w_ref[:, :width])
    for i in range(MEM_HEADS):
        k_o[0, :, i * hd:(i + 1) * hd] = _rms_rows(k[:, i * hd:(i + 1) * hd], kg_ref[...]).astype(BF16)
    v_o[0] = _dot(h, w_ref[:, width:]).astype(BF16)


def _mem_attn_kernel(q_ref, k_ref, v_ref, qg_ref, o_ref):
    width = q_ref.shape[2]
    hd = width // MEM_HEADS
    for i in range(MEM_HEADS):
        sl = slice(i * hd, (i + 1) * hd)
        q = (_rms_rows(q_ref[0, :, sl].astype(F32), qg_ref[...]) * (hd ** -0.5)).astype(BF16)
        s = _dot_nt(q, k_ref[0, :, sl])
        p = jnp.exp(s - jnp.max(s, axis=-1, keepdims=True))
        p = p / jnp.sum(p, axis=-1, keepdims=True)
        o_ref[0, :, sl] = _dot(p.astype(BF16), v_ref[0, :, sl]).astype(o_ref.dtype)


def _mem_attention(q_mem, mem, mem_norm_g, w_mem_kv, q_g, k_g):
    b, s, width = q_mem.shape
    n_mem, d = mem.shape[1], mem.shape[2]
    hd = width // MEM_HEADS
    kv_shape = jax.ShapeDtypeStruct((b, n_mem, width), BF16)
    kv_block = pl.BlockSpec((1, n_mem, width), lambda i: (i, 0, 0))
    km, vm = pl.pallas_call(
        _mem_kv_kernel,
        grid=(b,),
        in_specs=[pl.BlockSpec((1, n_mem, d), lambda i: (i, 0, 0)),
                  _resident((1, d)), _resident((d, 2 * width)), _resident((1, hd))],
        out_specs=[kv_block, kv_block],
        out_shape=[kv_shape, kv_shape],
        compiler_params=_params(1),
        name="mem_kv",
    )(mem, mem_norm_g.reshape(1, d), w_mem_kv.astype(BF16), k_g.reshape(1, hd))
    kv_block2 = pl.BlockSpec((1, n_mem, width), lambda i, j: (i, 0, 0))
    return pl.pallas_call(
        _mem_attn_kernel,
        grid=(b, s // MEM_TM),
        in_specs=[pl.BlockSpec((1, MEM_TM, width), lambda i, j: (i, j, 0)),
                  kv_block2, kv_block2, _resident((1, hd))],
        out_specs=pl.BlockSpec((1, MEM_TM, width), lambda i, j: (i, j, 0)),
        out_shape=jax.ShapeDtypeStruct((b, s, width), BF16),
        compiler_params=_params(2),
        name="mem_attn",
    )(q_mem, km, vm, q_g.reshape(1, hd))


def _merge_kernel(x_ref, nsa_ref, mem_ref, cb_ref, cc_ref, cx_ref, hc_ref, hx_ref,
                  g1_ref, g2_ref, g3_ref, cw_ref, bias_ref, wo_ref, o_ref):
    j = pl.program_id(1)

    def f32(ref):
        return ref[0].astype(F32)

    u = f32(cc_ref) * f32(cx_ref)
    halo = jnp.where(j > 0, f32(hc_ref) * f32(hx_ref), 0.0)
    prev1 = halo[HALO - 1:HALO, :]
    prev2 = halo[HALO - 2:HALO - 1, :]
    row = lax.broadcasted_iota(jnp.int32, u.shape, 0)
    u1 = jnp.where(row == 0, prev1, pltpu.roll(u, 1, 0))
    u2 = jnp.where(row == 0, prev2, jnp.where(row == 1, prev1, pltpu.roll(u, 2, 0)))
    y = cw_ref[0:1, :] * u2 + cw_ref[1:2, :] * u1 + cw_ref[2:3, :] * u
    o_conv = f32(cb_ref) * (y + bias_ref[...])
    merged = (_sigmoid(f32(g1_ref)) * f32(nsa_ref) + _sigmoid(f32(g2_ref)) * o_conv
              + _sigmoid(f32(g3_ref)) * f32(mem_ref))
    o_ref[0] = x_ref[0] + _dot(merged.astype(BF16), wo_ref[...])


def _merge(x, o_nsa, o_mem, conv_in, merge_g, conv_w, conv_b, w_out):
    b, s, d = x.shape
    tm = MERGE_TM

    def col(c):
        return pl.BlockSpec((1, tm, d), lambda i, j: (i, j, c))

    def halo(c):
        return pl.BlockSpec((1, HALO, d), lambda i, j: (i, jnp.maximum(j * (tm // HALO) - 1, 0), c))

    return pl.pallas_call(
        _merge_kernel,
        grid=(b, s // tm),
        in_specs=[col(0), col(0), col(0),
                  col(0), col(1), col(2), halo(1), halo(2),
                  col(0), col(1), col(2),
                  _resident((CONV_WIDTH, d)), _resident((1, d)), _resident((d, d))],
        out_specs=col(0),
        out_shape=jax.ShapeDtypeStruct((b, s, d), F32),
        compiler_params=_params(2),
        name="merge",
    )(x, o_nsa, o_mem, conv_in, conv_in, conv_in, conv_in, conv_in,
      merge_g, merge_g, merge_g, conv_w, conv_b.reshape(1, d), w_out.astype(BF16))


def _layer(x, mem, ffn1_norm_g, ffn1_w_in, ffn1_w_out, mix_norm_g, w_in, q_norm_g, k_norm_g,
           cmp_pe_k, cmp_w1_k, cmp_w2_k, cmp_pe_v, cmp_w1_v, cmp_w2_v, conv_w, conv_b,
           mem_norm_g, w_mem_kv, mem_q_norm_g, mem_k_norm_g, w_out,
           ffn2_norm_g, ffn2_w_in, ffn2_w_out, near_bias, cmp_bias):
    b, s, d = x.shape
    assert s % T == 0 and s % MERGE_TM == 0 and (b * s) % FFN_TM == 0
    assert WINDOW == 2 * T and REL_MAX_DIST <= T // 2
    assert SEL_BLOCK == 4 * CMP_STRIDE and CMP_BLOCK == 2 * CMP_STRIDE
    assert N_SELECT > N_FORCED and N_BUF >= N_NEAR and 2 <= PREFETCH < N_BUF and s % ROW_TM == 0
    nq = s // T
    d_q = N_HEADS * HEAD_DIM
    d_kv = N_KV_GROUPS * HEAD_DIM
    d_conv = conv_w.shape[1]
    d_mem = w_mem_kv.shape[1] // 2

    x = _ffn(x.reshape(b * s, d), ffn1_norm_g, ffn1_w_in, ffn1_w_out).reshape(b, s, d)

    o = 0
    w_q = w_in[:, o:o + d_q]; o += d_q
    w_kc, w_vc, w_ks, w_vs, w_kw, w_vw = [w_in[:, o + i * d_kv:o + (i + 1) * d_kv] for i in range(6)]
    o += 6 * d_kv
    w_g = w_in[:, o:o + 3 * N_HEADS]; o += 3 * N_HEADS
    w_conv = w_in[:, o:o + 3 * d_conv]; o += 3 * d_conv
    w_qm = w_in[:, o:o + d_mem]; o += d_mem
    w_mg = w_in[:, o:]

    w_rows = jnp.concatenate([w_ks, w_kw, w_kc, w_vc, w_conv, w_qm, w_mg], axis=1).astype(BF16)
    widths = (d_kv, d_kv, d_kv, d_kv, 3 * d_conv, d_mem, w_mg.shape[1])
    group_of = jnp.arange(d_kv) // HEAD_DIM
    block_diag = (group_of[:, None] == group_of[None, :]).astype(F32) / HEAD_DIM
    k_gain_row = jnp.tile(k_norm_g, N_KV_GROUPS).reshape(1, d_kv)

    def rows_out(wd, dt):
        return (pl.BlockSpec((1, ROW_TM, wd), lambda i, j: (i, j, 0)), jax.ShapeDtypeStruct((b, s, wd), dt))

    def planes_out(wd):
        n_planes = wd // LANES
        return (pl.BlockSpec((1, n_planes, ROW_TM, LANES), lambda i, j: (i, 0, j, 0)),
                jax.ShapeDtypeStruct((b, n_planes, s, LANES), F32))

    specs = [planes_out(wd) if i in (2, 3) else rows_out(wd, BF16) for i, wd in enumerate(widths)]
    ks, kw, kc, vc, conv_in, q_mem, merge_g = pl.pallas_call(
        functools.partial(_proj_rows_kernel, widths=widths),
        grid=(b, s // ROW_TM),
        in_specs=[pl.BlockSpec((1, ROW_TM, d), lambda i, j: (i, j, 0)),
                  _resident((1, d)), _resident(w_rows.shape), _resident((d_kv, d_kv)), _resident((1, d_kv))],
        out_specs=[sp[0] for sp in specs],
        out_shape=[sp[1] for sp in specs],
        compiler_params=_params(2),
        name="proj_rows",
    )(x, mix_norm_g.reshape(1, d), w_rows, block_diag, k_gain_row)

    w_g_t = w_g.reshape(d, N_HEADS, 3).transpose(2, 1, 0).reshape(3 * N_HEADS, d)
    n_gate_rows = 128
    w_g_t = jnp.pad(w_g_t, ((0, n_gate_rows - 3 * N_HEADS), (0, 0)))
    w_t = jnp.concatenate([w_q.T, w_vs.T, w_vw.T, w_g_t], axis=0).astype(BF16)

    def t_out(rows, dt):
        return (pl.BlockSpec((1, 1, rows, T), lambda i, j: (i, j, 0, 0)),
                jax.ShapeDtypeStruct((b, nq, rows, T), dt))

    v_rows = N_KV_GROUPS * V_ROWS
    t_specs = [t_out(d_q, BF16), t_out(v_rows, BF16), t_out(v_rows, BF16), t_out(n_gate_rows, F32)]
    q_t, vs_t, vw_t, gates_t = pl.pallas_call(
        functools.partial(_proj_t_kernel, d_q=d_q, d_kv=d_kv),
        grid=(b, nq),
        in_specs=[pl.BlockSpec((1, T, d), lambda i, j: (i, j, 0)),
                  _resident((1, d)), _resident(w_t.shape), _resident((HEAD_DIM, 1))],
        out_specs=[sp[0] for sp in t_specs],
        out_shape=[sp[1] for sp in t_specs],
        compiler_params=_params(2),
        name="proj_t",
    )(x, mix_norm_g.reshape(1, d), w_t, q_norm_g.reshape(HEAD_DIM, 1))

    k_cmp = _compress(kc, cmp_pe_k, cmp_w1_k, cmp_w2_k, k_norm_g, True)
    v_cmp_t = _compress(vc, cmp_pe_v, cmp_w1_v, cmp_w2_v, k_norm_g, False)

    o_nsa = _nsa(q_t, ks.reshape(b, nq, T, d_kv), vs_t.reshape(b, nq, N_KV_GROUPS, V_ROWS, T),
                 kw.reshape(b, nq, T, d_kv), vw_t.reshape(b, nq, N_KV_GROUPS, V_ROWS, T),
                 k_cmp, v_cmp_t, gates_t, near_bias, cmp_bias, s)

    o_mem = _mem_attention(q_mem, mem, mem_norm_g, w_mem_kv, mem_q_norm_g, mem_k_norm_g)
    x = _merge(x, o_nsa, o_mem, conv_in, merge_g, conv_w, conv_b, w_out)
    x = _ffn(x.reshape(b * s, d), ffn2_norm_g, ffn2_w_in, ffn2_w_out).reshape(b, s, d)
    return x


def kernel(x, mem, ffn1_norm_g, ffn1_w_in, ffn1_w_out, mix_norm_g, w_in, q_norm_g, k_norm_g, cmp_pe_k, cmp_w1_k, cmp_w2_k, cmp_pe_v, cmp_w1_v, cmp_w2_v, conv_w, conv_b, mem_norm_g, w_mem_kv, mem_q_norm_g, mem_k_norm_g, w_out, ffn2_norm_g, ffn2_w_in, ffn2_w_out, rel_bias):
    near_bias, cmp_bias = _bias_tiles(rel_bias)
    for l in range(ffn1_norm_g.shape[0]):
        x = _layer(x, mem, ffn1_norm_g[l], ffn1_w_in[l], ffn1_w_out[l], mix_norm_g[l], w_in[l],
                   q_norm_g[l], k_norm_g[l], cmp_pe_k[l], cmp_w1_k[l], cmp_w2_k[l],
                   cmp_pe_v[l], cmp_w1_v[l], cmp_w2_v[l], conv_w[l], conv_b[l],
                   mem_norm_g[l], w_mem_kv[l], mem_q_norm_g[l], mem_k_norm_g[l], w_out[l],
                   ffn2_norm_g[l], ffn2_w_in[l], ffn2_w_out[l], near_bias, cmp_bias)
    return x
```

```python
import functools
import math

import jax
import jax.numpy as jnp
from jax import lax
from jax.experimental import pallas as pl
from jax.experimental.pallas import tpu as pltpu

N_HEADS = 16
HEAD_DIM = 64
N_KV_GROUPS = 4
HEADS_PER_GROUP = N_HEADS // N_KV_GROUPS
GROUP_WIDTH = HEADS_PER_GROUP * HEAD_DIM
CMP_BLOCK = 32
CMP_STRIDE = 16
SEL_BLOCK = 64
N_SELECT = 16
WINDOW = 512
FORCE = 1e4
CONV_WIDTH = 3
MEM_HEADS = 4
REL_BUCKETS = 32
REL_MAX_DIST = 128
EPS = 1e-6
NEG = -1e30

LANES = 128
SUBLANES = 8
BF16_SUBLANES = 16
MXU_COLS = 256
VMEM_BYTES = 64 * 1024 * 1024

T = 256
BLK_PER_T = T // SEL_BLOCK
CMP_PER_T = T // CMP_STRIDE
CMP_NEAR = 2 * CMP_PER_T
N_NEAR = WINDOW // T + 1
N_FORCED = 3
N_BUF = 4
PREFETCH = N_BUF // 2
LONG_TRIP = 3 * N_BUF
CMP_CHUNK = LANES
FFN_TM = 512
ROW_TM = 256
MEM_TM = 512
MERGE_TM = 512
HALO = BF16_SUBLANES
GROUPS_PER_PLANE = LANES // HEAD_DIM
V_ROWS = HEAD_DIM + BF16_SUBLANES
LOG2E = math.log2(math.e)
VMEM_LIMIT = VMEM_BYTES - 12 * 1024 * 1024

F32 = jnp.float32
BF16 = jnp.bfloat16
HI = lax.Precision.HIGHEST


def _dot(a, b):
    return jnp.dot(a, b, preferred_element_type=F32)


def _dot_nt(a, b):
    return lax.dot_general(a, b, (((1,), (1,)), ((), ())), preferred_element_type=F32)


def _rms_rows(xf, g):
    return xf * lax.rsqrt(jnp.mean(xf * xf, axis=-1, keepdims=True) + EPS) * g


def _sigmoid(x):
    return 1.0 / (1.0 + jnp.exp(-x))


def _resident(shape):
    zeros = (0,) * len(shape)
    return pl.BlockSpec(shape, lambda *_: zeros, pipeline_mode=pl.Buffered(1))


def _params(n_axes):
    return pltpu.CompilerParams(dimension_semantics=("arbitrary",) * n_axes,
                                vmem_limit_bytes=VMEM_LIMIT)


def _ffn_kernel(x_ref, g_ref, wa_ref, wb_ref, wo_ref, o_ref, *, ff_bounds):
    x = x_ref[...]
    h = _rms_rows(x, g_ref[...]).astype(BF16)
    acc = jnp.zeros(x.shape, F32)
    for lo, hi in zip(ff_bounds[:-1], ff_bounds[1:]):
        a = _dot(h, wa_ref[:, lo:hi])
        b = _dot(h, wb_ref[:, lo:hi])
        z = (a * _sigmoid(a) * b).astype(BF16)
        acc = acc + _dot(z, wo_ref[lo:hi, :])
    o_ref[...] = x + 0.5 * acc


def _ffn(x2d, g, w_in, w_out):
    n, d = x2d.shape
    d_ff = w_out.shape[0]
    wa = w_in[:, :d_ff].astype(BF16)
    wb = w_in[:, d_ff:].astype(BF16)
    wo = w_out.astype(BF16)
    n_col_tiles = -(-d_ff // MXU_COLS)
    ff_bounds = (0, min(d_ff, (n_col_tiles + 1) // 2 * MXU_COLS), d_ff)
    return pl.pallas_call(
        functools.partial(_ffn_kernel, ff_bounds=ff_bounds),
        grid=(n // FFN_TM,),
        in_specs=[pl.BlockSpec((FFN_TM, d), lambda i: (i, 0)),
                  _resident((1, d)), _resident((d, d_ff)), _resident((d, d_ff)), _resident((d_ff, d))],
        out_specs=pl.BlockSpec((FFN_TM, d), lambda i: (i, 0)),
        out_shape=jax.ShapeDtypeStruct((n, d), F32),
        compiler_params=_params(1),
        name="ffn",
    )(x2d, g.reshape(1, d), wa, wb, wo)


def _proj_rows_kernel(x_ref, g_ref, w_ref, bd_ref, kg_ref,
                      ks_o, kw_o, kc_o, vc_o, conv_o, qm_o, mg_o, *, widths):
    h = _rms_rows(x_ref[0], g_ref[...]).astype(BF16)

    def knorm(k):
        ms = jnp.dot(k * k, bd_ref[...], precision=HI, preferred_element_type=F32)
        return (k * lax.rsqrt(ms + EPS) * kg_ref[...]).astype(BF16)

    lo = 0
    outs = (ks_o, kw_o, kc_o, vc_o, conv_o, qm_o, mg_o)
    for idx, (o_ref, wd) in enumerate(zip(outs, widths)):
        y = _dot(h, w_ref[:, lo:lo + wd])
        if idx < 2:
            o_ref[0] = knorm(y)
        elif idx < 4:
            for plane in range(wd // LANES):
                o_ref[0, plane] = y[:, plane * LANES:(plane + 1) * LANES]
        else:
            o_ref[0] = y.astype(o_ref.dtype)
        lo += wd


def _proj_t_kernel(x_ref, g_ref, wt_ref, qg_ref, q_o, vs_o, vw_o, gt_o, *, d_q, d_kv):
    h = _rms_rows(x_ref[0], g_ref[...]).astype(BF16)
    qg = qg_ref[...] * (HEAD_DIM ** -0.5 * LOG2E)
    out_t = _dot_nt(wt_ref[...], h)
    for hd in range(d_q // HEAD_DIM):
        q = out_t[hd * HEAD_DIM:(hd + 1) * HEAD_DIM, :]
        qn = q * lax.rsqrt(jnp.mean(q * q, axis=0, keepdims=True) + EPS) * qg
        q_o[0, 0, hd * HEAD_DIM:(hd + 1) * HEAD_DIM, :] = qn.astype(BF16)
    lo = d_q
    for v_o in (vs_o, vw_o):
        y = out_t[lo:lo + d_kv, :].astype(BF16)
        for grp in range(d_kv // HEAD_DIM):
            v_o[0, 0, grp * V_ROWS:grp * V_ROWS + HEAD_DIM, :] = y[grp * HEAD_DIM:(grp + 1) * HEAD_DIM]
            v_o[0, 0, grp * V_ROWS + HEAD_DIM:(grp + 1) * V_ROWS, :] = jnp.ones((V_ROWS - HEAD_DIM, T), BF16)
        lo += d_kv
    gt_o[0, 0] = _sigmoid(out_t[lo:, :])


def _compress_kernel(c_ref, pe_ref, w1_ref, w2_ref, kg_ref, o_ref, *, is_key, hidden):
    n_chunks = c_ref.shape[2] // CMP_STRIDE
    a = jnp.zeros((n_chunks, GROUPS_PER_PLANE * hidden), F32)
    b = jnp.zeros((n_chunks, GROUPS_PER_PLANE * hidden), F32)
    for pos in range(CMP_STRIDE):
        x = c_ref[0, 0, pl.ds(pos, n_chunks, stride=CMP_STRIDE), :]
        a = a + _dot((x + pe_ref[pos:pos + 1, :]).astype(BF16), w1_ref[pos])
        b = b + _dot((x + pe_ref[CMP_STRIDE + pos:CMP_STRIDE + pos + 1, :]).astype(BF16), w1_ref[CMP_STRIDE + pos])
    hid = a + pltpu.roll(b, n_chunks - 1, 0)
    hid = (hid * _sigmoid(hid)).astype(BF16)
    for grp in range(GROUPS_PER_PLANE):
        hid_g = hid[:, grp * hidden:(grp + 1) * hidden]
        if is_key:
            y = _dot(hid_g, w2_ref[...])
            y = _rms_rows(y, kg_ref[...])
            row = lax.broadcasted_iota(jnp.int32, y.shape, 0)
            o_ref[0, grp] = jnp.where(row < n_chunks - 1, y, 0.0).astype(BF16)
        else:
            y = _dot_nt(w2_ref[...], hid_g)
            col = lax.broadcasted_iota(jnp.int32, y.shape, 1)
            o_ref[0, grp] = jnp.where(col < n_chunks - 1, y, 0.0).astype(BF16)


def _compress(c_planes, pe, w1, w2, k_gain, is_key):
    b, planes, s, _ = c_planes.shape
    n_chunks = s // CMP_STRIDE
    hidden = w1.shape[1]
    w1_pos = w1.reshape(CMP_BLOCK, HEAD_DIM, hidden)
    zeros = jnp.zeros_like(w1_pos)
    w1_bd = jnp.concatenate([jnp.concatenate([w1_pos, zeros], axis=2),
                             jnp.concatenate([zeros, w1_pos], axis=2)], axis=1).astype(BF16)
    pe_planes = jnp.tile(pe, (1, GROUPS_PER_PLANE))
    w2_arg = w2.astype(BF16) if is_key else w2.T.astype(BF16)
    out_tail = (n_chunks, HEAD_DIM) if is_key else (HEAD_DIM, n_chunks)
    return pl.pallas_call(
        functools.partial(_compress_kernel, is_key=is_key, hidden=hidden),
        grid=(b, planes),
        in_specs=[pl.BlockSpec((1, 1, s, LANES), lambda i, j: (i, j, 0, 0)),
                  _resident(pe_planes.shape), _resident(w1_bd.shape),
                  _resident(w2_arg.shape), _resident((1, HEAD_DIM))],
        out_specs=pl.BlockSpec((1, GROUPS_PER_PLANE) + out_tail, lambda i, j: (i, j, 0, 0)),
        out_shape=jax.ShapeDtypeStruct((b, planes * GROUPS_PER_PLANE) + out_tail, BF16),
        compiler_params=_params(2),
        name="compress_k" if is_key else "compress_v",
    )(c_planes, pe_planes, w1_bd, w2_arg, k_gain.reshape(1, HEAD_DIM))


def _bias_kernel(rb_ref, bkt_near_ref, bkt_cmp_ref, near_o, cmp_o):
    h = pl.program_id(0)
    far = rb_ref[REL_BUCKETS - 1, h]

    def lookup(bkt):
        out = jnp.zeros(bkt.shape, F32)
        for k in range(REL_BUCKETS - 1):
            out = jnp.where(bkt == k, (rb_ref[k, h] - far) * LOG2E, out)
        return out

    key = lax.broadcasted_iota(jnp.int32, (T, T), 0)
    qry = lax.broadcasted_iota(jnp.int32, (T, T), 1)
    near_o[0, 0] = jnp.where(qry >= key, lookup(bkt_near_ref[0]), NEG)
    near_o[0, 1] = lookup(bkt_near_ref[1])
    near_o[0, 2] = jnp.where(key > qry, 0.0, NEG)
    cmp_o[0] = lookup(bkt_cmp_ref[...])


def _rel_bucket(dist):
    n = jnp.maximum(dist, 0)
    max_exact = REL_BUCKETS // 2
    nf = jnp.maximum(n, 1).astype(F32)
    large = max_exact + (jnp.log(nf / max_exact) / math.log(REL_MAX_DIST / max_exact)
                         * (REL_BUCKETS - max_exact)).astype(jnp.int32)
    large = jnp.minimum(large, REL_BUCKETS - 1)
    return jnp.where(n < max_exact, n, large)


def _bias_tiles(rel_bias):
    key = jnp.arange(T)[:, None]
    qry = jnp.arange(T)[None, :]
    bkt_near = jnp.stack([_rel_bucket(qry - key), _rel_bucket(qry - key + T)]).astype(jnp.int32)
    j = jnp.arange(CMP_NEAR)[:, None]
    bkt_cmp = _rel_bucket(qry - CMP_STRIDE * (j - CMP_PER_T) - (CMP_BLOCK - 1)).astype(jnp.int32)
    return pl.pallas_call(
        _bias_kernel,
        grid=(N_HEADS,),
        in_specs=[pl.BlockSpec(memory_space=pltpu.SMEM),
                  pl.BlockSpec((2, T, T), lambda h: (0, 0, 0)),
                  pl.BlockSpec((CMP_NEAR, T), lambda h: (0, 0))],
        out_specs=[pl.BlockSpec((1, N_NEAR, T, T), lambda h: (h // HEADS_PER_GROUP, 0, 0, h % HEADS_PER_GROUP)),
                   pl.BlockSpec((1, CMP_NEAR, T), lambda h: (h // HEADS_PER_GROUP, 0, h % HEADS_PER_GROUP))],
        out_shape=[jax.ShapeDtypeStruct((N_KV_GROUPS, N_NEAR, T, HEADS_PER_GROUP * T), F32),
                   jax.ShapeDtypeStruct((N_KV_GROUPS, CMP_NEAR, HEADS_PER_GROUP * T), F32)],
        compiler_params=_params(1),
        name="bias_tiles",
    )(rel_bias, bkt_near, bkt_cmp)


def _nsa_kernel(q_ref, ks_ref, vs_ref, kw_ref, vw_ref, kc_ref, vc_ref, gt_ref, nb_ref, cb_ref,
                o_ref,
                qcat_ref, sc_ref, psum_ref, selneg_ref, oc_ref, *rest, n_blk, n_cmp):
    s_bufs, rest = rest[:N_BUF], rest[N_BUF:]
    p_bufs, rest = rest[:N_BUF], rest[N_BUF:]
    w_bufs, rest = rest[:N_NEAR], rest[N_NEAR:]
    ms_ref, accs_ref, mw_ref, accw_ref = rest
    g = pl.program_id(1)
    qi = pl.program_id(2)
    R = HEADS_PER_GROUP
    W = R * T
    qry = lax.broadcasted_iota(jnp.int32, (1, W), 1) & (T - 1)
    t = qi * T + qry

    def q_head(r):
        return q_ref[0, 0, r * HEAD_DIM:(r + 1) * HEAD_DIM, :]

    rowgrp = lax.shift_right_logical(lax.broadcasted_iota(jnp.int32, (GROUP_WIDTH, T), 0),
                                     int(math.log2(HEAD_DIM)))
    for r in range(R):
        q4 = jnp.concatenate([q_head(r).astype(F32)] * N_KV_GROUPS, axis=0)
        qcat_ref[:, r * T:(r + 1) * T] = jnp.where(rowgrp == g, q4, 0.0).astype(BF16)

    for m_ref, acc_ref in ((ms_ref, accs_ref), (mw_ref, accw_ref)):
        m_ref[...] = jnp.full(m_ref.shape, NEG, F32)
        acc_ref[...] = jnp.zeros(acc_ref.shape, F32)

    def softmax_tile(s_ref, p_ref, m_ref, rows):
        top = None
        for j in range(BLK_PER_T):
            blk_max = s_ref[j * SEL_BLOCK:(j + 1) * SEL_BLOCK, :].reshape(SEL_BLOCK // SUBLANES, SUBLANES, W).max(axis=0)
            if rows is not None:
                blk_max = blk_max + rows[j]
            top = blk_max if top is None else jnp.maximum(top, blk_max)
        m_old = m_ref[...]
        m_new = jnp.maximum(m_old, jnp.max(top, axis=0, keepdims=True))
        for j in range(BLK_PER_T):
            shift = m_new if rows is None else m_new - rows[j]
            sl = slice(j * SEL_BLOCK, (j + 1) * SEL_BLOCK)
            p_ref[sl, :] = jnp.exp2(s_ref[sl, :] - shift).astype(BF16)
        m_ref[...] = m_new
        return jnp.exp2(m_old - m_new)

    neg_row = jnp.full((1, W), NEG, F32)

    def tile_of(y):
        return jnp.where(y == 0, qi, jnp.where(y == 1, jnp.maximum(qi - 1, 0), jnp.clip(y - 2, 0, qi)))

    win_tiles = [jnp.maximum(qi - back, 0) for back in range(N_NEAR)]
    for back in range(N_NEAR):
        w_bufs[back][...] = _dot(kw_ref[0, win_tiles[back]], qcat_ref[...]) + nb_ref[0, back]
    near0 = pl.multiple_of(qi * CMP_PER_T, CMP_PER_T)
    sc_ref[0:CMP_PER_T, :] = jnp.zeros((CMP_PER_T, W), F32)
    sc_ref[CMP_PER_T:, :] = _dot(kc_ref[0, 0], jnp.concatenate([q_head(r) for r in range(R)], axis=1))
    sc_ref[pl.ds(near0, CMP_NEAR), :] = sc_ref[pl.ds(near0, CMP_NEAR), :] + cb_ref[0]
    for y in range(PREFETCH):
        s = _dot(ks_ref[0, tile_of(y)], qcat_ref[...])
        s_bufs[y][...] = s + nb_ref[0, y] if y < 2 else s

    for back in range(N_NEAR):
        rows = None if back == 0 else [jnp.where(qi >= back, 0.0, neg_row)] * BLK_PER_T
        alpha = softmax_tile(w_bufs[back], p_bufs[back], mw_ref, rows)
        accw_ref[...] = alpha * accw_ref[...] + _dot(vw_ref[0, win_tiles[back], 0], p_bufs[back][...])

    def select_blocks(nc, nb):
        last_c = jnp.minimum(lax.shift_right_arithmetic(t - (CMP_BLOCK - 1), int(math.log2(CMP_STRIDE))),
                             n_cmp - 2)
        c_idx = lax.broadcasted_iota(jnp.int32, (nc, W), 0)
        s = jnp.where(c_idx <= last_c, sc_ref[CMP_PER_T:CMP_PER_T + nc, :], NEG)
        m = jnp.max(s, axis=0, keepdims=True)
        p = jnp.exp2(s - m)
        l = jnp.sum(p, axis=0, keepdims=True)
        p = p * jnp.where(last_c >= 0, 1.0 / l, 0.0)
        oc_ref[...] = _dot(vc_ref[0, 0, :, 0:nc], p.astype(BF16))
        psum = p[:, 0:T]
        for r in range(1, R):
            psum = psum + p[:, r * T:(r + 1) * T]

        for ln in range(T // LANES):
            psum_ref[ln, 0:8, :] = jnp.zeros((8, LANES), F32)
            psum_ref[ln, 8:8 + nc, :] = psum[:, ln * LANES:(ln + 1) * LANES]

        def every_fourth(off):
            return jnp.concatenate([psum_ref[ln, pl.ds(8 + off, nb, stride=SEL_BLOCK // CMP_STRIDE), :]
                                    for ln in range(T // LANES)], axis=1)

        imp = every_fourth(0) + every_fourth(1) + every_fourth(2) + 0.5 * (every_fourth(3) + every_fourth(-1))
        blk = lax.broadcasted_iota(jnp.int32, (nb, T), 0)
        cur = lax.shift_right_logical(t[:, 0:T], int(math.log2(SEL_BLOCK)))
        forced = (blk == 0) | (blk == cur) | (blk == cur - 1)
        score = jnp.where(forced, -jnp.inf, jnp.where(blk <= cur, imp, -FORCE))
        for _ in range(min(N_SELECT, n_blk) - N_FORCED):
            best = jnp.max(score, axis=0, keepdims=True)
            first = jnp.min(jnp.where(score == best, blk, nb), axis=0, keepdims=True)
            score = jnp.where(blk == first, -jnp.inf, score)
        selneg = jnp.where(score == -jnp.inf, 0.0, NEG)
        selneg_ref[0:nb, :] = jnp.concatenate([selneg] * R, axis=1)

    nq = n_cmp // CMP_PER_T
    n_var = max(1, n_cmp // CMP_CHUNK)
    tiles_per_var = nq // n_var
    for v in range(n_var):
        @pl.when((qi >= v * tiles_per_var) & (qi < (v + 1) * tiles_per_var))
        def _(v=v):
            n_tiles = (v + 1) * tiles_per_var
            select_blocks(n_tiles * CMP_PER_T, n_tiles * BLK_PER_T)

    def sel_rows(y):
        kj = tile_of(y)
        return [selneg_ref[pl.ds(kj * BLK_PER_T + j, 1), :] for j in range(BLK_PER_T)]

    p_bufs[N_BUF - 1][...] = jnp.zeros(p_bufs[0].shape, BF16)

    def positions(first, count, final=False):
        for k in range(count):
            y = first + k
            pv = _dot(vs_ref[0, tile_of(y - 1), 0], p_bufs[(k - 1) % N_BUF][...])
            if not final or k + PREFETCH < count:
                s_bufs[(k + PREFETCH) % N_BUF][...] = _dot(ks_ref[0, tile_of(y + PREFETCH)], qcat_ref[...])
            alpha = softmax_tile(s_bufs[k % N_BUF], p_bufs[k % N_BUF], ms_ref, sel_rows(y))
            accs_ref[...] = alpha * (accs_ref[...] + pv)

    n_long = lax.div(qi, LONG_TRIP)
    n_short = lax.div(qi - n_long * LONG_TRIP, N_BUF)

    def long_trip(i, carry):
        positions(LONG_TRIP * i, LONG_TRIP)
        return carry

    def short_trip(i, carry):
        positions(LONG_TRIP * n_long + N_BUF * i, N_BUF)
        return carry

    lax.fori_loop(0, n_long, long_trip, 0)
    lax.fori_loop(0, n_short, short_trip, 0)
    final_first = LONG_TRIP * n_long + N_BUF * n_short
    for count in range(1, N_BUF + 1):
        @pl.when(qi + 1 - final_first == count)
        def _(count=count):
            positions(final_first, count, final=True)
            accs_ref[...] = accs_ref[...] + _dot(vs_ref[0, tile_of(final_first + count - 1), 0],
                                                 p_bufs[count - 1][...])

    o_s = accs_ref[0:HEAD_DIM, :] / accs_ref[HEAD_DIM:HEAD_DIM + 1, :]
    o_w = accw_ref[0:HEAD_DIM, :] / accw_ref[HEAD_DIM:HEAD_DIM + 1, :]
    outs = []
    for r in range(R):
        head = g * R + r
        sl = slice(r * T, (r + 1) * T)
        g_c = gt_ref[0, 0, pl.ds(head, 1), :]
        g_s = gt_ref[0, 0, pl.ds(N_HEADS + head, 1), :]
        g_w = gt_ref[0, 0, pl.ds(2 * N_HEADS + head, 1), :]
        outs.append(g_c * oc_ref[:, sl] + g_s * o_s[:, sl] + g_w * o_w[:, sl])
    o_ref[0] = jnp.concatenate(outs, axis=0).T.astype(o_ref.dtype)


def _nsa(q_t, ks, vs_t, kw, vw_t, kc, vc_t, gates_t, near_bias, cmp_bias, seq):
    b, nq = q_t.shape[0], q_t.shape[1]
    n_blk = seq // SEL_BLOCK
    n_cmp = seq // CMP_STRIDE
    W = HEADS_PER_GROUP * T
    kv_spec = pl.BlockSpec((1, nq, T, GROUP_WIDTH), lambda i, j, k: (i, 0, 0, 0))
    vt_spec = pl.BlockSpec((1, nq, 1, V_ROWS, T), lambda i, j, k: (i, 0, j, 0, 0))
    stat = pltpu.VMEM((1, W), F32)
    acc = pltpu.VMEM((V_ROWS, W), F32)
    scores = pltpu.VMEM((T, W), F32)
    probs = pltpu.VMEM((T, W), BF16)
    return pl.pallas_call(
        functools.partial(_nsa_kernel, n_blk=n_blk, n_cmp=n_cmp),
        grid=(b, N_KV_GROUPS, nq),
        in_specs=[pl.BlockSpec((1, 1, GROUP_WIDTH, T), lambda i, j, k: (i, k, j, 0)),
                  kv_spec, vt_spec, kv_spec, vt_spec,
                  pl.BlockSpec((1, 1, n_cmp, HEAD_DIM), lambda i, j, k: (i, j, 0, 0)),
                  pl.BlockSpec((1, 1, HEAD_DIM, n_cmp), lambda i, j, k: (i, j, 0, 0)),
                  pl.BlockSpec((1, 1, gates_t.shape[2], T), lambda i, j, k: (i, k, 0, 0)),
                  pl.BlockSpec((1, N_NEAR, T, W), lambda i, j, k: (j, 0, 0, 0)),
                  pl.BlockSpec((1, CMP_NEAR, W), lambda i, j, k: (j, 0, 0))],
        out_specs=pl.BlockSpec((1, T, GROUP_WIDTH), lambda i, j, k: (i, k, j)),
        out_shape=jax.ShapeDtypeStruct((b, seq, N_HEADS * HEAD_DIM), BF16),
        scratch_shapes=[pltpu.VMEM((GROUP_WIDTH, W), BF16),
                        pltpu.VMEM((n_cmp + CMP_PER_T, W), F32),
                        pltpu.VMEM((T // LANES, n_cmp + 8, LANES), F32),
                        pltpu.VMEM((n_blk, W), F32),
                        pltpu.VMEM((HEAD_DIM, W), F32), *([scores] * N_BUF), *([probs] * N_BUF), *([scores] * N_NEAR),
                        stat, acc, stat, acc],
        compiler_params=_params(3),
        name="nsa",
    )(q_t, ks, vs_t, kw, vw_t, kc, vc_t, gates_t, near_bias, cmp_bias)


def _mem_kv_kernel(mem_ref, g_ref, w_ref, kg_ref, k_o, v_o):
    h = _rms_rows(mem_ref[0], g_ref[...]).astype(BF16)
    width = k_o.shape[2]
    hd = width // MEM_HEADS
    k = _dot(h, w_ref[:, :width])
    for i in range(MEM_HEADS):
        k_o[0, :, i * hd:(i + 1) * hd] = _rms_rows(k[:, i * hd:(i + 1) * hd], kg_ref[...]).astype(BF16)
    v_o[0] = _dot(h, w_ref[:, width:]).astype(BF16)


def _mem_attn_kernel(q_ref, k_ref, v_ref, qg_ref, o_ref):
    width = q_ref.shape[2]
    hd = width // MEM_HEADS
    for i in range(MEM_HEADS):
        sl = slice(i * hd, (i + 1) * hd)
        q = (_rms_rows(q_ref[0, :, sl].astype(F32), qg_ref[...]) * (hd ** -0.5)).astype(BF16)
        s = _dot_nt(q, k_ref[0, :, sl])
        p = jnp.exp(s - jnp.max(s, axis=-1, keepdims=True))
        p = p / jnp.sum(p, axis=-1, keepdims=True)
        o_ref[0, :, sl] = _dot(p.astype(BF16), v_ref[0, :, sl]).astype(o_ref.dtype)


def _mem_attention(q_mem, mem, mem_norm_g, w_mem_kv, q_g, k_g):
    b, s, width = q_mem.shape
    n_mem, d = mem.shape[1], mem.shape[2]
    hd = width // MEM_HEADS
    kv_shape = jax.ShapeDtypeStruct((b, n_mem, width), BF16)
    kv_block = pl.BlockSpec((1, n_mem, width), lambda i: (i, 0, 0))
    km, vm = pl.pallas_call(
        _mem_kv_kernel,
        grid=(b,),
        in_specs=[pl.BlockSpec((1, n_mem, d), lambda i: (i, 0, 0)),
                  _resident((1, d)), _resident((d, 2 * width)), _resident((1, hd))],
        out_specs=[kv_block, kv_block],
        out_shape=[kv_shape, kv_shape],
        compiler_params=_params(1),
        name="mem_kv",
    )(mem, mem_norm_g.reshape(1, d), w_mem_kv.astype(BF16), k_g.reshape(1, hd))
    kv_block2 = pl.BlockSpec((1, n_mem, width), lambda i, j: (i, 0, 0))
    return pl.pallas_call(
        _mem_attn_kernel,
        grid=(b, s // MEM_TM),
        in_specs=[pl.BlockSpec((1, MEM_TM, width), lambda i, j: (i, j, 0)),
                  kv_block2, kv_block2, _resident((1, hd))],
        out_specs=pl.BlockSpec((1, MEM_TM, width), lambda i, j: (i, j, 0)),
        out_shape=jax.ShapeDtypeStruct((b, s, width), BF16),
        compiler_params=_params(2),
        name="mem_attn",
    )(q_mem, km, vm, q_g.reshape(1, hd))


def _merge_kernel(x_ref, nsa_ref, mem_ref, cb_ref, cc_ref, cx_ref, hc_ref, hx_ref,
                  g1_ref, g2_ref, g3_ref, cw_ref, bias_ref, wo_ref, o_ref):
    j = pl.program_id(1)

    def f32(ref):
        return ref[0].astype(F32)

    u = f32(cc_ref) * f32(cx_ref)
    halo = jnp.where(j > 0, f32(hc_ref) * f32(hx_ref), 0.0)
    prev1 = halo[HALO - 1:HALO, :]
    prev2 = halo[HALO - 2:HALO - 1, :]
    row = lax.broadcasted_iota(jnp.int32, u.shape, 0)
    u1 = jnp.where(row == 0, prev1, pltpu.roll(u, 1, 0))
    u2 = jnp.where(row == 0, prev2, jnp.where(row == 1, prev1, pltpu.roll(u, 2, 0)))
    y = cw_ref[0:1, :] * u2 + cw_ref[1:2, :] * u1 + cw_ref[2:3, :] * u
    o_conv = f32(cb_ref) * (y + bias_ref[...])
    merged = (_sigmoid(f32(g1_ref)) * f32(nsa_ref) + _sigmoid(f32(g2_ref)) * o_conv
              + _sigmoid(f32(g3_ref)) * f32(mem_ref))
    o_ref[0] = x_ref[0] + _dot(merged.astype(BF16), wo_ref[...])


def _merge(x, o_nsa, o_mem, conv_in, merge_g, conv_w, conv_b, w_out):
    b, s, d = x.shape
    tm = MERGE_TM

    def col(c):
        return pl.BlockSpec((1, tm, d), lambda i, j: (i, j, c))

    def halo(c):
        return pl.BlockSpec((1, HALO, d), lambda i, j: (i, jnp.maximum(j * (tm // HALO) - 1, 0), c))

    return pl.pallas_call(
        _merge_kernel,
        grid=(b, s // tm),
        in_specs=[col(0), col(0), col(0),
                  col(0), col(1), col(2), halo(1), halo(2),
                  col(0), col(1), col(2),
                  _resident((CONV_WIDTH, d)), _resident((1, d)), _resident((d, d))],
        out_specs=col(0),
        out_shape=jax.ShapeDtypeStruct((b, s, d), F32),
        compiler_params=_params(2),
        name="merge",
    )(x, o_nsa, o_mem, conv_in, conv_in, conv_in, conv_in, conv_in,
      merge_g, merge_g, merge_g, conv_w, conv_b.reshape(1, d), w_out.astype(BF16))


def _layer(x, mem, ffn1_norm_g, ffn1_w_in, ffn1_w_out, mix_norm_g, w_in, q_norm_g, k_norm_g,
           cmp_pe_k, cmp_w1_k, cmp_w2_k, cmp_pe_v, cmp_w1_v, cmp_w2_v, conv_w, conv_b,
           mem_norm_g, w_mem_kv, mem_q_norm_g, mem_k_norm_g, w_out,
           ffn2_norm_g, ffn2_w_in, ffn2_w_out, near_bias, cmp_bias):
    b, s, d = x.shape
    assert s % T == 0 and s % MERGE_TM == 0 and (b * s) % FFN_TM == 0
    assert WINDOW == 2 * T and REL_MAX_DIST <= T // 2
    assert SEL_BLOCK == 4 * CMP_STRIDE and CMP_BLOCK == 2 * CMP_STRIDE
    assert N_SELECT > N_FORCED and N_BUF >= N_NEAR and 2 <= PREFETCH < N_BUF and s % ROW_TM == 0
    nq = s // T
    d_q = N_HEADS * HEAD_DIM
    d_kv = N_KV_GROUPS * HEAD_DIM
    d_conv = conv_w.shape[1]
    d_mem = w_mem_kv.shape[1] // 2

    x = _ffn(x.reshape(b * s, d), ffn1_norm_g, ffn1_w_in, ffn1_w_out).reshape(b, s, d)

    o = 0
    w_q = w_in[:, o:o + d_q]; o += d_q
    w_kc, w_vc, w_ks, w_vs, w_kw, w_vw = [w_in[:, o + i * d_kv:o + (i + 1) * d_kv] for i in range(6)]
    o += 6 * d_kv
    w_g = w_in[:, o:o + 3 * N_HEADS]; o += 3 * N_HEADS
    w_conv = w_in[:, o:o + 3 * d_conv]; o += 3 * d_conv
    w_qm = w_in[:, o:o + d_mem]; o += d_mem
    w_mg = w_in[:, o:]

    w_rows = jnp.concatenate([w_ks, w_kw, w_kc, w_vc, w_conv, w_qm, w_mg], axis=1).astype(BF16)
    widths = (d_kv, d_kv, d_kv, d_kv, 3 * d_conv, d_mem, w_mg.shape[1])
    group_of = jnp.arange(d_kv) // HEAD_DIM
    block_diag = (group_of[:, None] == group_of[None, :]).astype(F32) / HEAD_DIM
    k_gain_row = jnp.tile(k_norm_g, N_KV_GROUPS).reshape(1, d_kv)

    def rows_out(wd, dt):
        return (pl.BlockSpec((1, ROW_TM, wd), lambda i, j: (i, j, 0)), jax.ShapeDtypeStruct((b, s, wd), dt))

    def planes_out(wd):
        n_planes = wd // LANES
        return (pl.BlockSpec((1, n_planes, ROW_TM, LANES), lambda i, j: (i, 0, j, 0)),
                jax.ShapeDtypeStruct((b, n_planes, s, LANES), F32))

    specs = [planes_out(wd) if i in (2, 3) else rows_out(wd, BF16) for i, wd in enumerate(widths)]
    ks, kw, kc, vc, conv_in, q_mem, merge_g = pl.pallas_call(
        functools.partial(_proj_rows_kernel, widths=widths),
        grid=(b, s // ROW_TM),
        in_specs=[pl.BlockSpec((1, ROW_TM, d), lambda i, j: (i, j, 0)),
                  _resident((1, d)), _resident(w_rows.shape), _resident((d_kv, d_kv)), _resident((1, d_kv))],
        out_specs=[sp[0] for sp in specs],
        out_shape=[sp[1] for sp in specs],
        compiler_params=_params(2),
        name="proj_rows",
    )(x, mix_norm_g.reshape(1, d), w_rows, block_diag, k_gain_row)

    w_g_t = w_g.reshape(d, N_HEADS, 3).transpose(2, 1, 0).reshape(3 * N_HEADS, d)
    n_gate_rows = 128
    w_g_t = jnp.pad(w_g_t, ((0, n_gate_rows - 3 * N_HEADS), (0, 0)))
    w_t = jnp.concatenate([w_q.T, w_vs.T, w_vw.T, w_g_t], axis=0).astype(BF16)

    def t_out(rows, dt):
        return (pl.BlockSpec((1, 1, rows, T), lambda i, j: (i, j, 0, 0)),
                jax.ShapeDtypeStruct((b, nq, rows, T), dt))

    v_rows = N_KV_GROUPS * V_ROWS
    t_specs = [t_out(d_q, BF16), t_out(v_rows, BF16), t_out(v_rows, BF16), t_out(n_gate_rows, F32)]
    q_t, vs_t, vw_t, gates_t = pl.pallas_call(
        functools.partial(_proj_t_kernel, d_q=d_q, d_kv=d_kv),
        grid=(b, nq),
        in_specs=[pl.BlockSpec((1, T, d), lambda i, j: (i, j, 0)),
                  _resident((1, d)), _resident(w_t.shape), _resident((HEAD_DIM, 1))],
        out_specs=[sp[0] for sp in t_specs],
        out_shape=[sp[1] for sp in t_specs],
        compiler_params=_params(2),
        name="proj_t",
    )(x, mix_norm_g.reshape(1, d), w_t, q_norm_g.reshape(HEAD_DIM, 1))

    k_cmp = _compress(kc, cmp_pe_k, cmp_w1_k, cmp_w2_k, k_norm_g, True)
    v_cmp_t = _compress(vc, cmp_pe_v, cmp_w1_v, cmp_w2_v, k_norm_g, False)

    o_nsa = _nsa(q_t, ks.reshape(b, nq, T, d_kv), vs_t.reshape(b, nq, N_KV_GROUPS, V_ROWS, T),
                 kw.reshape(b, nq, T, d_kv), vw_t.reshape(b, nq, N_KV_GROUPS, V_ROWS, T),
                 k_cmp, v_cmp_t, gates_t, near_bias, cmp_bias, s)

    o_mem = _mem_attention(q_mem, mem, mem_norm_g, w_mem_kv, mem_q_norm_g, mem_k_norm_g)
    x = _merge(x, o_nsa, o_mem, conv_in, merge_g, conv_w, conv_b, w_out)
    x = _ffn(x.reshape(b * s, d), ffn2_norm_g, ffn2_w_in, ffn2_w_out).reshape(b, s, d)
    return x


def kernel(x, mem, ffn1_norm_g, ffn1_w_in, ffn1_w_out, mix_norm_g, w_in, q_norm_g, k_norm_g, cmp_pe_k, cmp_w1_k, cmp_w2_k, cmp_pe_v, cmp_w1_v, cmp_w2_v, conv_w, conv_b, mem_norm_g, w_mem_kv, mem_q_norm_g, mem_k_norm_g, w_out, ffn2_norm_g, ffn2_w_in, ffn2_w_out, rel_bias):
    near_bias, cmp_bias = _bias_tiles(rel_bias)
    for l in range(ffn1_norm_g.shape[0]):
        x = _layer(x, mem, ffn1_norm_g[l], ffn1_w_in[l], ffn1_w_out[l], mix_norm_g[l], w_in[l],
                   q_norm_g[l], k_norm_g[l], cmp_pe_k[l], cmp_w1_k[l], cmp_w2_k[l],
                   cmp_pe_v[l], cmp_w1_v[l], cmp_w2_v[l], conv_w[l], conv_b[l],
                   mem_norm_g[l], w_mem_kv[l], mem_q_norm_g[l], mem_k_norm_g[l], w_out[l],
                   ffn2_norm_g[l], ffn2_w_in[l], ffn2_w_out[l], near_bias, cmp_bias)
    return x
```
